```python
import jax, jax.numpy as jnp
from jax import lax
import numpy as np

D_MODEL = 1024
BATCH = 32
SEQ = 2048
DEPTH = 2
DEC_BATCH = 8
DEC_SEQ = 32
PAST_LEN = 4096

CHUNK = 64
Q_BLOCK = 128
D_PLE = 256
D_CONV = D_MODEL // 2
CONV_W = 3
V_DIM = 64
QK_NOPE = 64
QK_ROPE = 32
QK_DIM = QK_NOPE + QK_ROPE
N_HEADS = (D_MODEL // 2) // V_DIM
D_ATTN = N_HEADS * V_DIM
Q_LORA = 12 * V_DIM
KV_LORA = 4 * V_DIM
D_MIX = D_CONV + D_ATTN
D_FF = -(-8 * D_MODEL // (3 * 256)) * 256
ROPE_THETA = 10000.0
EPS = 1e-6
SCALE = QK_DIM ** -0.5

OFF_B = 0
OFF_C = OFF_B + D_CONV
OFF_X = OFF_C + D_CONV
OFF_Q = OFF_X + D_CONV
OFF_KV = OFF_Q + Q_LORA
OFF_KR = OFF_KV + KV_LORA
IN_COLS = OFF_KR + QK_ROPE

kernel_name = "hymba_conv_mla_streaming_step"


def rms_norm(x, g):
    xf = x.astype(jnp.float32)
    y = xf * lax.rsqrt(jnp.mean(xf * xf, axis=-1, keepdims=True) + EPS)
    return (y * g.astype(jnp.float32)).astype(x.dtype)


def apply_rope(x, pos):
    half = x.shape[-1] // 2
    inv = ROPE_THETA ** (-jnp.arange(half, dtype=jnp.float32) / half)
    ang = pos.astype(jnp.float32)[:, None] * inv[None, :]
    ang = ang.reshape(ang.shape[:1] + (1,) * (x.ndim - 3) + (half,))
    cos, sin = jnp.cos(ang), jnp.sin(ang)
    xf = x.astype(jnp.float32)
    x1, x2 = xf[..., :half], xf[..., half:]
    return jnp.concatenate([x1 * cos - x2 * sin, x1 * sin + x2 * cos], axis=-1).astype(x.dtype)


def attend(q_nope, q_rope, k_nope, k_rope, v, mask):
    s = (jnp.einsum('nqhd,nkhd->nhqk', q_nope, k_nope)
         + jnp.einsum('nqhr,nkr->nhqk', q_rope, k_rope)).astype(jnp.float32) * SCALE
    if mask is not None:
        s = jnp.where(mask, s, jnp.finfo(jnp.float32).min)
    pr = jax.nn.softmax(s, axis=-1).astype(v.dtype)
    return jnp.einsum('nhqk,nkhd->nqhd', pr, v)


def prompt_attention(q_nope, q_rope, k_nope, k_rope, v):
    n, s = q_nope.shape[0], q_nope.shape[1]
    nb = s // Q_BLOCK
    qn = q_nope.reshape(n, nb, Q_BLOCK, N_HEADS, QK_NOPE).transpose(1, 0, 2, 3, 4)
    qr = q_rope.reshape(n, nb, Q_BLOCK, N_HEADS, QK_ROPE).transpose(1, 0, 2, 3, 4)
    key_chunk = jnp.arange(s) // CHUNK

    def block(args):
        qn_b, qr_b, bi = args
        q_chunk = (bi * Q_BLOCK + jnp.arange(Q_BLOCK)) // CHUNK
        mask = key_chunk[None, :] <= q_chunk[:, None]
        return attend(qn_b, qr_b, k_nope, k_rope, v, mask)

    out = lax.map(block, (qn, qr, jnp.arange(nb)))
    return out.transpose(1, 0, 2, 3, 4).reshape(n, s, N_HEADS, V_DIM)


def layer(x, p_i, pos, conv_prev, past_c, past_kr, w):
    (g_mix_pre, w_in, w_conv, g_q, w_uq, g_kv, w_ukv, g_conv_out, g_attn_out, w_o,
     g_mix_post, g_ffn_pre, w_gate, w_up, w_down, g_ffn_post, w_ple_proj, w_ple_gate) = w
    n, l, _ = x.shape
    h = rms_norm(x, g_mix_pre)
    z = h @ w_in
    gb = z[..., OFF_B:OFF_C]
    gc = z[..., OFF_C:OFF_X]
    xin = z[..., OFF_X:OFF_Q]
    cq = z[..., OFF_Q:OFF_KV]
    ckv_raw = z[..., OFF_KV:OFF_KR]
    kr_raw = z[..., OFF_KR:IN_COLS]

    u = gc * xin
    ext = jnp.concatenate([conv_prev.astype(u.dtype), u], axis=1)
    conv = ext[:, 0:l] * w_conv[0]
    for j in range(1, CONV_W):
        conv = conv + ext[:, j:j + l] * w_conv[j]
    conv_out = gb * conv
    new_conv = ext[:, l:]

    q = (rms_norm(cq, g_q) @ w_uq).reshape(n, l, N_HEADS, QK_DIM)
    q_nope = q[..., :QK_NOPE]
    q_rope = apply_rope(q[..., QK_NOPE:], pos)
    c_kv = rms_norm(ckv_raw, g_kv)
    k_rope = apply_rope(kr_raw, pos)
    if past_c is None:
        kv = (c_kv @ w_ukv).reshape(n, l, N_HEADS, QK_NOPE + V_DIM)
        attn = prompt_attention(q_nope, q_rope, kv[..., :QK_NOPE], k_rope, kv[..., QK_NOPE:])
    else:
        c_all = jnp.concatenate([past_c.astype(c_kv.dtype), c_kv], axis=1)
        kr_all = jnp.concatenate([past_kr.astype(k_rope.dtype), k_rope], axis=1)
        kv = (c_all @ w_ukv).reshape(n, c_all.shape[1], N_HEADS, QK_NOPE + V_DIM)
        attn = attend(q_nope, q_rope, kv[..., :QK_NOPE], kr_all, kv[..., QK_NOPE:], None)
    attn = attn.reshape(n, l, D_ATTN)

    mix = jnp.concatenate([rms_norm(conv_out, g_conv_out), rms_norm(attn, g_attn_out)], axis=-1) @ w_o
    x = x + rms_norm(mix, g_mix_post)

    h = rms_norm(x, g_ffn_pre)
    f = (jax.nn.silu(h @ w_gate) * (h @ w_up)) @ w_down
    x = x + rms_norm(f, g_ffn_post)

    x = x + jax.nn.sigmoid(x @ w_ple_gate) * (p_i @ w_ple_proj)
    return x, c_kv, k_rope, new_conv


def setup_inputs(seed: int = 0) -> dict:
    key = jax.random.key(seed)
    ks = iter(jax.random.split(key, 40))

    def nrm(shape, scale=1.0):
        return jax.random.normal(next(ks), shape, jnp.float32) * scale

    def gain(shape):
        return 1.0 + 0.05 * nrm(shape)

    return {
        "x_prompt": nrm((BATCH, SEQ, D_MODEL)),
        "x_sample": nrm((DEC_BATCH, DEC_SEQ, D_MODEL)),
        "cache_kv_latent": nrm((DEPTH, DEC_BATCH, PAST_LEN, KV_LORA)),
        "cache_k_rope": nrm((DEPTH, DEC_BATCH, PAST_LEN, QK_ROPE)),
        "state_conv": nrm((DEPTH, DEC_BATCH, CONV_W - 1, D_CONV)),
        "p_prompt": nrm((DEPTH, BATCH, SEQ, D_PLE)),
        "p_sample": nrm((DEPTH, DEC_BATCH, DEC_SEQ, D_PLE)),
        "g_mix_pre": gain((DEPTH, D_MODEL)),
        "w_in": nrm((DEPTH, D_MODEL, IN_COLS), D_MODEL ** -0.5),
        "w_conv": nrm((DEPTH, CONV_W, D_CONV), CONV_W ** -0.5),
        "g_q": gain((DEPTH, Q_LORA)),
        "w_uq": nrm((DEPTH, Q_LORA, N_HEADS * QK_DIM), Q_LORA ** -0.5),
        "g_kv": gain((DEPTH, KV_LORA)),
        "w_ukv": nrm((DEPTH, KV_LORA, N_HEADS * (QK_NOPE + V_DIM)), KV_LORA ** -0.5),
        "g_conv_out": gain((DEPTH, D_CONV)),
        "g_attn_out": gain((DEPTH, D_ATTN)),
        "w_o": nrm((DEPTH, D_MIX, D_MODEL), D_MIX ** -0.5),
        "g_mix_post": gain((DEPTH, D_MODEL)),
        "g_ffn_pre": gain((DEPTH, D_MODEL)),
        "w_ffn_gate": nrm((DEPTH, D_MODEL, D_FF), D_MODEL ** -0.5),
        "w_ffn_up": nrm((DEPTH, D_MODEL, D_FF), D_MODEL ** -0.5),
        "w_ffn_down": nrm((DEPTH, D_FF, D_MODEL), D_FF ** -0.5),
        "g_ffn_post": gain((DEPTH, D_MODEL)),
        "w_ple_proj": nrm((DEPTH, D_PLE, D_MODEL), D_PLE ** -0.5),
        "w_ple_gate": nrm((DEPTH, D_MODEL, D_MODEL), D_MODEL ** -0.5),
    }


def reference(x_prompt, x_sample, cache_kv_latent, cache_k_rope, state_conv, p_prompt, p_sample,
              g_mix_pre, w_in, w_conv, g_q, w_uq, g_kv, w_ukv, g_conv_out, g_attn_out, w_o,
              g_mix_post, g_ffn_pre, w_ffn_gate, w_ffn_up, w_ffn_down, g_ffn_post,
              w_ple_proj, w_ple_gate):
    n_p, s_p = x_prompt.shape[0], x_prompt.shape[1]
    s_d = x_sample.shape[1]
    past_len = cache_kv_latent.shape[2]
    pos_p = jnp.arange(s_p)
    pos_d = past_len + jnp.arange(s_d)
    conv_zero = jnp.zeros((n_p, CONV_W - 1, D_CONV), x_prompt.dtype)

    xp, xd = x_prompt, x_sample
    lat_p, kr_p, cv_p, lat_d, kr_d, cv_d = [], [], [], [], [], []
    for i in range(DEPTH):
        w = (g_mix_pre[i], w_in[i], w_conv[i], g_q[i], w_uq[i], g_kv[i], w_ukv[i],
             g_conv_out[i], g_attn_out[i], w_o[i], g_mix_post[i], g_ffn_pre[i],
             w_ffn_gate[i], w_ffn_up[i], w_ffn_down[i], g_ffn_post[i],
             w_ple_proj[i], w_ple_gate[i])
        xp, c_p, k_p, s_p_new = layer(xp, p_prompt[i], pos_p, conv_zero, None, None, w)
        xd, c_d, k_d, s_d_new = layer(xd, p_sample[i], pos_d, state_conv[i],
                                      cache_kv_latent[i], cache_k_rope[i], w)
        lat_p.append(c_p); kr_p.append(k_p); cv_p.append(s_p_new)
        lat_d.append(c_d); kr_d.append(k_d); cv_d.append(s_d_new)

    new_kv_latent_prompt = jnp.stack(lat_p)
    new_k_rope_prompt = jnp.stack(kr_p)
    new_conv_prompt = jnp.stack(cv_p)
    new_kv_latent_sample = jnp.stack(lat_d)
    new_k_rope_sample = jnp.stack(kr_d)
    new_conv_sample = jnp.stack(cv_d)
    return (xp, xd, new_kv_latent_prompt, new_k_rope_prompt, new_conv_prompt,
            new_kv_latent_sample, new_k_rope_sample, new_conv_sample)
```

```python
import functools

import jax
import jax.numpy as jnp
from jax import lax
from jax.experimental import pallas as pl
from jax.experimental.pallas import tpu as pltpu

D_MODEL = 1024
CHUNK = 64
D_PLE = 256
D_CONV = 512
CONV_W = 3
V_DIM = 64
QK_NOPE = 64
QK_ROPE = 32
QK_DIM = QK_NOPE + QK_ROPE
N_HEADS = 8
D_ATTN = N_HEADS * V_DIM
Q_LORA = 768
KV_LORA = 256
D_FF = 2816
ROPE_THETA = 10000.0
EPS = 1e-6
SCALE = QK_DIM ** -0.5

OFF_C = D_CONV
OFF_X = 2 * D_CONV
OFF_Q = 3 * D_CONV
OFF_KV = OFF_Q + Q_LORA
OFF_KR = OFF_KV + KV_LORA
IN_COLS = OFF_KR + QK_ROPE

LANES = 128
HEAD_PAD = LANES
D_HEADS_PAD = N_HEADS * HEAD_PAD
IN_EXT = OFF_KR + LANES
FF_CHUNK = 256
NEG_BIG = -1e30

BF16 = jnp.bfloat16
F32 = jnp.float32


def _rms(x, g):
    ms = jnp.mean(x * x, axis=-1, keepdims=True)
    return x * lax.rsqrt(ms + EPS) * g


def _dot(a, b):
    return jnp.dot(a, b, preferred_element_type=F32)


def _dot_nt(a, b):
    return lax.dot_general(a, b, (((1,), (1,)), ((), ())), preferred_element_type=F32)


def _const_spec(shape):
    nd = len(shape)
    return pl.BlockSpec(shape, lambda *_: (0,) * nd, pipeline_mode=pl.Buffered(1))


def _in_proj_kernel(x_ref, tab_ref, cinit_ref, gpre_ref, win_ref, wconv_ref, gq_ref, wq_ref,
                    gkv_ref, wk_ref, wv_ref, gconv_ref,
                    convn_ref, q_ref, k_ref, v_ref, lat_ref, kr_ref, newconv_ref,
                    carry_ref):
    ts = x_ref.shape[1]
    s_idx = pl.program_id(1)

    @pl.when(s_idx == 0)
    def _():
        carry_ref[0:2, :] = cinit_ref[0]

    h = _rms(x_ref[0], gpre_ref[...]).astype(BF16)

    gb = _dot(h, win_ref[:, 0:OFF_C])
    gc = _dot(h, win_ref[:, OFF_C:OFF_X])
    xin = _dot(h, win_ref[:, OFF_X:OFF_Q])
    u = gc * xin
    row = lax.broadcasted_iota(jnp.int32, (ts, 1), 0)
    c0 = carry_ref[0:1, :]
    c1 = carry_ref[1:2, :]
    u_m1 = jnp.where(row == 0, c1, pltpu.roll(u, 1, axis=0))
    u_m2 = pltpu.roll(u, 2, axis=0)
    u_m2 = jnp.where(row == 0, c0, jnp.where(row == 1, c1, u_m2))
    conv = u_m2 * wconv_ref[0:1, :] + u_m1 * wconv_ref[1:2, :] + u * wconv_ref[2:3, :]
    convn_ref[0] = _rms(gb * conv, gconv_ref[...]).astype(BF16)
    tail = u[ts - 2:ts, :]
    carry_ref[0:2, :] = tail
    newconv_ref[0] = tail

    c_q = tab_ref[0]
    s_q = tab_ref[1]
    c_k = tab_ref[2]

    cqn = _rms(_dot(h, win_ref[:, OFF_Q:OFF_KV]), gq_ref[...]).astype(BF16)
    qa = _dot(cqn, wq_ref[...])
    for hd in range(N_HEADS):
        blk = qa[:, hd * HEAD_PAD:(hd + 1) * HEAD_PAD]
        rot = blk * c_q + pltpu.roll(blk, HEAD_PAD - QK_ROPE, axis=1) * s_q
        q_ref[0, :, hd * HEAD_PAD:(hd + 1) * HEAD_PAD] = rot.astype(BF16)

    c_kv = _rms(_dot(h, win_ref[:, OFF_KV:OFF_KR]), gkv_ref[...])
    lat_ref[0] = c_kv
    ckvb = c_kv.astype(BF16)
    zkr = _dot(h, win_ref[:, OFF_KR:IN_EXT])
    kr_blk = zkr * c_k + pltpu.roll(zkr, HEAD_PAD - QK_ROPE, axis=1) * s_q
    kr_ref[0] = kr_blk
    kn = _dot(ckvb, wk_ref[...])
    for hd in range(N_HEADS):
        k_ref[0, :, hd * HEAD_PAD:(hd + 1) * HEAD_PAD] = (
            kn[:, hd * HEAD_PAD:(hd + 1) * HEAD_PAD] + kr_blk).astype(BF16)
    v_ref[0] = _dot(ckvb, wv_ref[...]).astype(BF16)


def _in_proj(x, tables, conv_init, lw, ts):
    n, s, _ = x.shape
    grid = (n, s // ts)
    tok = lambda w: pl.BlockSpec((1, ts, w), lambda b, i: (b, i, 0))
    out_shape = (
        jax.ShapeDtypeStruct((n, s, D_CONV), BF16),
        jax.ShapeDtypeStruct((n, s, D_HEADS_PAD), BF16),
        jax.ShapeDtypeStruct((n, s, D_HEADS_PAD), BF16),
        jax.ShapeDtypeStruct((n, s, D_HEADS_PAD), BF16),
        jax.ShapeDtypeStruct((n, s, KV_LORA), F32),
        jax.ShapeDtypeStruct((n, s, LANES), F32),
        jax.ShapeDtypeStruct((n, CONV_W - 1, D_CONV), F32),
    )
    in_specs = [
        tok(D_MODEL),
        pl.BlockSpec((3, ts, LANES), lambda b, i: (0, i, 0)),
        pl.BlockSpec((1, CONV_W - 1, D_CONV), lambda b, i: (b, 0, 0)),
        _const_spec((1, D_MODEL)),
        _const_spec((D_MODEL, IN_EXT)),
        _const_spec((CONV_W, D_CONV)),
        _const_spec((1, Q_LORA)),
        _const_spec((Q_LORA, D_HEADS_PAD)),
        _const_spec((1, KV_LORA)),
        _const_spec((KV_LORA, D_HEADS_PAD)),
        _const_spec((KV_LORA, D_HEADS_PAD)),
        _const_spec((1, D_CONV)),
    ]
    out_specs = (
        tok(D_CONV), tok(D_HEADS_PAD), tok(D_HEADS_PAD), tok(D_HEADS_PAD), tok(KV_LORA), tok(LANES),
        pl.BlockSpec((1, CONV_W - 1, D_CONV), lambda b, i: (b, 0, 0)),
    )
    return pl.pallas_call(
        _in_proj_kernel,
        grid=grid,
        in_specs=in_specs,
        out_specs=out_specs,
        out_shape=out_shape,
        scratch_shapes=[pltpu.VMEM((8, D_CONV), F32)],
        compiler_params=pltpu.CompilerParams(
            dimension_semantics=("arbitrary", "arbitrary"),
            vmem_limit_bytes=48 * 1024 * 1024),
        name="in_proj",
    )(x, tables, conv_init, lw["g_mix_pre"], lw["w_in"], lw["w_conv"], lw["g_q"], lw["w_q"],
      lw["g_kv"], lw["w_k"], lw["w_v"], lw["g_conv_out"])


def _attn_prompt_kernel(q_ref, k_ref, v_ref, g_ref, o_ref, acc_ref, *, tq, tk):
    qi = pl.program_id(1)
    n_full = qi * (tq // tk)
    n_diag = tq // tk

    def step(qh, hd, kb, carry, masked):
        m, l, acc = carry
        k0 = pl.multiple_of(kb * tk, tk)
        kblk = k_ref[0, pl.ds(k0, tk), hd * HEAD_PAD:(hd + 1) * HEAD_PAD]
        vblk = v_ref[0, pl.ds(k0, tk), hd * HEAD_PAD:(hd + 1) * HEAD_PAD]
        s = _dot_nt(qh, kblk) * SCALE
        if masked:
            rq = (lax.broadcasted_iota(jnp.int32, (tq, tk), 0) + qi * tq) // CHUNK
            ck = (lax.broadcasted_iota(jnp.int32, (tq, tk), 1) + kb * tk) // CHUNK
            s = jnp.where(ck <= rq, s, NEG_BIG)
        m_new = jnp.maximum(m, jnp.max(s, axis=-1, keepdims=True))
        alpha = jnp.exp(m - m_new)
        p = jnp.exp(s - m_new)
        l = alpha * l + jnp.sum(p, axis=-1, keepdims=True)
        acc = alpha * acc + _dot(p.astype(BF16), vblk)
        return m_new, l, acc

    ssq = jnp.zeros((tq, 1), F32)
    for hd in range(N_HEADS):
        qh = q_ref[0, :, hd * HEAD_PAD:(hd + 1) * HEAD_PAD]
        carry = (jnp.full((tq, 1), NEG_BIG, F32), jnp.zeros((tq, 1), F32),
                 jnp.zeros((tq, HEAD_PAD), F32))
        carry = lax.fori_loop(
            0, n_full, lambda kb, c: step(qh, hd, kb, c, False), carry)
        for d in range(n_diag):
            carry = step(qh, hd, n_full + d, carry, True)
        _, l, acc = carry
        o = acc / l
        ssq = ssq + jnp.sum(o * o, axis=-1, keepdims=True)
        acc_ref[:, hd * HEAD_PAD:(hd + 1) * HEAD_PAD] = o
    rs = lax.rsqrt(ssq * (1.0 / D_ATTN) + EPS)
    o_ref[0] = (acc_ref[...] * rs * g_ref[...]).astype(BF16)


def _attn_prompt(q, k, v, g_attn, tq, tk):
    n, s, _ = q.shape
    return pl.pallas_call(
        functools.partial(_attn_prompt_kernel, tq=tq, tk=tk),
        grid=(n, s // tq),
        in_specs=[
            pl.BlockSpec((1, tq, D_HEADS_PAD), lambda b, i: (b, i, 0)),
            pl.BlockSpec((1, s, D_HEADS_PAD), lambda b, i: (b, 0, 0)),
            pl.BlockSpec((1, s, D_HEADS_PAD), lambda b, i: (b, 0, 0)),
            _const_spec((1, D_HEADS_PAD)),
        ],
        out_specs=pl.BlockSpec((1, tq, D_HEADS_PAD), lambda b, i: (b, i, 0)),
        out_shape=jax.ShapeDtypeStruct((n, s, D_HEADS_PAD), BF16),
        scratch_shapes=[pltpu.VMEM((tq, D_HEADS_PAD), F32)],
        compiler_params=pltpu.CompilerParams(
            dimension_semantics=("arbitrary", "arbitrary"),
            vmem_limit_bytes=48 * 1024 * 1024),
        name="attn_prompt",
    )(q, k, v, g_attn)


def _attn_sample_kernel(q_ref, cpast_ref, krpast_ref, cnew_ref, krnew_ref, wabs_ref, psel_ref,
                        wuv_ref, g_ref, o_ref):
    l_new = q_ref.shape[1]
    q = q_ref[0]
    ql, qr = [], []
    for hd in range(N_HEADS):
        qh = q[:, hd * HEAD_PAD:(hd + 1) * HEAD_PAD]
        ql.append(_dot(qh, wabs_ref[hd]))
        qr.append(_dot(qh, psel_ref[...]))
    ql = jnp.concatenate(ql, axis=0).astype(BF16)
    qr = jnp.concatenate(qr, axis=0).astype(BF16)
    cp = cpast_ref[0].astype(BF16)
    cn = cnew_ref[0].astype(BF16)
    s_p = (_dot_nt(ql, cp) + _dot_nt(qr, krpast_ref[0].astype(BF16))) * SCALE
    s_n = (_dot_nt(ql, cn) + _dot_nt(qr, krnew_ref[0].astype(BF16))) * SCALE
    m = jnp.maximum(jnp.max(s_p, axis=-1, keepdims=True), jnp.max(s_n, axis=-1, keepdims=True))
    p_p = jnp.exp(s_p - m)
    p_n = jnp.exp(s_n - m)
    l = jnp.sum(p_p, axis=-1, keepdims=True) + jnp.sum(p_n, axis=-1, keepdims=True)
    o_lat = ((_dot(p_p.astype(BF16), cp) + _dot(p_n.astype(BF16), cn)) / l).astype(BF16)
    outs = []
    ssq = jnp.zeros((l_new, 1), F32)
    for hd in range(N_HEADS):
        o = _dot(o_lat[hd * l_new:(hd + 1) * l_new, :], wuv_ref[hd])
        ssq = ssq + jnp.sum(o * o, axis=-1, keepdims=True)
        outs.append(o)
    rs = lax.rsqrt(ssq * (1.0 / D_ATTN) + EPS)
    for hd in range(N_HEADS):
        o_ref[0, :, hd * HEAD_PAD:(hd + 1) * HEAD_PAD] = (
            outs[hd] * rs * g_ref[:, hd * HEAD_PAD:(hd + 1) * HEAD_PAD]).astype(BF16)


def _attn_sample(q, c_past, kr_past, c_new, kr_new, lw):
    n, l_new, _ = q.shape
    p_len = c_past.shape[1]
    per_b = lambda rows, w: pl.BlockSpec((1, rows, w), lambda b: (b, 0, 0))
    return pl.pallas_call(
        _attn_sample_kernel,
        grid=(n,),
        in_specs=[
            per_b(l_new, D_HEADS_PAD),
            per_b(p_len, KV_LORA),
            per_b(p_len, QK_ROPE),
            per_b(l_new, KV_LORA),
            per_b(l_new, QK_ROPE),
            _const_spec((N_HEADS, HEAD_PAD, KV_LORA)),
            _const_spec((HEAD_PAD, QK_ROPE)),
            _const_spec((N_HEADS, KV_LORA, HEAD_PAD)),
            _const_spec((1, D_HEADS_PAD)),
        ],
        out_specs=per_b(l_new, D_HEADS_PAD),
        out_shape=jax.ShapeDtypeStruct((n, l_new, D_HEADS_PAD), BF16),
        compiler_params=pltpu.CompilerParams(
            dimension_semantics=("arbitrary",),
            vmem_limit_bytes=48 * 1024 * 1024),
        name="attn_sample",
    )(q, c_past, kr_past, c_new, kr_new, lw["w_abs"], lw["p_sel"], lw["w_uv"], lw["g_attn"])


def _out_ffn_kernel(x_ref, convn_ref, attn_ref, p_ref, woc_ref, woa_ref, gpost_ref, gffn_ref,
                    wg_ref, wu_ref, wd_ref, gfpost_ref, wpg_ref, wpp_ref, y_ref, act_ref):
    mix = _dot(convn_ref[...], woc_ref[...]) + _dot(attn_ref[...], woa_ref[...])
    x1 = x_ref[...] + _rms(mix, gpost_ref[...])
    h = _rms(x1, gffn_ref[...]).astype(BF16)
    for c in range(D_FF // FF_CHUNK):
        cols = slice(c * FF_CHUNK, (c + 1) * FF_CHUNK)
        g = _dot(h, wg_ref[:, cols])
        u = _dot(h, wu_ref[:, cols])
        act_ref[:, cols] = (g * jax.nn.sigmoid(g) * u).astype(BF16)
    f = _dot(act_ref[...], wd_ref[...])
    x2 = x1 + _rms(f, gfpost_ref[...])
    gate = jax.nn.sigmoid(_dot(x2.astype(BF16), wpg_ref[...]))
    y_ref[...] = x2 + gate * _dot(p_ref[...].astype(BF16), wpp_ref[...])


def _out_ffn(x, convn, attn, p, lw, tm):
    m = x.shape[0]
    tok = lambda w: pl.BlockSpec((tm, w), lambda i: (i, 0))
    return pl.pallas_call(
        _out_ffn_kernel,
        grid=(m // tm,),
        in_specs=[
            tok(D_MODEL), tok(D_CONV), tok(D_HEADS_PAD), tok(D_PLE),
            _const_spec((D_CONV, D_MODEL)),
            _const_spec((D_HEADS_PAD, D_MODEL)),
            _const_spec((1, D_MODEL)),
            _const_spec((1, D_MODEL)),
            _const_spec((D_MODEL, D_FF)),
            _const_spec((D_MODEL, D_FF)),
            _const_spec((D_FF, D_MODEL)),
            _const_spec((1, D_MODEL)),
            _const_spec((D_MODEL, D_MODEL)),
            _const_spec((D_PLE, D_MODEL)),
        ],
        out_specs=tok(D_MODEL),
        out_shape=jax.ShapeDtypeStruct((m, D_MODEL), F32),
        scratch_shapes=[pltpu.VMEM((tm, D_FF), BF16)],
        compiler_params=pltpu.CompilerParams(
            dimension_semantics=("arbitrary",),
            vmem_limit_bytes=56 * 1024 * 1024),
        name="out_ffn",
    )(x, convn, attn, p, lw["w_o_conv"], lw["w_o_attn"], lw["g_mix_post"], lw["g_ffn_pre"],
      lw["w_gate"], lw["w_up"], lw["w_down"], lw["g_ffn_post"], lw["w_ple_gate"], lw["w_ple_proj"])


def _swap_halves(a):
    half = a.shape[-1] // 2
    return jnp.concatenate([a[..., half:], a[..., :half]], axis=-1)


def _layer_weights(i, g_mix_pre, w_in, w_conv, g_q, w_uq, g_kv, w_ukv, g_conv_out, g_attn_out, w_o,
                   g_mix_post, g_ffn_pre, w_ffn_gate, w_ffn_up, w_ffn_down, g_ffn_post,
                   w_ple_proj, w_ple_gate):
    row = lambda g: g[i].reshape(1, -1).astype(F32)
    w_kr = w_in[i][:, OFF_KR:IN_COLS]
    w_in_ext = jnp.concatenate(
        [w_in[i][:, :OFF_KR], jnp.zeros((D_MODEL, QK_NOPE), F32), w_kr, _swap_halves(w_kr)], axis=1)
    wq3 = w_uq[i].reshape(Q_LORA, N_HEADS, QK_DIM)
    w_q = jnp.concatenate(
        [wq3, _swap_halves(wq3[..., QK_NOPE:])], axis=-1).reshape(Q_LORA, D_HEADS_PAD)
    wkv3 = w_ukv[i].reshape(KV_LORA, N_HEADS, QK_NOPE + V_DIM)
    w_uk = wkv3[..., :QK_NOPE]
    w_uv = wkv3[..., QK_NOPE:]
    zpad = jnp.zeros((KV_LORA, N_HEADS, HEAD_PAD - QK_NOPE), F32)
    w_k = jnp.concatenate([w_uk, zpad], axis=-1).reshape(KV_LORA, D_HEADS_PAD)
    w_v = jnp.concatenate([w_uv, zpad], axis=-1).reshape(KV_LORA, D_HEADS_PAD)
    w_abs = jnp.concatenate([w_uk, zpad], axis=-1).transpose(1, 2, 0)
    w_uv_pad = jnp.concatenate([w_uv, zpad], axis=-1).transpose(1, 0, 2)
    p_sel = jnp.zeros((HEAD_PAD, QK_ROPE), F32).at[QK_NOPE:QK_DIM, :].set(jnp.eye(QK_ROPE, dtype=F32))
    g_attn = jnp.concatenate(
        [g_attn_out[i].reshape(N_HEADS, V_DIM), jnp.zeros((N_HEADS, HEAD_PAD - V_DIM), F32)],
        axis=-1).reshape(1, D_HEADS_PAD)
    w_oa = w_o[i][D_CONV:].reshape(N_HEADS, V_DIM, D_MODEL)
    w_o_attn = jnp.concatenate(
        [w_oa, jnp.zeros((N_HEADS, HEAD_PAD - V_DIM, D_MODEL), F32)], axis=1).reshape(D_HEADS_PAD, D_MODEL)
    return dict(
        g_mix_pre=row(g_mix_pre), w_in=w_in_ext.astype(BF16), w_conv=w_conv[i].astype(F32),
        g_q=row(g_q), w_q=w_q.astype(BF16), g_kv=row(g_kv), w_k=w_k.astype(BF16), w_v=w_v.astype(BF16),
        g_conv_out=row(g_conv_out), g_attn=g_attn.astype(F32),
        w_abs=w_abs.astype(BF16), w_uv=w_uv_pad.astype(BF16), p_sel=p_sel.astype(BF16),
        w_o_conv=w_o[i][:D_CONV].astype(BF16), w_o_attn=w_o_attn.astype(BF16),
        g_mix_post=row(g_mix_post), g_ffn_pre=row(g_ffn_pre),
        w_gate=w_ffn_gate[i].astype(BF16), w_up=w_ffn_up[i].astype(BF16), w_down=w_ffn_down[i].astype(BF16),
        g_ffn_post=row(g_ffn_post), w_ple_gate=w_ple_gate[i].astype(BF16), w_ple_proj=w_ple_proj[i].astype(BF16),
    )


def _rope_tables(pos):
    half = QK_ROPE // 2
    inv = ROPE_THETA ** (-jnp.arange(half, dtype=F32) / half)
    ang = pos.astype(F32)[:, None] * inv[None, :]
    cos, sin = jnp.cos(ang), jnp.sin(ang)
    n = pos.shape[0]
    cc = jnp.concatenate([cos, cos], axis=1)
    ss = jnp.concatenate([-sin, sin], axis=1)
    z_lo = jnp.zeros((n, QK_NOPE), F32)
    z_hi = jnp.zeros((n, HEAD_PAD - QK_DIM), F32)
    c_q = jnp.concatenate([jnp.ones((n, QK_NOPE), F32), cc, z_hi], axis=1)
    s_q = jnp.concatenate([z_lo, ss, z_hi], axis=1)
    c_k = jnp.concatenate([z_lo, cc, z_hi], axis=1)
    return jnp.stack([c_q, s_q, c_k])


def _pick_tile(n, cap):
    t = min(n, cap)
    while n % t:
        t //= 2
    return t


def kernel(x_prompt, x_sample, cache_kv_latent, cache_k_rope, state_conv, p_prompt, p_sample, g_mix_pre, w_in, w_conv, g_q, w_uq, g_kv, w_ukv, g_conv_out, g_attn_out, w_o, g_mix_post, g_ffn_pre, w_ffn_gate, w_ffn_up, w_ffn_down, g_ffn_post, w_ple_proj, w_ple_gate):
    depth = w_in.shape[0]
    n_p, s_p, _ = x_prompt.shape
    n_d, s_d, _ = x_sample.shape
    past_len = cache_kv_latent.shape[2]
    tab_p = _rope_tables(jnp.arange(s_p))
    tab_d = _rope_tables(past_len + jnp.arange(s_d))
    conv_zero = jnp.zeros((n_p, CONV_W - 1, D_CONV), F32)
    ts_p = _pick_tile(s_p, 512)
    tq = _pick_tile(s_p, 256)
    tm_p = _pick_tile(n_p * s_p, 256)
    tm_d = _pick_tile(n_d * s_d, 256)

    xp, xd = x_prompt, x_sample
    lat_p, kr_p, cv_p, lat_d, kr_d, cv_d = [], [], [], [], [], []
    for i in range(depth):
        lw = _layer_weights(i, g_mix_pre, w_in, w_conv, g_q, w_uq, g_kv, w_ukv, g_conv_out,
                            g_attn_out, w_o, g_mix_post, g_ffn_pre, w_ffn_gate, w_ffn_up,
                            w_ffn_down, g_ffn_post, w_ple_proj, w_ple_gate)
        convn, q, k, v, lat, krb, ncv = _in_proj(xp, tab_p, conv_zero, lw, ts_p)
        attn = _attn_prompt(q, k, v, lw["g_attn"], tq, tq)
        xp = _out_ffn(xp.reshape(n_p * s_p, D_MODEL), convn.reshape(n_p * s_p, D_CONV),
                      attn.reshape(n_p * s_p, D_HEADS_PAD), p_prompt[i].reshape(n_p * s_p, D_PLE),
                      lw, tm_p).reshape(n_p, s_p, D_MODEL)
        lat_p.append(lat); kr_p.append(krb[..., QK_NOPE:QK_DIM]); cv_p.append(ncv)
        convn, q, _, _, lat, krb, ncv = _in_proj(xd, tab_d, state_conv[i], lw, s_d)
        kr_new = krb[..., QK_NOPE:QK_DIM]
        attn = _attn_sample(q, cache_kv_latent[i], cache_k_rope[i], lat, kr_new, lw)
        xd = _out_ffn(xd.reshape(n_d * s_d, D_MODEL), convn.reshape(n_d * s_d, D_CONV),
                      attn.reshape(n_d * s_d, D_HEADS_PAD), p_sample[i].reshape(n_d * s_d, D_PLE),
                      lw, tm_d).reshape(n_d, s_d, D_MODEL)
        lat_d.append(lat); kr_d.append(kr_new); cv_d.append(ncv)

    return (xp, xd, jnp.stack(lat_p), jnp.stack(kr_p), jnp.stack(cv_p),
            jnp.stack(lat_d), jnp.stack(kr_d), jnp.stack(cv_d))
```

```python
import functools

import jax
import jax.numpy as jnp
from jax import lax
from jax.experimental import pallas as pl
from jax.experimental.pallas import tpu as pltpu

D_MODEL = 1024
CHUNK = 64
D_PLE = 256
D_CONV = 512
CONV_W = 3
V_DIM = 64
QK_NOPE = 64
QK_ROPE = 32
QK_DIM = QK_NOPE + QK_ROPE
N_HEADS = 8
D_ATTN = N_HEADS * V_DIM
Q_LORA = 768
KV_LORA = 256
D_FF = 2816
ROPE_THETA = 10000.0
EPS = 1e-6
SCALE = QK_DIM ** -0.5

OFF_C = D_CONV
OFF_X = 2 * D_CONV
OFF_Q = 3 * D_CONV
OFF_KV = OFF_Q + Q_LORA
OFF_KR = OFF_KV + KV_LORA
IN_COLS = OFF_KR + QK_ROPE

LANES = 128
HEAD_PAD = LANES
D_HEADS_PAD = N_HEADS * HEAD_PAD
IN_EXT = OFF_KR + LANES
FF_CHUNK = 256
NEG_BIG = -1e30
Q_PRESCALE = SCALE * 1.4426950408889634

BF16 = jnp.bfloat16
F32 = jnp.float32


def _rms(x, g):
    ms = jnp.mean(x * x, axis=-1, keepdims=True)
    return x * lax.rsqrt(ms + EPS) * g


def _dot(a, b):
    return jnp.dot(a, b, preferred_element_type=F32)


def _dot_nt(a, b):
    return lax.dot_general(a, b, (((1,), (1,)), ((), ())), preferred_element_type=F32)


def _const_spec(shape):
    nd = len(shape)
    return pl.BlockSpec(shape, lambda *_: (0,) * nd, pipeline_mode=pl.Buffered(1))


def _in_proj_kernel(x_ref, tab_ref, cinit_ref, gpre_ref, win_ref, wconv_ref, gq_ref, wq_ref,
                    gkv_ref, gconv_ref, *rest, with_kv):
    if with_kv:
        wk_ref, wvt_ref, convn_ref, q_ref, lat_ref, kr_ref, newconv_ref, k_ref, vt_ref, carry_ref = rest
    else:
        convn_ref, q_ref, lat_ref, kr_ref, newconv_ref, carry_ref = rest
    ts = x_ref.shape[1]
    s_idx = pl.program_id(1)

    @pl.when(s_idx == 0)
    def _():
        carry_ref[0:2, :] = cinit_ref[0]

    h = _rms(x_ref[0], gpre_ref[...]).astype(BF16)

    gb = _dot(h, win_ref[:, 0:OFF_C])
    gc = _dot(h, win_ref[:, OFF_C:OFF_X])
    xin = _dot(h, win_ref[:, OFF_X:OFF_Q])
    u = gc * xin
    row = lax.broadcasted_iota(jnp.int32, (ts, 1), 0)
    c0 = carry_ref[0:1, :]
    c1 = carry_ref[1:2, :]
    u_m1 = jnp.where(row == 0, c1, pltpu.roll(u, 1, axis=0))
    u_m2 = pltpu.roll(u, 2, axis=0)
    u_m2 = jnp.where(row == 0, c0, jnp.where(row == 1, c1, u_m2))
    conv = u_m2 * wconv_ref[0:1, :] + u_m1 * wconv_ref[1:2, :] + u * wconv_ref[2:3, :]
    convn_ref[0] = _rms(gb * conv, gconv_ref[...]).astype(BF16)
    tail = u[ts - 2:ts, :]
    carry_ref[0:2, :] = tail
    newconv_ref[0] = tail

    c_q = tab_ref[0]
    s_q = tab_ref[1]
    c_k = tab_ref[2]
    s_k = tab_ref[3]

    cqn = _rms(_dot(h, win_ref[:, OFF_Q:OFF_KV]), gq_ref[...]).astype(BF16)
    qa = _dot(cqn, wq_ref[...])
    for hd in range(N_HEADS):
        blk = qa[:, hd * HEAD_PAD:(hd + 1) * HEAD_PAD]
        rot = blk * c_q + pltpu.roll(blk, HEAD_PAD - QK_ROPE, axis=1) * s_q
        q_ref[0, :, hd * HEAD_PAD:(hd + 1) * HEAD_PAD] = rot.astype(BF16)

    c_kv = _rms(_dot(h, win_ref[:, OFF_KV:OFF_KR]), gkv_ref[...])
    lat_ref[0] = c_kv
    ckvb = c_kv.astype(BF16)
    zkr = _dot(h, win_ref[:, OFF_KR:IN_EXT])
    kr_blk = zkr * c_k + pltpu.roll(zkr, HEAD_PAD - QK_ROPE, axis=1) * s_k
    kr_ref[0] = kr_blk
    if with_kv:
        kn = _dot(ckvb, wk_ref[...])
        for hd in range(N_HEADS):
            k_ref[0, :, hd * HEAD_PAD:(hd + 1) * HEAD_PAD] = (
                kn[:, hd * HEAD_PAD:(hd + 1) * HEAD_PAD] + kr_blk).astype(BF16)
        vt = _dot_nt(wvt_ref[...], ckvb).astype(BF16)
        tk = vt_ref.shape[3]
        for j in range(ts // tk):
            vt_ref[0, j] = vt[:, j * tk:(j + 1) * tk]


def _in_proj(x, tables, conv_init, lw, ts, tk=None):
    n, s, _ = x.shape
    with_kv = tk is not None
    grid = (n, s // ts)
    tok = lambda w: pl.BlockSpec((1, ts, w), lambda b, i: (b, i, 0))
    out_shape = [
        jax.ShapeDtypeStruct((n, s, D_CONV), BF16),
        jax.ShapeDtypeStruct((n, s, D_HEADS_PAD), BF16),
        jax.ShapeDtypeStruct((n, s, KV_LORA), F32),
        jax.ShapeDtypeStruct((n, s, LANES), F32),
        jax.ShapeDtypeStruct((n, CONV_W - 1, D_CONV), F32),
    ]
    in_specs = [
        tok(D_MODEL),
        pl.BlockSpec((4, ts, LANES), lambda b, i: (0, i, 0)),
        pl.BlockSpec((1, CONV_W - 1, D_CONV), lambda b, i: (b, 0, 0)),
        _const_spec((1, D_MODEL)),
        _const_spec((D_MODEL, IN_EXT)),
        _const_spec((CONV_W, D_CONV)),
        _const_spec((1, Q_LORA)),
        _const_spec((Q_LORA, D_HEADS_PAD)),
        _const_spec((1, KV_LORA)),
        _const_spec((1, D_CONV)),
    ]
    out_specs = [
        tok(D_CONV), tok(D_HEADS_PAD), tok(KV_LORA), tok(LANES),
        pl.BlockSpec((1, CONV_W - 1, D_CONV), lambda b, i: (b, 0, 0)),
    ]
    args = [x, tables, conv_init, lw["g_mix_pre"], lw["w_in"], lw["w_conv"], lw["g_q"], lw["w_q"],
            lw["g_kv"], lw["g_conv_out"]]
    if with_kv:
        in_specs += [_const_spec((KV_LORA, D_HEADS_PAD)), _const_spec((D_ATTN, KV_LORA))]
        args += [lw["w_k"], lw["w_vt"]]
        out_shape += [jax.ShapeDtypeStruct((n, s, D_HEADS_PAD), BF16),
                      jax.ShapeDtypeStruct((n, s // tk, D_ATTN, tk), BF16)]
        out_specs += [tok(D_HEADS_PAD),
                      pl.BlockSpec((1, ts // tk, D_ATTN, tk), lambda b, i: (b, i, 0, 0))]
    return pl.pallas_call(
        functools.partial(_in_proj_kernel, with_kv=with_kv),
        grid=grid,
        in_specs=in_specs,
        out_specs=tuple(out_specs),
        out_shape=tuple(out_shape),
        scratch_shapes=[pltpu.VMEM((8, D_CONV), F32)],
        compiler_params=pltpu.CompilerParams(
            dimension_semantics=("arbitrary", "arbitrary"),
            vmem_limit_bytes=48 * 1024 * 1024),
        name="in_proj_kv" if with_kv else "in_proj",
    )(*args)


def _attn_prompt_kernel(q_ref, k_ref, vt_ref, g_ref, o_ref, acc_ref, m_ref, l_ref, *, tq, tk):
    qi = pl.program_id(1)
    n_full = qi * (tq // tk)
    n_diag = tq // tk
    m_ref[...] = jnp.full(m_ref.shape, NEG_BIG, F32)
    l_ref[...] = jnp.zeros(l_ref.shape, F32)
    acc_ref[...] = jnp.zeros(acc_ref.shape, F32)

    def block(kb, masked):
        k0 = pl.multiple_of(kb * tk, tk)
        if masked:
            ck = (lax.broadcasted_iota(jnp.int32, (tk, tq), 0) + kb * tk) // CHUNK
            rq = (lax.broadcasted_iota(jnp.int32, (tk, tq), 1) + qi * tq) // CHUNK
            vis = ck <= rq
        def scores(hd):
            cols = slice(hd * HEAD_PAD, (hd + 1) * HEAD_PAD)
            return _dot_nt(k_ref[0, pl.ds(k0, tk), cols], q_ref[0, :, cols])

        st_next = scores(0)
        for hd in range(N_HEADS):
            st = st_next
            if hd + 1 < N_HEADS:
                st_next = scores(hd + 1)
            if masked:
                st = jnp.where(vis, st, NEG_BIG)
            m_prev = m_ref[hd:hd + 1, :]
            m_new = jnp.maximum(m_prev, jnp.max(st, axis=0, keepdims=True))
            alpha = jnp.exp2(m_prev - m_new)
            pt = jnp.exp2(st - m_new)
            l_ref[hd:hd + 1, :] = alpha * l_ref[hd:hd + 1, :] + jnp.sum(pt, axis=0, keepdims=True)
            acc_ref[hd] = alpha * acc_ref[hd] + _dot(
                vt_ref[0, kb, hd * V_DIM:(hd + 1) * V_DIM, :], pt.astype(BF16))
            m_ref[hd:hd + 1, :] = m_new

    def loop_body(kb, carry):
        block(kb, False)
        return carry

    lax.fori_loop(0, n_full, loop_body, 0)
    for d in range(n_diag):
        block(n_full + d, True)

    ssq = jnp.zeros((1, tq), F32)
    for hd in range(N_HEADS):
        o = acc_ref[hd] / l_ref[hd:hd + 1, :]
        ssq = ssq + jnp.sum(o * o, axis=0, keepdims=True)
        acc_ref[hd] = o
    rs = lax.rsqrt(ssq * (1.0 / D_ATTN) + EPS)
    ot = acc_ref[...].reshape(D_ATTN, tq) * rs
    o_ref[0] = (ot.T * g_ref[...]).astype(BF16)


def _attn_prompt(q, k, vt, g_attn, tq):
    n, s, _ = q.shape
    tk = vt.shape[3]
    return pl.pallas_call(
        functools.partial(_attn_prompt_kernel, tq=tq, tk=tk),
        grid=(n, s // tq),
        in_specs=[
            pl.BlockSpec((1, tq, D_HEADS_PAD), lambda b, i: (b, i, 0)),
            pl.BlockSpec((1, s, D_HEADS_PAD), lambda b, i: (b, 0, 0)),
            pl.BlockSpec((1, s // tk, D_ATTN, tk), lambda b, i: (b, 0, 0, 0)),
            _const_spec((1, D_ATTN)),
        ],
        out_specs=pl.BlockSpec((1, tq, D_ATTN), lambda b, i: (b, i, 0)),
        out_shape=jax.ShapeDtypeStruct((n, s, D_ATTN), BF16),
        scratch_shapes=[pltpu.VMEM((N_HEADS, V_DIM, tq), F32),
                        pltpu.VMEM((N_HEADS, tq), F32),
                        pltpu.VMEM((N_HEADS, tq), F32)],
        compiler_params=pltpu.CompilerParams(
            dimension_semantics=("arbitrary", "arbitrary"),
            vmem_limit_bytes=48 * 1024 * 1024),
        name="attn_prompt",
    )(q, k, vt, g_attn)


def _attn_sample_kernel(q_ref, cpast_ref, krpast_ref, cnew_ref, krnew_ref, wabs_ref, psel_ref,
                        wuv_ref, g_ref, o_ref):
    l_new = q_ref.shape[1]
    q = q_ref[0]
    ql, qr = [], []
    for hd in range(N_HEADS):
        qh = q[:, hd * HEAD_PAD:(hd + 1) * HEAD_PAD]
        ql.append(_dot(qh, wabs_ref[hd]))
        qr.append(_dot(qh, psel_ref[...]))
    ql = jnp.concatenate(ql, axis=0).astype(BF16)
    qr = jnp.concatenate(qr, axis=0).astype(BF16)
    cp = cpast_ref[0].astype(BF16)
    cn = cnew_ref[0].astype(BF16)
    s_p = _dot_nt(ql, cp) + _dot_nt(qr, krpast_ref[0].astype(BF16))
    s_n = _dot_nt(ql, cn) + _dot_nt(qr, krnew_ref[0].astype(BF16))
    m = jnp.maximum(jnp.max(s_p, axis=-1, keepdims=True), jnp.max(s_n, axis=-1, keepdims=True))
    p_p = jnp.exp2(s_p - m)
    p_n = jnp.exp2(s_n - m)
    l = jnp.sum(p_p, axis=-1, keepdims=True) + jnp.sum(p_n, axis=-1, keepdims=True)
    o_lat = ((_dot(p_p.astype(BF16), cp) + _dot(p_n.astype(BF16), cn)) / l).astype(BF16)
    o = _dot(o_lat[0:l_new, :], wuv_ref[0])
    for hd in range(1, N_HEADS):
        o = o + _dot(o_lat[hd * l_new:(hd + 1) * l_new, :], wuv_ref[hd])
    o_ref[0] = _rms(o, g_ref[...]).astype(BF16)


def _attn_sample(q, c_past, kr_past, c_new, kr_new, lw):
    n, l_new, _ = q.shape
    p_len = c_past.shape[1]
    per_b = lambda rows, w: pl.BlockSpec((1, rows, w), lambda b: (b, 0, 0))
    return pl.pallas_call(
        _attn_sample_kernel,
        grid=(n,),
        in_specs=[
            per_b(l_new, D_HEADS_PAD),
            per_b(p_len, KV_LORA),
            per_b(p_len, QK_ROPE),
            per_b(l_new, KV_LORA),
            per_b(l_new, QK_ROPE),
            _const_spec((N_HEADS, HEAD_PAD, KV_LORA)),
            _const_spec((HEAD_PAD, QK_ROPE)),
            _const_spec((N_HEADS, KV_LORA, D_ATTN)),
            _const_spec((1, D_ATTN)),
        ],
        out_specs=per_b(l_new, D_ATTN),
        out_shape=jax.ShapeDtypeStruct((n, l_new, D_ATTN), BF16),
        compiler_params=pltpu.CompilerParams(
            dimension_semantics=("arbitrary",),
            vmem_limit_bytes=48 * 1024 * 1024),
        name="attn_sample",
    )(q, c_past, kr_past, c_new, kr_new, lw["w_abs"], lw["p_sel"], lw["w_uv"], lw["g_attn"])


def _out_ffn_kernel(x_ref, convn_ref, attn_ref, p_ref, woc_ref, woa_ref, gpost_ref, gffn_ref,
                    wg_ref, wu_ref, wd_ref, gfpost_ref, wpg_ref, wpp_ref, y_ref, act_ref):
    mix = _dot(convn_ref[...], woc_ref[...]) + _dot(attn_ref[...], woa_ref[...])
    x1 = x_ref[...] + _rms(mix, gpost_ref[...])
    h = _rms(x1, gffn_ref[...]).astype(BF16)
    for c in range(D_FF // FF_CHUNK):
        cols = slice(c * FF_CHUNK, (c + 1) * FF_CHUNK)
        g = _dot(h, wg_ref[:, cols])
        u = _dot(h, wu_ref[:, cols])
        act_ref[:, cols] = (g * jax.nn.sigmoid(g) * u).astype(BF16)
    f = _dot(act_ref[...], wd_ref[...])
    x2 = x1 + _rms(f, gfpost_ref[...])
    gate = jax.nn.sigmoid(_dot(x2.astype(BF16), wpg_ref[...]))
    y_ref[...] = x2 + gate * _dot(p_ref[...].astype(BF16), wpp_ref[...])


def _out_ffn(x, convn, attn, p, lw, tm):
    m = x.shape[0]
    tok = lambda w: pl.BlockSpec((tm, w), lambda i: (i, 0))
    return pl.pallas_call(
        _out_ffn_kernel,
        grid=(m // tm,),
        in_specs=[
            tok(D_MODEL), tok(D_CONV), tok(D_ATTN), tok(D_PLE),
            _const_spec((D_CONV, D_MODEL)),
            _const_spec((D_ATTN, D_MODEL)),
            _const_spec((1, D_MODEL)),
            _const_spec((1, D_MODEL)),
            _const_spec((D_MODEL, D_FF)),
            _const_spec((D_MODEL, D_FF)),
            _const_spec((D_FF, D_MODEL)),
            _const_spec((1, D_MODEL)),
            _const_spec((D_MODEL, D_MODEL)),
            _const_spec((D_PLE, D_MODEL)),
        ],
        out_specs=tok(D_MODEL),
        out_shape=jax.ShapeDtypeStruct((m, D_MODEL), F32),
        scratch_shapes=[pltpu.VMEM((tm, D_FF), BF16)],
        compiler_params=pltpu.CompilerParams(
            dimension_semantics=("arbitrary",),
            vmem_limit_bytes=56 * 1024 * 1024),
        name="out_ffn",
    )(x, convn, attn, p, lw["w_o_conv"], lw["w_o_attn"], lw["g_mix_post"], lw["g_ffn_pre"],
      lw["w_gate"], lw["w_up"], lw["w_down"], lw["g_ffn_post"], lw["w_ple_gate"], lw["w_ple_proj"])


def _swap_halves(a):
    half = a.shape[-1] // 2
    return jnp.concatenate([a[..., half:], a[..., :half]], axis=-1)


def _layer_weights(i, g_mix_pre, w_in, w_conv, g_q, w_uq, g_kv, w_ukv, g_conv_out, g_attn_out, w_o,
                   g_mix_post, g_ffn_pre, w_ffn_gate, w_ffn_up, w_ffn_down, g_ffn_post,
                   w_ple_proj, w_ple_gate):
    row = lambda g: g[i].reshape(1, -1).astype(F32)
    w_kr = w_in[i][:, OFF_KR:IN_COLS]
    w_in_ext = jnp.concatenate(
        [w_in[i][:, :OFF_KR], jnp.zeros((D_MODEL, QK_NOPE), F32), w_kr, _swap_halves(w_kr)], axis=1)
    wq3 = w_uq[i].reshape(Q_LORA, N_HEADS, QK_DIM)
    w_q = jnp.concatenate(
        [wq3, _swap_halves(wq3[..., QK_NOPE:])], axis=-1).reshape(Q_LORA, D_HEADS_PAD)
    wkv3 = w_ukv[i].reshape(KV_LORA, N_HEADS, QK_NOPE + V_DIM)
    w_uk = wkv3[..., :QK_NOPE]
    w_uv = wkv3[..., QK_NOPE:]
    zpad = jnp.zeros((KV_LORA, N_HEADS, HEAD_PAD - QK_NOPE), F32)
    w_k = jnp.concatenate([w_uk, zpad], axis=-1).reshape(KV_LORA, D_HEADS_PAD)
    w_vt = w_uv.reshape(KV_LORA, D_ATTN).T
    w_abs = jnp.concatenate([w_uk, zpad], axis=-1).transpose(1, 2, 0)
    head_of_col = jnp.arange(D_ATTN) // V_DIM
    w_uv_heads = jnp.where(head_of_col[None, None, :] == jnp.arange(N_HEADS)[:, None, None],
                           w_uv.reshape(1, KV_LORA, D_ATTN), 0.0)
    p_sel = jnp.zeros((HEAD_PAD, QK_ROPE), F32).at[QK_NOPE:QK_DIM, :].set(jnp.eye(QK_ROPE, dtype=F32))
    return dict(
        g_mix_pre=row(g_mix_pre), w_in=w_in_ext.astype(BF16), w_conv=w_conv[i].astype(F32),
        g_q=row(g_q), w_q=w_q.astype(BF16), g_kv=row(g_kv), w_k=w_k.astype(BF16), w_vt=w_vt.astype(BF16),
        g_conv_out=row(g_conv_out), g_attn=row(g_attn_out),
        w_abs=w_abs.astype(BF16), w_uv=w_uv_heads.astype(BF16), p_sel=p_sel.astype(BF16),
        w_o_conv=w_o[i][:D_CONV].astype(BF16), w_o_attn=w_o[i][D_CONV:].astype(BF16),
        g_mix_post=row(g_mix_post), g_ffn_pre=row(g_ffn_pre),
        w_gate=w_ffn_gate[i].astype(BF16), w_up=w_ffn_up[i].astype(BF16), w_down=w_ffn_down[i].astype(BF16),
        g_ffn_post=row(g_ffn_post), w_ple_gate=w_ple_gate[i].astype(BF16), w_ple_proj=w_ple_proj[i].astype(BF16),
    )


def _rope_tables(pos):
    half = QK_ROPE // 2
    inv = ROPE_THETA ** (-jnp.arange(half, dtype=F32) / half)
    ang = pos.astype(F32)[:, None] * inv[None, :]
    cos, sin = jnp.cos(ang), jnp.sin(ang)
    n = pos.shape[0]
    cc = jnp.concatenate([cos, cos], axis=1)
    ss = jnp.concatenate([-sin, sin], axis=1)
    z_lo = jnp.zeros((n, QK_NOPE), F32)
    z_hi = jnp.zeros((n, HEAD_PAD - QK_DIM), F32)
    c_q = jnp.concatenate([jnp.ones((n, QK_NOPE), F32), cc, z_hi], axis=1) * Q_PRESCALE
    s_q = jnp.concatenate([z_lo, ss, z_hi], axis=1)
    c_k = jnp.concatenate([z_lo, cc, z_hi], axis=1)
    return jnp.stack([c_q, s_q * Q_PRESCALE, c_k, s_q])


def _pick_tile(n, cap):
    t = min(n, cap)
    while n % t:
        t //= 2
    return t


def kernel(x_prompt, x_sample, cache_kv_latent, cache_k_rope, state_conv, p_prompt, p_sample, g_mix_pre, w_in, w_conv, g_q, w_uq, g_kv, w_ukv, g_conv_out, g_attn_out, w_o, g_mix_post, g_ffn_pre, w_ffn_gate, w_ffn_up, w_ffn_down, g_ffn_post, w_ple_proj, w_ple_gate):
    depth = w_in.shape[0]
    n_p, s_p, _ = x_prompt.shape
    n_d, s_d, _ = x_sample.shape
    past_len = cache_kv_latent.shape[2]
    tab_p = _rope_tables(jnp.arange(s_p))
    tab_d = _rope_tables(past_len + jnp.arange(s_d))
    conv_zero = jnp.zeros((n_p, CONV_W - 1, D_CONV), F32)
    ts_p = _pick_tile(s_p, 512)
    tq = _pick_tile(s_p, 256)
    tm_p = _pick_tile(n_p * s_p, 256)
    tm_d = _pick_tile(n_d * s_d, 256)

    xp, xd = x_prompt, x_sample
    lat_p, kr_p, cv_p, lat_d, kr_d, cv_d = [], [], [], [], [], []
    for i in range(depth):
        lw = _layer_weights(i, g_mix_pre, w_in, w_conv, g_q, w_uq, g_kv, w_ukv, g_conv_out,
                            g_attn_out, w_o, g_mix_post, g_ffn_pre, w_ffn_gate, w_ffn_up,
                            w_ffn_down, g_ffn_post, w_ple_proj, w_ple_gate)
        convn, q, lat, krb, ncv, k, vt = _in_proj(xp, tab_p, conv_zero, lw, ts_p, tk=tq)
        attn = _attn_prompt(q, k, vt, lw["g_attn"], tq)
        xp = _out_ffn(xp.reshape(n_p * s_p, D_MODEL), convn.reshape(n_p * s_p, D_CONV),
                      attn.reshape(n_p * s_p, D_ATTN), p_prompt[i].reshape(n_p * s_p, D_PLE),
                      lw, tm_p).reshape(n_p, s_p, D_MODEL)
        lat_p.append(lat); kr_p.append(krb[..., QK_NOPE:QK_DIM]); cv_p.append(ncv)
        convn, q, lat, krb, ncv = _in_proj(xd, tab_d, state_conv[i], lw, s_d)
        kr_new = krb[..., QK_NOPE:QK_DIM]
        attn = _attn_sample(q, cache_kv_latent[i], cache_k_rope[i], lat, kr_new, lw)
        xd = _out_ffn(xd.reshape(n_d * s_d, D_MODEL), convn.reshape(n_d * s_d, D_CONV),
                      attn.reshape(n_d * s_d, D_ATTN), p_sample[i].reshape(n_d * s_d, D_PLE),
                      lw, tm_d).reshape(n_d, s_d, D_MODEL)
        lat_d.append(lat); kr_d.append(kr_new); cv_d.append(ncv)

    return (xp, xd, jnp.stack(lat_p), jnp.stack(kr_p), jnp.stack(cv_p),
            jnp.stack(lat_d), jnp.stack(kr_d), jnp.stack(cv_d))
```

```python
import functools

import jax
import jax.numpy as jnp
from jax import lax
from jax.experimental import pallas as pl
from jax.experimental.pallas import tpu as pltpu

D_MODEL = 1024
CHUNK = 64
D_PLE = 256
D_CONV = 512
CONV_W = 3
V_DIM = 64
QK_NOPE = 64
QK_ROPE = 32
QK_DIM = QK_NOPE + QK_ROPE
N_HEADS = 8
D_ATTN = N_HEADS * V_DIM
Q_LORA = 768
KV_LORA = 256
D_FF = 2816
ROPE_THETA = 10000.0
EPS = 1e-6
SCALE = QK_DIM ** -0.5

OFF_C = D_CONV
OFF_X = 2 * D_CONV
OFF_Q = 3 * D_CONV
OFF_KV = OFF_Q + Q_LORA
OFF_KR = OFF_KV + KV_LORA
IN_COLS = OFF_KR + QK_ROPE

LANES = 128
HEAD_PAD = LANES
D_HEADS_PAD = N_HEADS * HEAD_PAD
IN_EXT = OFF_KR + LANES
FF_CHUNK = 256
NEG_BIG = -1e30
Q_PRESCALE = SCALE * 1.4426950408889634
SCORE_LEAD = 3
SUM_ROWS = 16

BF16 = jnp.bfloat16
F32 = jnp.float32


def _rms(x, g):
    ms = jnp.mean(x * x, axis=-1, keepdims=True)
    return x * lax.rsqrt(ms + EPS) * g


def _dot(a, b):
    return jnp.dot(a, b, preferred_element_type=F32)


def _dot_nt(a, b):
    return lax.dot_general(a, b, (((1,), (1,)), ((), ())), preferred_element_type=F32)


def _const_spec(shape):
    nd = len(shape)
    return pl.BlockSpec(shape, lambda *_: (0,) * nd, pipeline_mode=pl.Buffered(1))


def _in_proj_kernel(x_ref, tab_ref, cinit_ref, gpre_ref, win_ref, wconv_ref, gq_ref, wq_ref,
                    gkv_ref, gconv_ref, *rest, with_kv):
    if with_kv:
        wk_ref, wvt_ref, convn_ref, q_ref, lat_ref, kr_ref, newconv_ref, k_ref, vt_ref, carry_ref = rest
    else:
        convn_ref, q_ref, lat_ref, kr_ref, newconv_ref, carry_ref = rest
    ts = x_ref.shape[1]
    s_idx = pl.program_id(1)

    @pl.when(s_idx == 0)
    def _():
        carry_ref[0:2, :] = cinit_ref[0]

    h = _rms(x_ref[0], gpre_ref[...]).astype(BF16)

    gb = _dot(h, win_ref[:, 0:OFF_C])
    gc = _dot(h, win_ref[:, OFF_C:OFF_X])
    xin = _dot(h, win_ref[:, OFF_X:OFF_Q])
    u = gc * xin
    row = lax.broadcasted_iota(jnp.int32, (ts, 1), 0)
    c0 = carry_ref[0:1, :]
    c1 = carry_ref[1:2, :]
    u_m1 = jnp.where(row == 0, c1, pltpu.roll(u, 1, axis=0))
    u_m2 = pltpu.roll(u, 2, axis=0)
    u_m2 = jnp.where(row == 0, c0, jnp.where(row == 1, c1, u_m2))
    conv = u_m2 * wconv_ref[0:1, :] + u_m1 * wconv_ref[1:2, :] + u * wconv_ref[2:3, :]
    convn_ref[0] = _rms(gb * conv, gconv_ref[...]).astype(BF16)
    tail = u[ts - 2:ts, :]
    carry_ref[0:2, :] = tail
    newconv_ref[0] = tail

    c_q = tab_ref[0]
    s_q = tab_ref[1]
    c_k = tab_ref[2]
    s_k = tab_ref[3]

    cqn = _rms(_dot(h, win_ref[:, OFF_Q:OFF_KV]), gq_ref[...]).astype(BF16)
    qa = _dot(cqn, wq_ref[...])
    for hd in range(N_HEADS):
        blk = qa[:, hd * HEAD_PAD:(hd + 1) * HEAD_PAD]
        rot = blk * c_q + pltpu.roll(blk, HEAD_PAD - QK_ROPE, axis=1) * s_q
        q_ref[0, :, hd * HEAD_PAD:(hd + 1) * HEAD_PAD] = rot.astype(BF16)

    c_kv = _rms(_dot(h, win_ref[:, OFF_KV:OFF_KR]), gkv_ref[...])
    lat_ref[0] = c_kv
    ckvb = c_kv.astype(BF16)
    zkr = _dot(h, win_ref[:, OFF_KR:IN_EXT])
    kr_blk = zkr * c_k + pltpu.roll(zkr, HEAD_PAD - QK_ROPE, axis=1) * s_k
    kr_ref[0] = kr_blk
    if with_kv:
        kn = _dot(ckvb, wk_ref[...])
        for hd in range(N_HEADS):
            k_ref[0, :, hd * HEAD_PAD:(hd + 1) * HEAD_PAD] = (
                kn[:, hd * HEAD_PAD:(hd + 1) * HEAD_PAD] + kr_blk).astype(BF16)
        vt = _dot_nt(wvt_ref[...], ckvb).astype(BF16)
        tk = vt_ref.shape[3]
        for j in range(ts // tk):
            vt_ref[0, j] = vt[:, j * tk:(j + 1) * tk]


def _in_proj(x, tables, conv_init, lw, ts, tk=None):
    n, s, _ = x.shape
    with_kv = tk is not None
    grid = (n, s // ts)
    tok = lambda w: pl.BlockSpec((1, ts, w), lambda b, i: (b, i, 0))
    out_shape = [
        jax.ShapeDtypeStruct((n, s, D_CONV), BF16),
        jax.ShapeDtypeStruct((n, s, D_HEADS_PAD), BF16),
        jax.ShapeDtypeStruct((n, s, KV_LORA), F32),
        jax.ShapeDtypeStruct((n, s, LANES), F32),
        jax.ShapeDtypeStruct((n, CONV_W - 1, D_CONV), F32),
    ]
    in_specs = [
        tok(D_MODEL),
        pl.BlockSpec((4, ts, LANES), lambda b, i: (0, i, 0)),
        pl.BlockSpec((1, CONV_W - 1, D_CONV), lambda b, i: (b, 0, 0)),
        _const_spec((1, D_MODEL)),
        _const_spec((D_MODEL, IN_EXT)),
        _const_spec((CONV_W, D_CONV)),
        _const_spec((1, Q_LORA)),
        _const_spec((Q_LORA, D_HEADS_PAD)),
        _const_spec((1, KV_LORA)),
        _const_spec((1, D_CONV)),
    ]
    out_specs = [
        tok(D_CONV), tok(D_HEADS_PAD), tok(KV_LORA), tok(LANES),
        pl.BlockSpec((1, CONV_W - 1, D_CONV), lambda b, i: (b, 0, 0)),
    ]
    args = [x, tables, conv_init, lw["g_mix_pre"], lw["w_in"], lw["w_conv"], lw["g_q"], lw["w_q"],
            lw["g_kv"], lw["g_conv_out"]]
    if with_kv:
        in_specs += [_const_spec((KV_LORA, D_HEADS_PAD)), _const_spec((D_ATTN, KV_LORA))]
        args += [lw["w_k"], lw["w_vt"]]
        out_shape += [jax.ShapeDtypeStruct((n, s, D_HEADS_PAD), BF16),
                      jax.ShapeDtypeStruct((n, s // tk, D_ATTN, tk), BF16)]
        out_specs += [tok(D_HEADS_PAD),
                      pl.BlockSpec((1, ts // tk, D_ATTN, tk), lambda b, i: (b, i, 0, 0))]
    return pl.pallas_call(
        functools.partial(_in_proj_kernel, with_kv=with_kv),
        grid=grid,
        in_specs=in_specs,
        out_specs=tuple(out_specs),
        out_shape=tuple(out_shape),
        scratch_shapes=[pltpu.VMEM((8, D_CONV), F32)],
        compiler_params=pltpu.CompilerParams(
            dimension_semantics=("arbitrary", "arbitrary"),
            vmem_limit_bytes=48 * 1024 * 1024),
        name="in_proj_kv" if with_kv else "in_proj",
    )(*args)


def _attn_prompt_kernel(q_ref, k_ref, vt_ref, g_ref, o_ref, acc_ref, m_ref, ot_ref, *, tq):
    tk = vt_ref.shape[3]
    qi = pl.program_id(1)
    odd = jnp.bitwise_and(qi, 1)
    n_loop = lax.shift_right_logical(qi + 1, 1) - odd
    m_ref[...] = jnp.full(m_ref.shape, NEG_BIG, F32)
    acc_ref[...] = jnp.zeros(acc_ref.shape, F32)

    def block(kb, width, masked):
        k0 = pl.multiple_of(kb * tk, tk)
        if masked:
            ck = (lax.broadcasted_iota(jnp.int32, (width, tq), 0) + kb * tk) // CHUNK
            rq = (lax.broadcasted_iota(jnp.int32, (width, tq), 1) + qi * tq) // CHUNK
            vis = ck <= rq
        ones = jnp.ones((SUM_ROWS, width), BF16)

        def scores(hd):
            cols = slice(hd * HEAD_PAD, (hd + 1) * HEAD_PAD)
            return _dot_nt(k_ref[0, pl.ds(k0, width), cols], q_ref[0, :, cols])

        def softmax(hd, st):
            if masked:
                st = jnp.where(vis, st, NEG_BIG)
            m_prev = m_ref[hd:hd + 1, :]
            m_new = jnp.maximum(m_prev, jnp.max(st, axis=0, keepdims=True))
            m_ref[hd:hd + 1, :] = m_new
            return jnp.exp2(m_prev - m_new), jnp.exp2(st - m_new).astype(BF16)

        def weighted_values(hd, alpha, pt):
            lhs = jnp.concatenate([vt_ref[0, kb, hd * V_DIM:(hd + 1) * V_DIM, 0:width], ones], axis=0)
            acc_ref[hd] = alpha * acc_ref[hd] + _dot(lhs, pt)

        sts = {}
        for step in range(N_HEADS + SCORE_LEAD):
            if step < N_HEADS:
                sts[step] = scores(step)
            hd = step - SCORE_LEAD
            if hd >= 0:
                weighted_values(hd, *softmax(hd, sts.pop(hd)))

    def loop_body(kb, carry):
        block(kb, tk, False)
        return carry

    lax.fori_loop(0, n_loop, loop_body, 0)

    @pl.when(odd == 1)
    def _():
        block(n_loop, tk, True)

    @pl.when(odd == 0)
    def _():
        block(n_loop, tq, True)

    ssq = jnp.zeros((1, tq), F32)
    for hd in range(N_HEADS):
        a = acc_ref[hd]
        o = a[0:V_DIM, :] / a[V_DIM:V_DIM + 1, :]
        ssq = ssq + jnp.sum(o * o, axis=0, keepdims=True)
        ot_ref[hd * V_DIM:(hd + 1) * V_DIM, :] = o
    rs = lax.rsqrt(ssq * (1.0 / D_ATTN) + EPS)
    o_ref[0] = ((ot_ref[...] * rs).T * g_ref[...]).astype(BF16)


def _attn_prompt(q, k, vt, g_attn, tq):
    n, s, _ = q.shape
    tk = vt.shape[3]
    assert tk == 2 * tq and s % tk == 0
    return pl.pallas_call(
        functools.partial(_attn_prompt_kernel, tq=tq),
        grid=(n, s // tq),
        in_specs=[
            pl.BlockSpec((1, tq, D_HEADS_PAD), lambda b, i: (b, i, 0)),
            pl.BlockSpec((1, s, D_HEADS_PAD), lambda b, i: (b, 0, 0)),
            pl.BlockSpec((1, s // tk, D_ATTN, tk), lambda b, i: (b, 0, 0, 0)),
            _const_spec((1, D_ATTN)),
        ],
        out_specs=pl.BlockSpec((1, tq, D_ATTN), lambda b, i: (b, i, 0)),
        out_shape=jax.ShapeDtypeStruct((n, s, D_ATTN), BF16),
        scratch_shapes=[pltpu.VMEM((N_HEADS, V_DIM + SUM_ROWS, tq), F32),
                        pltpu.VMEM((N_HEADS, tq), F32),
                        pltpu.VMEM((D_ATTN, tq), F32)],
        compiler_params=pltpu.CompilerParams(
            dimension_semantics=("arbitrary", "arbitrary"),
            vmem_limit_bytes=48 * 1024 * 1024),
        name="attn_prompt",
    )(q, k, vt, g_attn)


def _attn_sample_kernel(q_ref, cpast_ref, krpast_ref, cnew_ref, krnew_ref, wabs_ref, psel_ref,
                        wuv_ref, g_ref, o_ref):
    l_new = q_ref.shape[1]
    q = q_ref[0]
    ql, qr = [], []
    for hd in range(N_HEADS):
        qh = q[:, hd * HEAD_PAD:(hd + 1) * HEAD_PAD]
        ql.append(_dot(qh, wabs_ref[hd]))
        qr.append(_dot(qh, psel_ref[...]))
    ql = jnp.concatenate(ql, axis=0).astype(BF16)
    qr = jnp.concatenate(qr, axis=0).astype(BF16)
    cp = cpast_ref[0].astype(BF16)
    cn = cnew_ref[0].astype(BF16)
    s_p = _dot_nt(ql, cp) + _dot_nt(qr, krpast_ref[0].astype(BF16))
    s_n = _dot_nt(ql, cn) + _dot_nt(qr, krnew_ref[0].astype(BF16))
    m = jnp.maximum(jnp.max(s_p, axis=-1, keepdims=True), jnp.max(s_n, axis=-1, keepdims=True))
    p_p = jnp.exp2(s_p - m)
    p_n = jnp.exp2(s_n - m)
    l = jnp.sum(p_p, axis=-1, keepdims=True) + jnp.sum(p_n, axis=-1, keepdims=True)
    o_lat = ((_dot(p_p.astype(BF16), cp) + _dot(p_n.astype(BF16), cn)) / l).astype(BF16)
    o = _dot(o_lat[0:l_new, :], wuv_ref[0])
    for hd in range(1, N_HEADS):
        o = o + _dot(o_lat[hd * l_new:(hd + 1) * l_new, :], wuv_ref[hd])
    o_ref[0] = _rms(o, g_ref[...]).astype(BF16)


def _attn_sample(q, c_past, kr_past, c_new, kr_new, lw):
    n, l_new, _ = q.shape
    p_len = c_past.shape[1]
    per_b = lambda rows, w: pl.BlockSpec((1, rows, w), lambda b: (b, 0, 0))
    return pl.pallas_call(
        _attn_sample_kernel,
        grid=(n,),
        in_specs=[
            per_b(l_new, D_HEADS_PAD),
            per_b(p_len, KV_LORA),
            per_b(p_len, QK_ROPE),
            per_b(l_new, KV_LORA),
            per_b(l_new, QK_ROPE),
            _const_spec((N_HEADS, HEAD_PAD, KV_LORA)),
            _const_spec((HEAD_PAD, QK_ROPE)),
            _const_spec((N_HEADS, KV_LORA, D_ATTN)),
            _const_spec((1, D_ATTN)),
        ],
        out_specs=per_b(l_new, D_ATTN),
        out_shape=jax.ShapeDtypeStruct((n, l_new, D_ATTN), BF16),
        compiler_params=pltpu.CompilerParams(
            dimension_semantics=("arbitrary",),
            vmem_limit_bytes=48 * 1024 * 1024),
        name="attn_sample",
    )(q, c_past, kr_past, c_new, kr_new, lw["w_abs"], lw["p_sel"], lw["w_uv"], lw["g_attn"])


def _out_ffn_kernel(x_ref, convn_ref, attn_ref, p_ref, woc_ref, woa_ref, gpost_ref, gffn_ref,
                    wg_ref, wu_ref, wd_ref, gfpost_ref, wpg_ref, wpp_ref, y_ref, act_ref):
    mix = _dot(convn_ref[...], woc_ref[...]) + _dot(attn_ref[...], woa_ref[...])
    x1 = x_ref[...] + _rms(mix, gpost_ref[...])
    h = _rms(x1, gffn_ref[...]).astype(BF16)
    for c in range(D_FF // FF_CHUNK):
        cols = slice(c * FF_CHUNK, (c + 1) * FF_CHUNK)
        g = _dot(h, wg_ref[:, cols])
        u = _dot(h, wu_ref[:, cols])
        act_ref[:, cols] = (g * jax.nn.sigmoid(g) * u).astype(BF16)
    f = _dot(act_ref[...], wd_ref[...])
    x2 = x1 + _rms(f, gfpost_ref[...])
    gate = jax.nn.sigmoid(_dot(x2.astype(BF16), wpg_ref[...]))
    y_ref[...] = x2 + gate * _dot(p_ref[...].astype(BF16), wpp_ref[...])


def _out_ffn(x, convn, attn, p, lw, tm):
    m = x.shape[0]
    tok = lambda w: pl.BlockSpec((tm, w), lambda i: (i, 0))
    return pl.pallas_call(
        _out_ffn_kernel,
        grid=(m // tm,),
        in_specs=[
            tok(D_MODEL), tok(D_CONV), tok(D_ATTN), tok(D_PLE),
            _const_spec((D_CONV, D_MODEL)),
            _const_spec((D_ATTN, D_MODEL)),
            _const_spec((1, D_MODEL)),
            _const_spec((1, D_MODEL)),
            _const_spec((D_MODEL, D_FF)),
            _const_spec((D_MODEL, D_FF)),
            _const_spec((D_FF, D_MODEL)),
            _const_spec((1, D_MODEL)),
            _const_spec((D_MODEL, D_MODEL)),
            _const_spec((D_PLE, D_MODEL)),
        ],
        out_specs=tok(D_MODEL),
        out_shape=jax.ShapeDtypeStruct((m, D_MODEL), F32),
        scratch_shapes=[pltpu.VMEM((tm, D_FF), BF16)],
        compiler_params=pltpu.CompilerParams(
            dimension_semantics=("arbitrary",),
            vmem_limit_bytes=56 * 1024 * 1024),
        name="out_ffn",
    )(x, convn, attn, p, lw["w_o_conv"], lw["w_o_attn"], lw["g_mix_post"], lw["g_ffn_pre"],
      lw["w_gate"], lw["w_up"], lw["w_down"], lw["g_ffn_post"], lw["w_ple_gate"], lw["w_ple_proj"])


def _swap_halves(a):
    half = a.shape[-1] // 2
    return jnp.concatenate([a[..., half:], a[..., :half]], axis=-1)


def _layer_weights(i, g_mix_pre, w_in, w_conv, g_q, w_uq, g_kv, w_ukv, g_conv_out, g_attn_out, w_o,
                   g_mix_post, g_ffn_pre, w_ffn_gate, w_ffn_up, w_ffn_down, g_ffn_post,
                   w_ple_proj, w_ple_gate):
    row = lambda g: g[i].reshape(1, -1).astype(F32)
    w_kr = w_in[i][:, OFF_KR:IN_COLS]
    w_in_ext = jnp.concatenate(
        [w_in[i][:, :OFF_KR], jnp.zeros((D_MODEL, QK_NOPE), F32), w_kr, _swap_halves(w_kr)], axis=1)
    wq3 = w_uq[i].reshape(Q_LORA, N_HEADS, QK_DIM)
    w_q = jnp.concatenate(
        [wq3, _swap_halves(wq3[..., QK_NOPE:])], axis=-1).reshape(Q_LORA, D_HEADS_PAD)
    wkv3 = w_ukv[i].reshape(KV_LORA, N_HEADS, QK_NOPE + V_DIM)
    w_uk = wkv3[..., :QK_NOPE]
    w_uv = wkv3[..., QK_NOPE:]
    zpad = jnp.zeros((KV_LORA, N_HEADS, HEAD_PAD - QK_NOPE), F32)
    w_k = jnp.concatenate([w_uk, zpad], axis=-1).reshape(KV_LORA, D_HEADS_PAD)
    w_vt = w_uv.reshape(KV_LORA, D_ATTN).T
    w_abs = jnp.concatenate([w_uk, zpad], axis=-1).transpose(1, 2, 0)
    head_of_col = jnp.arange(D_ATTN) // V_DIM
    w_uv_heads = jnp.where(head_of_col[None, None, :] == jnp.arange(N_HEADS)[:, None, None],
                           w_uv.reshape(1, KV_LORA, D_ATTN), 0.0)
    p_sel = jnp.zeros((HEAD_PAD, QK_ROPE), F32).at[QK_NOPE:QK_DIM, :].set(jnp.eye(QK_ROPE, dtype=F32))
    return dict(
        g_mix_pre=row(g_mix_pre), w_in=w_in_ext.astype(BF16), w_conv=w_conv[i].astype(F32),
        g_q=row(g_q), w_q=w_q.astype(BF16), g_kv=row(g_kv), w_k=w_k.astype(BF16), w_vt=w_vt.astype(BF16),
        g_conv_out=row(g_conv_out), g_attn=row(g_attn_out),
        w_abs=w_abs.astype(BF16), w_uv=w_uv_heads.astype(BF16), p_sel=p_sel.astype(BF16),
        w_o_conv=w_o[i][:D_CONV].astype(BF16), w_o_attn=w_o[i][D_CONV:].astype(BF16),
        g_mix_post=row(g_mix_post), g_ffn_pre=row(g_ffn_pre),
        w_gate=w_ffn_gate[i].astype(BF16), w_up=w_ffn_up[i].astype(BF16), w_down=w_ffn_down[i].astype(BF16),
        g_ffn_post=row(g_ffn_post), w_ple_gate=w_ple_gate[i].astype(BF16), w_ple_proj=w_ple_proj[i].astype(BF16),
    )


def _rope_tables(pos):
    half = QK_ROPE // 2
    inv = ROPE_THETA ** (-jnp.arange(half, dtype=F32) / half)
    ang = pos.astype(F32)[:, None] * inv[None, :]
    cos, sin = jnp.cos(ang), jnp.sin(ang)
    n = pos.shape[0]
    cc = jnp.concatenate([cos, cos], axis=1)
    ss = jnp.concatenate([-sin, sin], axis=1)
    z_lo = jnp.zeros((n, QK_NOPE), F32)
    z_hi = jnp.zeros((n, HEAD_PAD - QK_DIM), F32)
    c_q = jnp.concatenate([jnp.ones((n, QK_NOPE), F32), cc, z_hi], axis=1) * Q_PRESCALE
    s_q = jnp.concatenate([z_lo, ss, z_hi], axis=1)
    c_k = jnp.concatenate([z_lo, cc, z_hi], axis=1)
    return jnp.stack([c_q, s_q * Q_PRESCALE, c_k, s_q])


def _pick_tile(n, cap):
    t = min(n, cap)
    while n % t:
        t //= 2
    return t


def kernel(x_prompt, x_sample, cache_kv_latent, cache_k_rope, state_conv, p_prompt, p_sample, g_mix_pre, w_in, w_conv, g_q, w_uq, g_kv, w_ukv, g_conv_out, g_attn_out, w_o, g_mix_post, g_ffn_pre, w_ffn_gate, w_ffn_up, w_ffn_down, g_ffn_post, w_ple_proj, w_ple_gate):
    depth = w_in.shape[0]
    n_p, s_p, _ = x_prompt.shape
    n_d, s_d, _ = x_sample.shape
    past_len = cache_kv_latent.shape[2]
    tab_p = _rope_tables(jnp.arange(s_p))
    tab_d = _rope_tables(past_len + jnp.arange(s_d))
    conv_zero = jnp.zeros((n_p, CONV_W - 1, D_CONV), F32)
    ts_p = _pick_tile(s_p, 512)
    tq = _pick_tile(s_p, 256)
    tm_p = _pick_tile(n_p * s_p, 256)
    tm_d = _pick_tile(n_d * s_d, 256)

    xp, xd = x_prompt, x_sample
    lat_p, kr_p, cv_p, lat_d, kr_d, cv_d = [], [], [], [], [], []
    for i in range(depth):
        lw = _layer_weights(i, g_mix_pre, w_in, w_conv, g_q, w_uq, g_kv, w_ukv, g_conv_out,
                            g_attn_out, w_o, g_mix_post, g_ffn_pre, w_ffn_gate, w_ffn_up,
                            w_ffn_down, g_ffn_post, w_ple_proj, w_ple_gate)
        convn, q, lat, krb, ncv, k, vt = _in_proj(xp, tab_p, conv_zero, lw, ts_p, tk=2 * tq)
        attn = _attn_prompt(q, k, vt, lw["g_attn"], tq)
        xp = _out_ffn(xp.reshape(n_p * s_p, D_MODEL), convn.reshape(n_p * s_p, D_CONV),
                      attn.reshape(n_p * s_p, D_ATTN), p_prompt[i].reshape(n_p * s_p, D_PLE),
                      lw, tm_p).reshape(n_p, s_p, D_MODEL)
        lat_p.append(lat); kr_p.append(krb[..., QK_NOPE:QK_DIM]); cv_p.append(ncv)
        convn, q, lat, krb, ncv = _in_proj(xd, tab_d, state_conv[i], lw, s_d)
        kr_new = krb[..., QK_NOPE:QK_DIM]
        attn = _attn_sample(q, cache_kv_latent[i], cache_k_rope[i], lat, kr_new, lw)
        xd = _out_ffn(xd.reshape(n_d * s_d, D_MODEL), convn.reshape(n_d * s_d, D_CONV),
                      attn.reshape(n_d * s_d, D_ATTN), p_sample[i].reshape(n_d * s_d, D_PLE),
                      lw, tm_d).reshape(n_d, s_d, D_MODEL)
        lat_d.append(lat); kr_d.append(kr_new); cv_d.append(ncv)

    return (xp, xd, jnp.stack(lat_p), jnp.stack(kr_p), jnp.stack(cv_p),
            jnp.stack(lat_d), jnp.stack(kr_d), jnp.stack(cv_d))
```

```python
import functools

import jax
import jax.numpy as jnp
from jax import lax
from jax.experimental import pallas as pl
from jax.experimental.pallas import tpu as pltpu

D_MODEL = 1024
CHUNK = 64
D_PLE = 256
D_CONV = 512
CONV_W = 3
V_DIM = 64
QK_NOPE = 64
QK_ROPE = 32
QK_DIM = QK_NOPE + QK_ROPE
N_HEADS = 8
D_ATTN = N_HEADS * V_DIM
Q_LORA = 768
KV_LORA = 256
D_FF = 2816
ROPE_THETA = 10000.0
EPS = 1e-6
SCALE = QK_DIM ** -0.5

OFF_C = D_CONV
OFF_X = 2 * D_CONV
OFF_Q = 3 * D_CONV
OFF_KV = OFF_Q + Q_LORA
OFF_KR = OFF_KV + KV_LORA
IN_COLS = OFF_KR + QK_ROPE

LANES = 128
HEAD_PAD = LANES
D_HEADS_PAD = N_HEADS * HEAD_PAD
IN_EXT = OFF_KR + LANES
FF_CHUNK = 256
NEG_BIG = -1e30
Q_PRESCALE = SCALE * 1.4426950408889634
SCORE_LEAD = 3
ROW_GROUPS = 2
SUM_ROWS = 16

BF16 = jnp.bfloat16
F32 = jnp.float32


def _rms(x, g):
    ms = jnp.mean(x * x, axis=-1, keepdims=True)
    return x * lax.rsqrt(ms + EPS) * g


def _dot(a, b):
    return jnp.dot(a, b, preferred_element_type=F32)


def _dot_nt(a, b):
    return lax.dot_general(a, b, (((1,), (1,)), ((), ())), preferred_element_type=F32)


def _const_spec(shape):
    nd = len(shape)
    return pl.BlockSpec(shape, lambda *_: (0,) * nd, pipeline_mode=pl.Buffered(1))


def _in_proj_kernel(x_ref, tab_ref, cinit_ref, gpre_ref, win_ref, wconv_ref, gq_ref, wq_ref,
                    gkv_ref, gconv_ref, *rest, with_kv):
    if with_kv:
        wk_ref, wvt_ref, convn_ref, q_ref, lat_ref, kr_ref, newconv_ref, k_ref, vt_ref, carry_ref = rest
    else:
        convn_ref, q_ref, lat_ref, kr_ref, newconv_ref, carry_ref = rest
    ts = x_ref.shape[1]
    s_idx = pl.program_id(1)

    @pl.when(s_idx == 0)
    def _():
        carry_ref[0:2, :] = cinit_ref[0]

    ng = ROW_GROUPS if ts % (ROW_GROUPS * LANES) == 0 else 1
    gs = ts // ng
    groups = [slice(i * gs, (i + 1) * gs) for i in range(ng)]
    z = []
    for r in groups:
        h = _rms(x_ref[0, r, :], gpre_ref[...]).astype(BF16)
        z.append([_dot(h, win_ref[:, a:b]) for a, b in
                  ((0, OFF_C), (OFF_C, OFF_X), (OFF_X, OFF_Q), (OFF_Q, OFF_KV), (OFF_KV, OFF_KR),
                   (OFF_KR, IN_EXT))])
    cat = lambda j: z[0][j] if ng == 1 else jnp.concatenate([zg[j] for zg in z], axis=0)

    gb = cat(0)
    u = cat(1) * cat(2)
    row = lax.broadcasted_iota(jnp.int32, (ts, 1), 0)
    c0 = carry_ref[0:1, :]
    c1 = carry_ref[1:2, :]
    u_m1 = jnp.where(row == 0, c1, pltpu.roll(u, 1, axis=0))
    u_m2 = pltpu.roll(u, 2, axis=0)
    u_m2 = jnp.where(row == 0, c0, jnp.where(row == 1, c1, u_m2))
    conv = u_m2 * wconv_ref[0:1, :] + u_m1 * wconv_ref[1:2, :] + u * wconv_ref[2:3, :]
    convn_ref[0] = _rms(gb * conv, gconv_ref[...]).astype(BF16)
    tail = u[ts - 2:ts, :]
    carry_ref[0:2, :] = tail
    newconv_ref[0] = tail

    for gi, r in enumerate(groups):
        c_q = tab_ref[0, r, :]
        s_q = tab_ref[1, r, :]
        c_k = tab_ref[2, r, :]
        s_k = tab_ref[3, r, :]

        cqn = _rms(z[gi][3], gq_ref[...]).astype(BF16)
        qa = _dot(cqn, wq_ref[...])
        for hd in range(N_HEADS):
            blk = qa[:, hd * HEAD_PAD:(hd + 1) * HEAD_PAD]
            rot = blk * c_q + pltpu.roll(blk, HEAD_PAD - QK_ROPE, axis=1) * s_q
            q_ref[0, r, hd * HEAD_PAD:(hd + 1) * HEAD_PAD] = rot.astype(BF16)

        c_kv = _rms(z[gi][4], gkv_ref[...])
        lat_ref[0, r, :] = c_kv
        ckvb = c_kv.astype(BF16)
        zkr = z[gi][5]
        kr_blk = zkr * c_k + pltpu.roll(zkr, HEAD_PAD - QK_ROPE, axis=1) * s_k
        kr_ref[0, r, :] = kr_blk
        if with_kv:
            kn = _dot(ckvb, wk_ref[...])
            for hd in range(N_HEADS):
                k_ref[0, r, hd * HEAD_PAD:(hd + 1) * HEAD_PAD] = (
                    kn[:, hd * HEAD_PAD:(hd + 1) * HEAD_PAD] + kr_blk).astype(BF16)
            vt = _dot_nt(wvt_ref[...], ckvb).astype(BF16)
            tk = vt_ref.shape[3]
            start = gi * gs
            vt_ref[0, start // tk, :, start % tk:start % tk + gs] = vt


def _in_proj(x, tables, conv_init, lw, ts, tk=None):
    n, s, _ = x.shape
    with_kv = tk is not None
    grid = (n, s // ts)
    tok = lambda w: pl.BlockSpec((1, ts, w), lambda b, i: (b, i, 0))
    out_shape = [
        jax.ShapeDtypeStruct((n, s, D_CONV), BF16),
        jax.ShapeDtypeStruct((n, s, D_HEADS_PAD), BF16),
        jax.ShapeDtypeStruct((n, s, KV_LORA), F32),
        jax.ShapeDtypeStruct((n, s, LANES), F32),
        jax.ShapeDtypeStruct((n, CONV_W - 1, D_CONV), F32),
    ]
    in_specs = [
        tok(D_MODEL),
        pl.BlockSpec((4, ts, LANES), lambda b, i: (0, i, 0)),
        pl.BlockSpec((1, CONV_W - 1, D_CONV), lambda b, i: (b, 0, 0)),
        _const_spec((1, D_MODEL)),
        _const_spec((D_MODEL, IN_EXT)),
        _const_spec((CONV_W, D_CONV)),
        _const_spec((1, Q_LORA)),
        _const_spec((Q_LORA, D_HEADS_PAD)),
        _const_spec((1, KV_LORA)),
        _const_spec((1, D_CONV)),
    ]
    out_specs = [
        tok(D_CONV), tok(D_HEADS_PAD), tok(KV_LORA), tok(LANES),
        pl.BlockSpec((1, CONV_W - 1, D_CONV), lambda b, i: (b, 0, 0)),
    ]
    args = [x, tables, conv_init, lw["g_mix_pre"], lw["w_in"], lw["w_conv"], lw["g_q"], lw["w_q"],
            lw["g_kv"], lw["g_conv_out"]]
    if with_kv:
        in_specs += [_const_spec((KV_LORA, D_HEADS_PAD)), _const_spec((D_ATTN, KV_LORA))]
        args += [lw["w_k"], lw["w_vt"]]
        out_shape += [jax.ShapeDtypeStruct((n, s, D_HEADS_PAD), BF16),
                      jax.ShapeDtypeStruct((n, s // tk, D_ATTN, tk), BF16)]
        out_specs += [tok(D_HEADS_PAD),
                      pl.BlockSpec((1, ts // tk, D_ATTN, tk), lambda b, i: (b, i, 0, 0))]
    return pl.pallas_call(
        functools.partial(_in_proj_kernel, with_kv=with_kv),
        grid=grid,
        in_specs=in_specs,
        out_specs=tuple(out_specs),
        out_shape=tuple(out_shape),
        scratch_shapes=[pltpu.VMEM((8, D_CONV), F32)],
        compiler_params=pltpu.CompilerParams(
            dimension_semantics=("arbitrary", "arbitrary"),
            vmem_limit_bytes=48 * 1024 * 1024),
        name="in_proj_kv" if with_kv else "in_proj",
    )(*args)


def _attn_prompt_kernel(q_ref, k_ref, vt_ref, g_ref, o_ref, acc_ref, m_ref, ot_ref, *, tq):
    tk = vt_ref.shape[3]
    qi = pl.program_id(1)
    odd = jnp.bitwise_and(qi, 1)
    n_loop = lax.shift_right_logical(qi + 1, 1) - odd
    m_ref[...] = jnp.full(m_ref.shape, NEG_BIG, F32)
    acc_ref[...] = jnp.zeros(acc_ref.shape, F32)

    def block(kb, width, masked):
        k0 = pl.multiple_of(kb * tk, tk)
        if masked:
            ck = (lax.broadcasted_iota(jnp.int32, (width, tq), 0) + kb * tk) // CHUNK
            rq = (lax.broadcasted_iota(jnp.int32, (width, tq), 1) + qi * tq) // CHUNK
            vis = ck <= rq
        ones = jnp.ones((SUM_ROWS, width), BF16)

        def scores(hd):
            cols = slice(hd * HEAD_PAD, (hd + 1) * HEAD_PAD)
            return _dot_nt(k_ref[0, pl.ds(k0, width), cols], q_ref[0, :, cols])

        def softmax(hd, st):
            if masked:
                st = jnp.where(vis, st, NEG_BIG)
            m_prev = m_ref[hd:hd + 1, :]
            m_new = jnp.maximum(m_prev, jnp.max(st, axis=0, keepdims=True))
            m_ref[hd:hd + 1, :] = m_new
            return jnp.exp2(m_prev - m_new), jnp.exp2(st - m_new).astype(BF16)

        def weighted_values(hd, alpha, pt):
            lhs = jnp.concatenate([vt_ref[0, kb, hd * V_DIM:(hd + 1) * V_DIM, 0:width], ones], axis=0)
            acc_ref[hd] = alpha * acc_ref[hd] + _dot(lhs, pt)

        sts = {}
        for step in range(N_HEADS + SCORE_LEAD):
            if step < N_HEADS:
                sts[step] = scores(step)
            hd = step - SCORE_LEAD
            if hd >= 0:
                weighted_values(hd, *softmax(hd, sts.pop(hd)))

    def loop_body(kb, carry):
        block(kb, tk, False)
        return carry

    lax.fori_loop(0, n_loop, loop_body, 0)

    @pl.when(odd == 1)
    def _():
        block(n_loop, tk, True)

    @pl.when(odd == 0)
    def _():
        block(n_loop, tq, True)

    ssq = jnp.zeros((1, tq), F32)
    for hd in range(N_HEADS):
        a = acc_ref[hd]
        o = a[0:V_DIM, :] / a[V_DIM:V_DIM + 1, :]
        ssq = ssq + jnp.sum(o * o, axis=0, keepdims=True)
        ot_ref[hd * V_DIM:(hd + 1) * V_DIM, :] = o
    rs = lax.rsqrt(ssq * (1.0 / D_ATTN) + EPS)
    o_ref[0] = ((ot_ref[...] * rs).T * g_ref[...]).astype(BF16)


def _attn_prompt(q, k, vt, g_attn, tq):
    n, s, _ = q.shape
    tk = vt.shape[3]
    assert tk == 2 * tq and s % tk == 0
    return pl.pallas_call(
        functools.partial(_attn_prompt_kernel, tq=tq),
        grid=(n, s // tq),
        in_specs=[
            pl.BlockSpec((1, tq, D_HEADS_PAD), lambda b, i: (b, i, 0)),
            pl.BlockSpec((1, s, D_HEADS_PAD), lambda b, i: (b, 0, 0)),
            pl.BlockSpec((1, s // tk, D_ATTN, tk), lambda b, i: (b, 0, 0, 0)),
            _const_spec((1, D_ATTN)),
        ],
        out_specs=pl.BlockSpec((1, tq, D_ATTN), lambda b, i: (b, i, 0)),
        out_shape=jax.ShapeDtypeStruct((n, s, D_ATTN), BF16),
        scratch_shapes=[pltpu.VMEM((N_HEADS, V_DIM + SUM_ROWS, tq), F32),
                        pltpu.VMEM((N_HEADS, tq), F32),
                        pltpu.VMEM((D_ATTN, tq), F32)],
        compiler_params=pltpu.CompilerParams(
            dimension_semantics=("arbitrary", "arbitrary"),
            vmem_limit_bytes=48 * 1024 * 1024),
        name="attn_prompt",
    )(q, k, vt, g_attn)


def _attn_sample_kernel(q_ref, cpast_ref, krpast_ref, cnew_ref, krnew_ref, wabs_ref, psel_ref,
                        wuv_ref, g_ref, o_ref):
    l_new = q_ref.shape[1]
    q = q_ref[0]
    ql, qr = [], []
    for hd in range(N_HEADS):
        qh = q[:, hd * HEAD_PAD:(hd + 1) * HEAD_PAD]
        ql.append(_dot(qh, wabs_ref[hd]))
        qr.append(_dot(qh, psel_ref[...]))
    ql = jnp.concatenate(ql, axis=0).astype(BF16)
    qr = jnp.concatenate(qr, axis=0).astype(BF16)
    cp = cpast_ref[0].astype(BF16)
    cn = cnew_ref[0].astype(BF16)
    s_p = _dot_nt(ql, cp) + _dot_nt(qr, krpast_ref[0].astype(BF16))
    s_n = _dot_nt(ql, cn) + _dot_nt(qr, krnew_ref[0].astype(BF16))
    m = jnp.maximum(jnp.max(s_p, axis=-1, keepdims=True), jnp.max(s_n, axis=-1, keepdims=True))
    p_p = jnp.exp2(s_p - m)
    p_n = jnp.exp2(s_n - m)
    l = jnp.sum(p_p, axis=-1, keepdims=True) + jnp.sum(p_n, axis=-1, keepdims=True)
    o_lat = ((_dot(p_p.astype(BF16), cp) + _dot(p_n.astype(BF16), cn)) / l).astype(BF16)
    o = _dot(o_lat[0:l_new, :], wuv_ref[0])
    for hd in range(1, N_HEADS):
        o = o + _dot(o_lat[hd * l_new:(hd + 1) * l_new, :], wuv_ref[hd])
    o_ref[0] = _rms(o, g_ref[...]).astype(BF16)


def _attn_sample(q, c_past, kr_past, c_new, kr_new, lw):
    n, l_new, _ = q.shape
    p_len = c_past.shape[1]
    per_b = lambda rows, w: pl.BlockSpec((1, rows, w), lambda b: (b, 0, 0))
    return pl.pallas_call(
        _attn_sample_kernel,
        grid=(n,),
        in_specs=[
            per_b(l_new, D_HEADS_PAD),
            per_b(p_len, KV_LORA),
            per_b(p_len, QK_ROPE),
            per_b(l_new, KV_LORA),
            per_b(l_new, QK_ROPE),
            _const_spec((N_HEADS, HEAD_PAD, KV_LORA)),
            _const_spec((HEAD_PAD, QK_ROPE)),
            _const_spec((N_HEADS, KV_LORA, D_ATTN)),
            _const_spec((1, D_ATTN)),
        ],
        out_specs=per_b(l_new, D_ATTN),
        out_shape=jax.ShapeDtypeStruct((n, l_new, D_ATTN), BF16),
        compiler_params=pltpu.CompilerParams(
            dimension_semantics=("arbitrary",),
            vmem_limit_bytes=48 * 1024 * 1024),
        name="attn_sample",
    )(q, c_past, kr_past, c_new, kr_new, lw["w_abs"], lw["p_sel"], lw["w_uv"], lw["g_attn"])


def _out_ffn_kernel(x_ref, convn_ref, attn_ref, p_ref, woc_ref, woa_ref, gpost_ref, gffn_ref,
                    wg_ref, wu_ref, wd_ref, gfpost_ref, wpg_ref, wpp_ref, y_ref, act_ref):
    tm = x_ref.shape[0]
    groups = [slice(i * (tm // ROW_GROUPS), (i + 1) * (tm // ROW_GROUPS)) for i in range(ROW_GROUPS)]
    mix = [_dot(convn_ref[r, :], woc_ref[...]) + _dot(attn_ref[r, :], woa_ref[...]) for r in groups]
    x1, h = [], []
    for r, mx in zip(groups, mix):
        x1.append(x_ref[r, :] + _rms(mx, gpost_ref[...]))
        h.append(_rms(x1[-1], gffn_ref[...]).astype(BF16))
    for r, hr in zip(groups, h):
        for c in range(D_FF // FF_CHUNK):
            cols = slice(c * FF_CHUNK, (c + 1) * FF_CHUNK)
            g = _dot(hr, wg_ref[:, cols])
            u = _dot(hr, wu_ref[:, cols])
            act_ref[r, cols] = (g * jax.nn.sigmoid(g) * u).astype(BF16)
    f = [_dot(act_ref[r, :], wd_ref[...]) for r in groups]
    for r, x1r, fr in zip(groups, x1, f):
        pp = _dot(p_ref[r, :].astype(BF16), wpp_ref[...])
        x2 = x1r + _rms(fr, gfpost_ref[...])
        gate = jax.nn.sigmoid(_dot(x2.astype(BF16), wpg_ref[...]))
        y_ref[r, :] = x2 + gate * pp


def _out_ffn(x, convn, attn, p, lw, tm):
    m = x.shape[0]
    tok = lambda w: pl.BlockSpec((tm, w), lambda i: (i, 0))
    return pl.pallas_call(
        _out_ffn_kernel,
        grid=(m // tm,),
        in_specs=[
            tok(D_MODEL), tok(D_CONV), tok(D_ATTN), tok(D_PLE),
            _const_spec((D_CONV, D_MODEL)),
            _const_spec((D_ATTN, D_MODEL)),
            _const_spec((1, D_MODEL)),
            _const_spec((1, D_MODEL)),
            _const_spec((D_MODEL, D_FF)),
            _const_spec((D_MODEL, D_FF)),
            _const_spec((D_FF, D_MODEL)),
            _const_spec((1, D_MODEL)),
            _const_spec((D_MODEL, D_MODEL)),
            _const_spec((D_PLE, D_MODEL)),
        ],
        out_specs=tok(D_MODEL),
        out_shape=jax.ShapeDtypeStruct((m, D_MODEL), F32),
        scratch_shapes=[pltpu.VMEM((tm, D_FF), BF16)],
        compiler_params=pltpu.CompilerParams(
            dimension_semantics=("arbitrary",),
            vmem_limit_bytes=56 * 1024 * 1024),
        name="out_ffn",
    )(x, convn, attn, p, lw["w_o_conv"], lw["w_o_attn"], lw["g_mix_post"], lw["g_ffn_pre"],
      lw["w_gate"], lw["w_up"], lw["w_down"], lw["g_ffn_post"], lw["w_ple_gate"], lw["w_ple_proj"])


def _swap_halves(a):
    half = a.shape[-1] // 2
    return jnp.concatenate([a[..., half:], a[..., :half]], axis=-1)


def _layer_weights(i, g_mix_pre, w_in, w_conv, g_q, w_uq, g_kv, w_ukv, g_conv_out, g_attn_out, w_o,
                   g_mix_post, g_ffn_pre, w_ffn_gate, w_ffn_up, w_ffn_down, g_ffn_post,
                   w_ple_proj, w_ple_gate):
    row = lambda g: g[i].reshape(1, -1).astype(F32)
    w_kr = w_in[i][:, OFF_KR:IN_COLS]
    w_in_ext = jnp.concatenate(
        [w_in[i][:, :OFF_KR], jnp.zeros((D_MODEL, QK_NOPE), F32), w_kr, _swap_halves(w_kr)], axis=1)
    wq3 = w_uq[i].reshape(Q_LORA, N_HEADS, QK_DIM)
    w_q = jnp.concatenate(
        [wq3, _swap_halves(wq3[..., QK_NOPE:])], axis=-1).reshape(Q_LORA, D_HEADS_PAD)
    wkv3 = w_ukv[i].reshape(KV_LORA, N_HEADS, QK_NOPE + V_DIM)
    w_uk = wkv3[..., :QK_NOPE]
    w_uv = wkv3[..., QK_NOPE:]
    zpad = jnp.zeros((KV_LORA, N_HEADS, HEAD_PAD - QK_NOPE), F32)
    w_k = jnp.concatenate([w_uk, zpad], axis=-1).reshape(KV_LORA, D_HEADS_PAD)
    w_vt = w_uv.reshape(KV_LORA, D_ATTN).T
    w_abs = jnp.concatenate([w_uk, zpad], axis=-1).transpose(1, 2, 0)
    head_of_col = jnp.arange(D_ATTN) // V_DIM
    w_uv_heads = jnp.where(head_of_col[None, None, :] == jnp.arange(N_HEADS)[:, None, None],
                           w_uv.reshape(1, KV_LORA, D_ATTN), 0.0)
    p_sel = jnp.zeros((HEAD_PAD, QK_ROPE), F32).at[QK_NOPE:QK_DIM, :].set(jnp.eye(QK_ROPE, dtype=F32))
    return dict(
        g_mix_pre=row(g_mix_pre), w_in=w_in_ext.astype(BF16), w_conv=w_conv[i].astype(F32),
        g_q=row(g_q), w_q=w_q.astype(BF16), g_kv=row(g_kv), w_k=w_k.astype(BF16), w_vt=w_vt.astype(BF16),
        g_conv_out=row(g_conv_out), g_attn=row(g_attn_out),
        w_abs=w_abs.astype(BF16), w_uv=w_uv_heads.astype(BF16), p_sel=p_sel.astype(BF16),
        w_o_conv=w_o[i][:D_CONV].astype(BF16), w_o_attn=w_o[i][D_CONV:].astype(BF16),
        g_mix_post=row(g_mix_post), g_ffn_pre=row(g_ffn_pre),
        w_gate=w_ffn_gate[i].astype(BF16), w_up=w_ffn_up[i].astype(BF16), w_down=w_ffn_down[i].astype(BF16),
        g_ffn_post=row(g_ffn_post), w_ple_gate=w_ple_gate[i].astype(BF16), w_ple_proj=w_ple_proj[i].astype(BF16),
    )


def _rope_tables(pos):
    half = QK_ROPE // 2
    inv = ROPE_THETA ** (-jnp.arange(half, dtype=F32) / half)
    ang = pos.astype(F32)[:, None] * inv[None, :]
    cos, sin = jnp.cos(ang), jnp.sin(ang)
    n = pos.shape[0]
    cc = jnp.concatenate([cos, cos], axis=1)
    ss = jnp.concatenate([-sin, sin], axis=1)
    z_lo = jnp.zeros((n, QK_NOPE), F32)
    z_hi = jnp.zeros((n, HEAD_PAD - QK_DIM), F32)
    c_q = jnp.concatenate([jnp.ones((n, QK_NOPE), F32), cc, z_hi], axis=1) * Q_PRESCALE
    s_q = jnp.concatenate([z_lo, ss, z_hi], axis=1)
    c_k = jnp.concatenate([z_lo, cc, z_hi], axis=1)
    return jnp.stack([c_q, s_q * Q_PRESCALE, c_k, s_q])


def _pick_tile(n, cap):
    t = min(n, cap)
    while n % t:
        t //= 2
    return t


def kernel(x_prompt, x_sample, cache_kv_latent, cache_k_rope, state_conv, p_prompt, p_sample, g_mix_pre, w_in, w_conv, g_q, w_uq, g_kv, w_ukv, g_conv_out, g_attn_out, w_o, g_mix_post, g_ffn_pre, w_ffn_gate, w_ffn_up, w_ffn_down, g_ffn_post, w_ple_proj, w_ple_gate):
    depth = w_in.shape[0]
    n_p, s_p, _ = x_prompt.shape
    n_d, s_d, _ = x_sample.shape
    past_len = cache_kv_latent.shape[2]
    tab_p = _rope_tables(jnp.arange(s_p))
    tab_d = _rope_tables(past_len + jnp.arange(s_d))
    conv_zero = jnp.zeros((n_p, CONV_W - 1, D_CONV), F32)
    ts_p = _pick_tile(s_p, 512)
    tq = _pick_tile(s_p, 256)
    tm_p = _pick_tile(n_p * s_p, 512)
    tm_d = _pick_tile(n_d * s_d, 256)

    xp, xd = x_prompt, x_sample
    lat_p, kr_p, cv_p, lat_d, kr_d, cv_d = [], [], [], [], [], []
    for i in range(depth):
        lw = _layer_weights(i, g_mix_pre, w_in, w_conv, g_q, w_uq, g_kv, w_ukv, g_conv_out,
                            g_attn_out, w_o, g_mix_post, g_ffn_pre, w_ffn_gate, w_ffn_up,
                            w_ffn_down, g_ffn_post, w_ple_proj, w_ple_gate)
        convn, q, lat, krb, ncv, k, vt = _in_proj(xp, tab_p, conv_zero, lw, ts_p, tk=2 * tq)
        attn = _attn_prompt(q, k, vt, lw["g_attn"], tq)
        xp = _out_ffn(xp.reshape(n_p * s_p, D_MODEL), convn.reshape(n_p * s_p, D_CONV),
                      attn.reshape(n_p * s_p, D_ATTN), p_prompt[i].reshape(n_p * s_p, D_PLE),
                      lw, tm_p).reshape(n_p, s_p, D_MODEL)
        lat_p.append(lat); kr_p.append(krb[..., QK_NOPE:QK_DIM]); cv_p.append(ncv)
        convn, q, lat, krb, ncv = _in_proj(xd, tab_d, state_conv[i], lw, s_d)
        kr_new = krb[..., QK_NOPE:QK_DIM]
        attn = _attn_sample(q, cache_kv_latent[i], cache_k_rope[i], lat, kr_new, lw)
        xd = _out_ffn(xd.reshape(n_d * s_d, D_MODEL), convn.reshape(n_d * s_d, D_CONV),
                      attn.reshape(n_d * s_d, D_ATTN), p_sample[i].reshape(n_d * s_d, D_PLE),
                      lw, tm_d).reshape(n_d, s_d, D_MODEL)
        lat_d.append(lat); kr_d.append(kr_new); cv_d.append(ncv)

    return (xp, xd, jnp.stack(lat_p), jnp.stack(kr_p), jnp.stack(cv_p),
            jnp.stack(lat_d), jnp.stack(kr_d), jnp.stack(cv_d))
```

```python
import functools

import jax
import jax.numpy as jnp
from jax import lax
from jax.experimental import pallas as pl
from jax.experimental.pallas import tpu as pltpu

D_MODEL = 1024
CHUNK = 64
D_PLE = 256
D_CONV = 512
CONV_W = 3
V_DIM = 64
QK_NOPE = 64
QK_ROPE = 32
QK_DIM = QK_NOPE + QK_ROPE
N_HEADS = 8
D_ATTN = N_HEADS * V_DIM
Q_LORA = 768
KV_LORA = 256
D_FF = 2816
ROPE_THETA = 10000.0
EPS = 1e-6
SCALE = QK_DIM ** -0.5

OFF_C = D_CONV
OFF_X = 2 * D_CONV
OFF_Q = 3 * D_CONV
OFF_KV = OFF_Q + Q_LORA
OFF_KR = OFF_KV + KV_LORA
IN_COLS = OFF_KR + QK_ROPE

LANES = 128
HEAD_PAD = LANES
D_HEADS_PAD = N_HEADS * HEAD_PAD
IN_EXT = OFF_KR + LANES
FF_CHUNK = 256
NEG_BIG = -1e30
Q_PRESCALE = SCALE * 1.4426950408889634
SCORE_LEAD = 3
ROW_GROUPS = 2
SUM_ROWS = 16

BF16 = jnp.bfloat16
F32 = jnp.float32


def _rms(x, g):
    ms = jnp.mean(x * x, axis=-1, keepdims=True)
    return x * lax.rsqrt(ms + EPS) * g


def _dot(a, b):
    return jnp.dot(a, b, preferred_element_type=F32)


def _dot_nt(a, b):
    return lax.dot_general(a, b, (((1,), (1,)), ((), ())), preferred_element_type=F32)


def _const_spec(shape):
    nd = len(shape)
    return pl.BlockSpec(shape, lambda *_: (0,) * nd, pipeline_mode=pl.Buffered(1))


def _in_proj_kernel(x_ref, tab_ref, cinit_ref, gpre_ref, win_ref, wconv_ref, gq_ref, wq_ref,
                    gkv_ref, gconv_ref, *rest, with_kv, n_alias):
    if with_kv:
        wk_ref, wvt_ref = rest[:2]
        rest = rest[2:]
    rest = rest[n_alias:]
    if with_kv:
        convn_ref, q_ref, lat_ref, kr_ref, newconv_ref, k_ref, vt_ref, carry_ref = rest
    else:
        convn_ref, q_ref, lat_ref, kr_ref, newconv_ref, carry_ref = rest
    ts = x_ref.shape[1]
    s_idx = pl.program_id(1)

    @pl.when(s_idx == 0)
    def _():
        carry_ref[0:2, :] = cinit_ref[0]

    ng = ROW_GROUPS if ts % (ROW_GROUPS * LANES) == 0 else 1
    gs = ts // ng
    groups = [slice(i * gs, (i + 1) * gs) for i in range(ng)]
    z = []
    for r in groups:
        h = _rms(x_ref[0, r, :], gpre_ref[...]).astype(BF16)
        z.append([_dot(h, win_ref[:, a:b]) for a, b in
                  ((0, OFF_C), (OFF_C, OFF_X), (OFF_X, OFF_Q), (OFF_Q, OFF_KV), (OFF_KV, OFF_KR),
                   (OFF_KR, IN_EXT))])
    cat = lambda j: z[0][j] if ng == 1 else jnp.concatenate([zg[j] for zg in z], axis=0)

    gb = cat(0)
    u = cat(1) * cat(2)
    row = lax.broadcasted_iota(jnp.int32, (ts, 1), 0)
    c0 = carry_ref[0:1, :]
    c1 = carry_ref[1:2, :]
    u_m1 = jnp.where(row == 0, c1, pltpu.roll(u, 1, axis=0))
    u_m2 = pltpu.roll(u, 2, axis=0)
    u_m2 = jnp.where(row == 0, c0, jnp.where(row == 1, c1, u_m2))
    conv = u_m2 * wconv_ref[0:1, :] + u_m1 * wconv_ref[1:2, :] + u * wconv_ref[2:3, :]
    convn_ref[0] = _rms(gb * conv, gconv_ref[...]).astype(BF16)
    tail = u[ts - 2:ts, :]
    carry_ref[0:2, :] = tail
    newconv_ref[0] = tail

    for gi, r in enumerate(groups):
        c_q = tab_ref[0, r, :]
        s_q = tab_ref[1, r, :]
        c_k = tab_ref[2, r, :]
        s_k = tab_ref[3, r, :]

        cqn = _rms(z[gi][3], gq_ref[...]).astype(BF16)
        qa = _dot(cqn, wq_ref[...])
        for hd in range(N_HEADS):
            blk = qa[:, hd * HEAD_PAD:(hd + 1) * HEAD_PAD]
            rot = blk * c_q + pltpu.roll(blk, HEAD_PAD - QK_ROPE, axis=1) * s_q
            q_ref[0, r, hd * HEAD_PAD:(hd + 1) * HEAD_PAD] = rot.astype(BF16)

        c_kv = _rms(z[gi][4], gkv_ref[...])
        lat_ref[0, r, :] = c_kv
        ckvb = c_kv.astype(BF16)
        zkr = z[gi][5]
        kr_blk = zkr * c_k + pltpu.roll(zkr, HEAD_PAD - QK_ROPE, axis=1) * s_k
        kr_ref[0, r, :] = kr_blk[:, QK_NOPE:QK_DIM]
        if with_kv:
            kn = _dot(ckvb, wk_ref[...])
            for hd in range(N_HEADS):
                k_ref[0, r, hd * HEAD_PAD:(hd + 1) * HEAD_PAD] = (
                    kn[:, hd * HEAD_PAD:(hd + 1) * HEAD_PAD] + kr_blk).astype(BF16)
            vt = _dot_nt(wvt_ref[...], ckvb).astype(BF16)
            tk = vt_ref.shape[3]
            start = gi * gs
            vt_ref[0, start // tk, :, start % tk:start % tk + gs] = vt


def _in_proj(x, tables, conv_init, conv_layer, lw, layer, depth, ts, tk=None, stacked=None):
    n, s, _ = x.shape
    with_kv = tk is not None
    grid = (n, s // ts)
    tok = lambda w: pl.BlockSpec((1, ts, w), lambda b, i: (b, i, 0))
    lay = lambda rows, w, tiled: pl.BlockSpec(
        (None, 1, rows, w), (lambda b, i: (layer, b, i, 0)) if tiled else (lambda b, i: (layer, b, 0, 0)))
    out_shape = [
        jax.ShapeDtypeStruct((n, s, D_CONV), BF16),
        jax.ShapeDtypeStruct((n, s, D_HEADS_PAD), BF16),
        jax.ShapeDtypeStruct((depth, n, s, KV_LORA), F32),
        jax.ShapeDtypeStruct((depth, n, s, QK_ROPE), F32),
        jax.ShapeDtypeStruct((depth, n, CONV_W - 1, D_CONV), F32),
    ]
    in_specs = [
        tok(D_MODEL),
        pl.BlockSpec((4, ts, LANES), lambda b, i: (0, i, 0)),
        pl.BlockSpec((None, 1, CONV_W - 1, D_CONV), lambda b, i: (conv_layer, b, 0, 0)),
        _const_spec((1, D_MODEL)),
        _const_spec((D_MODEL, IN_EXT)),
        _const_spec((CONV_W, D_CONV)),
        _const_spec((1, Q_LORA)),
        _const_spec((Q_LORA, D_HEADS_PAD)),
        _const_spec((1, KV_LORA)),
        _const_spec((1, D_CONV)),
    ]
    out_specs = [
        tok(D_CONV), tok(D_HEADS_PAD), lay(ts, KV_LORA, True), lay(ts, QK_ROPE, True),
        lay(CONV_W - 1, D_CONV, False),
    ]
    args = [x, tables, conv_init, lw["g_mix_pre"], lw["w_in"], lw["w_conv"], lw["g_q"], lw["w_q"],
            lw["g_kv"], lw["g_conv_out"]]
    if with_kv:
        in_specs += [_const_spec((KV_LORA, D_HEADS_PAD)), _const_spec((D_ATTN, KV_LORA))]
        args += [lw["w_k"], lw["w_vt"]]
        out_shape += [jax.ShapeDtypeStruct((n, s, D_HEADS_PAD), BF16),
                      jax.ShapeDtypeStruct((n, s // tk, D_ATTN, tk), BF16)]
        out_specs += [tok(D_HEADS_PAD),
                      pl.BlockSpec((1, ts // tk, D_ATTN, tk), lambda b, i: (b, i, 0, 0))]
    aliases = {}
    if stacked is not None:
        for j, arr in enumerate(stacked):
            aliases[len(args)] = 2 + j
            in_specs.append(pl.BlockSpec(memory_space=pl.ANY))
            args.append(arr)
    return pl.pallas_call(
        functools.partial(_in_proj_kernel, with_kv=with_kv, n_alias=len(aliases)),
        grid=grid,
        in_specs=in_specs,
        out_specs=tuple(out_specs),
        out_shape=tuple(out_shape),
        input_output_aliases=aliases,
        scratch_shapes=[pltpu.VMEM((8, D_CONV), F32)],
        compiler_params=pltpu.CompilerParams(
            dimension_semantics=("arbitrary", "arbitrary"),
            vmem_limit_bytes=48 * 1024 * 1024),
        name="in_proj_kv" if with_kv else "in_proj",
    )(*args)


def _attn_prompt_kernel(q_ref, k_ref, vt_ref, g_ref, o_ref, acc_ref, m_ref, ot_ref, ahead_ref, *, tq):
    tk = vt_ref.shape[3]
    qi = pl.program_id(1)
    odd = jnp.bitwise_and(qi, 1)
    n_loop = lax.shift_right_logical(qi + 1, 1) - odd
    m_ref[...] = jnp.full(m_ref.shape, NEG_BIG, F32)
    acc_ref[...] = jnp.zeros(acc_ref.shape, F32)

    def scores(kb, hd, width):
        cols = slice(hd * HEAD_PAD, (hd + 1) * HEAD_PAD)
        k0 = pl.multiple_of(kb * tk, tk)
        return _dot_nt(k_ref[0, pl.ds(k0, width), cols], q_ref[0, :, cols])

    for hd in range(SCORE_LEAD):
        ahead_ref[hd] = scores(0, hd, tk)

    def block(kb, width, masked, next_kb):
        if masked:
            ck = (lax.broadcasted_iota(jnp.int32, (width, tq), 0) + kb * tk) // CHUNK
            rq = (lax.broadcasted_iota(jnp.int32, (width, tq), 1) + qi * tq) // CHUNK
            vis = ck <= rq
        ones = jnp.ones((SUM_ROWS, width), BF16)

        def softmax(hd, st):
            if masked:
                st = jnp.where(vis, st, NEG_BIG)
            m_prev = m_ref[hd:hd + 1, :]
            m_new = jnp.maximum(m_prev, jnp.max(st, axis=0, keepdims=True))
            m_ref[hd:hd + 1, :] = m_new
            return jnp.exp2(m_prev - m_new), jnp.exp2(st - m_new).astype(BF16)

        def weighted_values(hd, alpha, pt):
            lhs = jnp.concatenate([vt_ref[0, kb, hd * V_DIM:(hd + 1) * V_DIM, 0:width], ones], axis=0)
            acc_ref[hd] = alpha * acc_ref[hd] + _dot(lhs, pt)

        sts = {hd: ahead_ref[hd, 0:width, :] for hd in range(SCORE_LEAD)}
        for hd in range(N_HEADS):
            lead = hd + SCORE_LEAD
            if lead < N_HEADS:
                sts[lead] = scores(kb, lead, width)
            elif next_kb is not None:
                ahead_ref[lead - N_HEADS] = scores(next_kb, lead - N_HEADS, tk)
            weighted_values(hd, *softmax(hd, sts.pop(hd)))

    def loop_body(kb, carry):
        block(kb, tk, False, kb + 1)
        return carry

    lax.fori_loop(0, n_loop, loop_body, 0)

    @pl.when(odd == 1)
    def _():
        block(n_loop, tk, True, None)

    @pl.when(odd == 0)
    def _():
        block(n_loop, tq, True, None)

    ssq = jnp.zeros((1, tq), F32)
    for hd in range(N_HEADS):
        a = acc_ref[hd]
        o = a[0:V_DIM, :] / a[V_DIM:V_DIM + 1, :]
        ssq = ssq + jnp.sum(o * o, axis=0, keepdims=True)
        ot_ref[hd * V_DIM:(hd + 1) * V_DIM, :] = o
    rs = lax.rsqrt(ssq * (1.0 / D_ATTN) + EPS)
    o_ref[0] = ((ot_ref[...] * rs).T * g_ref[...]).astype(BF16)


def _attn_prompt(q, k, vt, g_attn, tq):
    n, s, _ = q.shape
    tk = vt.shape[3]
    assert tk == 2 * tq and s % tk == 0
    return pl.pallas_call(
        functools.partial(_attn_prompt_kernel, tq=tq),
        grid=(n, s // tq),
        in_specs=[
            pl.BlockSpec((1, tq, D_HEADS_PAD), lambda b, i: (b, i, 0)),
            pl.BlockSpec((1, s, D_HEADS_PAD), lambda b, i: (b, 0, 0)),
            pl.BlockSpec((1, s // tk, D_ATTN, tk), lambda b, i: (b, 0, 0, 0)),
            _const_spec((1, D_ATTN)),
        ],
        out_specs=pl.BlockSpec((1, tq, D_ATTN), lambda b, i: (b, i, 0)),
        out_shape=jax.ShapeDtypeStruct((n, s, D_ATTN), BF16),
        scratch_shapes=[pltpu.VMEM((N_HEADS, V_DIM + SUM_ROWS, tq), F32),
                        pltpu.VMEM((N_HEADS, tq), F32),
                        pltpu.VMEM((D_ATTN, tq), F32),
                        pltpu.VMEM((SCORE_LEAD, tk, tq), F32)],
        compiler_params=pltpu.CompilerParams(
            dimension_semantics=("arbitrary", "arbitrary"),
            vmem_limit_bytes=48 * 1024 * 1024),
        name="attn_prompt",
    )(q, k, vt, g_attn)


def _attn_sample_kernel(q_ref, cpast_ref, krpast_ref, cnew_ref, krnew_ref, wabs_ref, psel_ref,
                        wuv_ref, g_ref, o_ref):
    l_new = q_ref.shape[1]
    q = q_ref[0]
    ql, qr = [], []
    for hd in range(N_HEADS):
        qh = q[:, hd * HEAD_PAD:(hd + 1) * HEAD_PAD]
        ql.append(_dot(qh, wabs_ref[hd]))
        qr.append(_dot(qh, psel_ref[...]))
    ql = jnp.concatenate(ql, axis=0).astype(BF16)
    qr = jnp.concatenate(qr, axis=0).astype(BF16)
    cp = cpast_ref[0].astype(BF16)
    cn = cnew_ref[0].astype(BF16)
    s_p = _dot_nt(ql, cp) + _dot_nt(qr, krpast_ref[0].astype(BF16))
    s_n = _dot_nt(ql, cn) + _dot_nt(qr, krnew_ref[0].astype(BF16))
    m = jnp.maximum(jnp.max(s_p, axis=-1, keepdims=True), jnp.max(s_n, axis=-1, keepdims=True))
    p_p = jnp.exp2(s_p - m)
    p_n = jnp.exp2(s_n - m)
    l = jnp.sum(p_p, axis=-1, keepdims=True) + jnp.sum(p_n, axis=-1, keepdims=True)
    o_lat = ((_dot(p_p.astype(BF16), cp) + _dot(p_n.astype(BF16), cn)) / l).astype(BF16)
    o = _dot(o_lat[0:l_new, :], wuv_ref[0])
    for hd in range(1, N_HEADS):
        o = o + _dot(o_lat[hd * l_new:(hd + 1) * l_new, :], wuv_ref[hd])
    o_ref[0] = _rms(o, g_ref[...]).astype(BF16)


def _attn_sample(q, c_past, kr_past, c_new, kr_new, layer, lw):
    n, l_new, _ = q.shape
    p_len = c_past.shape[2]
    per_b = lambda rows, w: pl.BlockSpec((1, rows, w), lambda b: (b, 0, 0))
    lay = lambda rows, w: pl.BlockSpec((None, 1, rows, w), lambda b: (layer, b, 0, 0))
    return pl.pallas_call(
        _attn_sample_kernel,
        grid=(n,),
        in_specs=[
            per_b(l_new, D_HEADS_PAD),
            lay(p_len, KV_LORA),
            lay(p_len, QK_ROPE),
            lay(l_new, KV_LORA),
            lay(l_new, QK_ROPE),
            _const_spec((N_HEADS, HEAD_PAD, KV_LORA)),
            _const_spec((HEAD_PAD, QK_ROPE)),
            _const_spec((N_HEADS, KV_LORA, D_ATTN)),
            _const_spec((1, D_ATTN)),
        ],
        out_specs=per_b(l_new, D_ATTN),
        out_shape=jax.ShapeDtypeStruct((n, l_new, D_ATTN), BF16),
        compiler_params=pltpu.CompilerParams(
            dimension_semantics=("arbitrary",),
            vmem_limit_bytes=48 * 1024 * 1024),
        name="attn_sample",
    )(q, c_past, kr_past, c_new, kr_new, lw["w_abs"], lw["p_sel"], lw["w_uv"], lw["g_attn"])


def _out_ffn_kernel(x_ref, convn_ref, attn_ref, p_ref, woc_ref, woa_ref, gpost_ref, gffn_ref,
                    wg_ref, wu_ref, wd_ref, gfpost_ref, wpg_ref, wpp_ref, y_ref, act_ref):
    tm = x_ref.shape[0]
    groups = [slice(i * (tm // ROW_GROUPS), (i + 1) * (tm // ROW_GROUPS)) for i in range(ROW_GROUPS)]
    mix = [_dot(convn_ref[r, :], woc_ref[...]) + _dot(attn_ref[r, :], woa_ref[...]) for r in groups]
    x1, h = [], []
    for r, mx in zip(groups, mix):
        x1.append(x_ref[r, :] + _rms(mx, gpost_ref[...]))
        h.append(_rms(x1[-1], gffn_ref[...]).astype(BF16))
    for r, hr in zip(groups, h):
        for c in range(D_FF // FF_CHUNK):
            cols = slice(c * FF_CHUNK, (c + 1) * FF_CHUNK)
            g = _dot(hr, wg_ref[:, cols])
            u = _dot(hr, wu_ref[:, cols])
            act_ref[r, cols] = (g * jax.nn.sigmoid(g) * u).astype(BF16)
    f = [_dot(act_ref[r, :], wd_ref[...]) for r in groups]
    for r, x1r, fr in zip(groups, x1, f):
        pp = _dot(p_ref[r, :].astype(BF16), wpp_ref[...])
        x2 = x1r + _rms(fr, gfpost_ref[...])
        gate = jax.nn.sigmoid(_dot(x2.astype(BF16), wpg_ref[...]))
        y_ref[r, :] = x2 + gate * pp


def _out_ffn(x, convn, attn, p, layer, lw, tm):
    m = x.shape[0]
    tok = lambda w: pl.BlockSpec((tm, w), lambda i: (i, 0))
    return pl.pallas_call(
        _out_ffn_kernel,
        grid=(m // tm,),
        in_specs=[
            tok(D_MODEL), tok(D_CONV), tok(D_ATTN),
            pl.BlockSpec((None, tm, D_PLE), lambda i: (layer, i, 0)),
            _const_spec((D_CONV, D_MODEL)),
            _const_spec((D_ATTN, D_MODEL)),
            _const_spec((1, D_MODEL)),
            _const_spec((1, D_MODEL)),
            _const_spec((D_MODEL, D_FF)),
            _const_spec((D_MODEL, D_FF)),
            _const_spec((D_FF, D_MODEL)),
            _const_spec((1, D_MODEL)),
            _const_spec((D_MODEL, D_MODEL)),
            _const_spec((D_PLE, D_MODEL)),
        ],
        out_specs=tok(D_MODEL),
        out_shape=jax.ShapeDtypeStruct((m, D_MODEL), F32),
        scratch_shapes=[pltpu.VMEM((tm, D_FF), BF16)],
        compiler_params=pltpu.CompilerParams(
            dimension_semantics=("arbitrary",),
            vmem_limit_bytes=56 * 1024 * 1024),
        name="out_ffn",
    )(x, convn, attn, p, lw["w_o_conv"], lw["w_o_attn"], lw["g_mix_post"], lw["g_ffn_pre"],
      lw["w_gate"], lw["w_up"], lw["w_down"], lw["g_ffn_post"], lw["w_ple_gate"], lw["w_ple_proj"])


def _swap_halves(a):
    half = a.shape[-1] // 2
    return jnp.concatenate([a[..., half:], a[..., :half]], axis=-1)


def _layer_weights(i, g_mix_pre, w_in, w_conv, g_q, w_uq, g_kv, w_ukv, g_conv_out, g_attn_out, w_o,
                   g_mix_post, g_ffn_pre, w_ffn_gate, w_ffn_up, w_ffn_down, g_ffn_post,
                   w_ple_proj, w_ple_gate):
    row = lambda g: g[i].reshape(1, -1).astype(F32)
    w_kr = w_in[i][:, OFF_KR:IN_COLS]
    w_in_ext = jnp.concatenate(
        [w_in[i][:, :OFF_KR], jnp.zeros((D_MODEL, QK_NOPE), F32), w_kr, _swap_halves(w_kr)], axis=1)
    wq3 = w_uq[i].reshape(Q_LORA, N_HEADS, QK_DIM)
    w_q = jnp.concatenate(
        [wq3, _swap_halves(wq3[..., QK_NOPE:])], axis=-1).reshape(Q_LORA, D_HEADS_PAD)
    wkv3 = w_ukv[i].reshape(KV_LORA, N_HEADS, QK_NOPE + V_DIM)
    w_uk = wkv3[..., :QK_NOPE]
    w_uv = wkv3[..., QK_NOPE:]
    zpad = jnp.zeros((KV_LORA, N_HEADS, HEAD_PAD - QK_NOPE), F32)
    w_k = jnp.concatenate([w_uk, zpad], axis=-1).reshape(KV_LORA, D_HEADS_PAD)
    w_vt = w_uv.reshape(KV_LORA, D_ATTN).T
    w_abs = jnp.concatenate([w_uk, zpad], axis=-1).transpose(1, 2, 0)
    head_of_col = jnp.arange(D_ATTN) // V_DIM
    w_uv_heads = jnp.where(head_of_col[None, None, :] == jnp.arange(N_HEADS)[:, None, None],
                           w_uv.reshape(1, KV_LORA, D_ATTN), 0.0)
    p_sel = jnp.zeros((HEAD_PAD, QK_ROPE), F32).at[QK_NOPE:QK_DIM, :].set(jnp.eye(QK_ROPE, dtype=F32))
    return dict(
        g_mix_pre=row(g_mix_pre), w_in=w_in_ext.astype(BF16), w_conv=w_conv[i].astype(F32),
        g_q=row(g_q), w_q=w_q.astype(BF16), g_kv=row(g_kv), w_k=w_k.astype(BF16), w_vt=w_vt.astype(BF16),
        g_conv_out=row(g_conv_out), g_attn=row(g_attn_out),
        w_abs=w_abs.astype(BF16), w_uv=w_uv_heads.astype(BF16), p_sel=p_sel.astype(BF16),
        w_o_conv=w_o[i][:D_CONV].astype(BF16), w_o_attn=w_o[i][D_CONV:].astype(BF16),
        g_mix_post=row(g_mix_post), g_ffn_pre=row(g_ffn_pre),
        w_gate=w_ffn_gate[i].astype(BF16), w_up=w_ffn_up[i].astype(BF16), w_down=w_ffn_down[i].astype(BF16),
        g_ffn_post=row(g_ffn_post), w_ple_gate=w_ple_gate[i].astype(BF16), w_ple_proj=w_ple_proj[i].astype(BF16),
    )


def _rope_tables(pos):
    half = QK_ROPE // 2
    inv = ROPE_THETA ** (-jnp.arange(half, dtype=F32) / half)
    ang = pos.astype(F32)[:, None] * inv[None, :]
    cos, sin = jnp.cos(ang), jnp.sin(ang)
    n = pos.shape[0]
    cc = jnp.concatenate([cos, cos], axis=1)
    ss = jnp.concatenate([-sin, sin], axis=1)
    z_lo = jnp.zeros((n, QK_NOPE), F32)
    z_hi = jnp.zeros((n, HEAD_PAD - QK_DIM), F32)
    c_q = jnp.concatenate([jnp.ones((n, QK_NOPE), F32), cc, z_hi], axis=1) * Q_PRESCALE
    s_q = jnp.concatenate([z_lo, ss, z_hi], axis=1)
    c_k = jnp.concatenate([z_lo, cc, z_hi], axis=1)
    return jnp.stack([c_q, s_q * Q_PRESCALE, c_k, s_q])


def _pick_tile(n, cap):
    t = min(n, cap)
    while n % t:
        t //= 2
    return t


def kernel(x_prompt, x_sample, cache_kv_latent, cache_k_rope, state_conv, p_prompt, p_sample, g_mix_pre, w_in, w_conv, g_q, w_uq, g_kv, w_ukv, g_conv_out, g_attn_out, w_o, g_mix_post, g_ffn_pre, w_ffn_gate, w_ffn_up, w_ffn_down, g_ffn_post, w_ple_proj, w_ple_gate):
    depth = w_in.shape[0]
    n_p, s_p, _ = x_prompt.shape
    n_d, s_d, _ = x_sample.shape
    past_len = cache_kv_latent.shape[2]
    tab_p = _rope_tables(jnp.arange(s_p))
    tab_d = _rope_tables(past_len + jnp.arange(s_d))
    conv_zero = jnp.zeros((1, n_p, CONV_W - 1, D_CONV), F32)
    ts_p = _pick_tile(s_p, 512)
    tq = _pick_tile(s_p, 256)
    tm_p = _pick_tile(n_p * s_p, 512)
    tm_d = _pick_tile(n_d * s_d, 256)
    pp_flat = p_prompt.reshape(depth, n_p * s_p, D_PLE)
    pd_flat = p_sample.reshape(depth, n_d * s_d, D_PLE)

    xp, xd = x_prompt, x_sample
    stack_p = stack_d = None
    for i in range(depth):
        lw = _layer_weights(i, g_mix_pre, w_in, w_conv, g_q, w_uq, g_kv, w_ukv, g_conv_out,
                            g_attn_out, w_o, g_mix_post, g_ffn_pre, w_ffn_gate, w_ffn_up,
                            w_ffn_down, g_ffn_post, w_ple_proj, w_ple_gate)
        convn, q, *stack_p, k, vt = _in_proj(xp, tab_p, conv_zero, 0, lw, i, depth, ts_p,
                                             tk=2 * tq, stacked=stack_p)
        attn = _attn_prompt(q, k, vt, lw["g_attn"], tq)
        xp = _out_ffn(xp.reshape(n_p * s_p, D_MODEL), convn.reshape(n_p * s_p, D_CONV),
                      attn.reshape(n_p * s_p, D_ATTN), pp_flat, i, lw, tm_p).reshape(n_p, s_p, D_MODEL)
        convn, q, *stack_d = _in_proj(xd, tab_d, state_conv, i, lw, i, depth, s_d, stacked=stack_d)
        attn = _attn_sample(q, cache_kv_latent, cache_k_rope, stack_d[0], stack_d[1], i, lw)
        xd = _out_ffn(xd.reshape(n_d * s_d, D_MODEL), convn.reshape(n_d * s_d, D_CONV),
                      attn.reshape(n_d * s_d, D_ATTN), pd_flat, i, lw, tm_d).reshape(n_d, s_d, D_MODEL)

    return (xp, xd, *stack_p, *stack_d)
```

```python
import functools

import jax
import jax.numpy as jnp
from jax import lax
from jax.experimental import pallas as pl
from jax.experimental.pallas import tpu as pltpu

D_MODEL = 1024
CHUNK = 64
D_PLE = 256
D_CONV = 512
CONV_W = 3
V_DIM = 64
QK_NOPE = 64
QK_ROPE = 32
QK_DIM = QK_NOPE + QK_ROPE
N_HEADS = 8
D_ATTN = N_HEADS * V_DIM
Q_LORA = 768
KV_LORA = 256
D_FF = 2816
ROPE_THETA = 10000.0
EPS = 1e-6
SCALE = QK_DIM ** -0.5

OFF_C = D_CONV
OFF_X = 2 * D_CONV
OFF_Q = 3 * D_CONV
OFF_KV = OFF_Q + Q_LORA
OFF_KR = OFF_KV + KV_LORA
IN_COLS = OFF_KR + QK_ROPE

LANES = 128
HEAD_PAD = LANES
D_HEADS_PAD = N_HEADS * HEAD_PAD
IN_EXT = OFF_KR + LANES
FF_CHUNK = 256
NEG_BIG = -1e30
Q_PRESCALE = SCALE * 1.4426950408889634
Q_TILES = 2
SCORE_LEAD = 3
ROW_GROUPS = 2
SUM_ROWS = 16

BF16 = jnp.bfloat16
F32 = jnp.float32


def _rms(x, g):
    ms = jnp.mean(x * x, axis=-1, keepdims=True)
    return x * lax.rsqrt(ms + EPS) * g


def _dot(a, b):
    return jnp.dot(a, b, preferred_element_type=F32)


def _dot_nt(a, b):
    return lax.dot_general(a, b, (((1,), (1,)), ((), ())), preferred_element_type=F32)


def _const_spec(shape):
    nd = len(shape)
    return pl.BlockSpec(shape, lambda *_: (0,) * nd, pipeline_mode=pl.Buffered(1))


def _in_proj_kernel(x_ref, tab_ref, cinit_ref, gpre_ref, win_ref, wconv_ref, gq_ref, wq_ref,
                    gkv_ref, gconv_ref, *rest, with_kv, n_alias):
    if with_kv:
        wk_ref, wvt_ref = rest[:2]
        rest = rest[2:]
    rest = rest[n_alias:]
    if with_kv:
        convn_ref, q_ref, lat_ref, kr_ref, newconv_ref, k_ref, vt_ref, carry_ref = rest
    else:
        convn_ref, q_ref, lat_ref, kr_ref, newconv_ref, carry_ref = rest
    ts = x_ref.shape[1]
    s_idx = pl.program_id(1)

    @pl.when(s_idx == 0)
    def _():
        carry_ref[0:2, :] = cinit_ref[0]

    ng = ROW_GROUPS if ts % (ROW_GROUPS * LANES) == 0 else 1
    gs = ts // ng
    groups = [slice(i * gs, (i + 1) * gs) for i in range(ng)]
    z = []
    for r in groups:
        h = _rms(x_ref[0, r, :], gpre_ref[...]).astype(BF16)
        z.append([_dot(h, win_ref[:, a:b]) for a, b in
                  ((0, OFF_C), (OFF_C, OFF_X), (OFF_X, OFF_Q), (OFF_Q, OFF_KV), (OFF_KV, OFF_KR),
                   (OFF_KR, IN_EXT))])
    cat = lambda j: z[0][j] if ng == 1 else jnp.concatenate([zg[j] for zg in z], axis=0)

    gb = cat(0)
    u = cat(1) * cat(2)
    row = lax.broadcasted_iota(jnp.int32, (ts, 1), 0)
    c0 = carry_ref[0:1, :]
    c1 = carry_ref[1:2, :]
    u_m1 = jnp.where(row == 0, c1, pltpu.roll(u, 1, axis=0))
    u_m2 = pltpu.roll(u, 2, axis=0)
    u_m2 = jnp.where(row == 0, c0, jnp.where(row == 1, c1, u_m2))
    conv = u_m2 * wconv_ref[0:1, :] + u_m1 * wconv_ref[1:2, :] + u * wconv_ref[2:3, :]
    convn_ref[0] = _rms(gb * conv, gconv_ref[...]).astype(BF16)
    tail = u[ts - 2:ts, :]
    carry_ref[0:2, :] = tail
    newconv_ref[0] = tail

    for gi, r in enumerate(groups):
        c_q = tab_ref[0, r, :]
        s_q = tab_ref[1, r, :]
        c_k = tab_ref[2, r, :]
        s_k = tab_ref[3, r, :]

        cqn = _rms(z[gi][3], gq_ref[...]).astype(BF16)
        qa = _dot(cqn, wq_ref[...])
        for hd in range(N_HEADS):
            blk = qa[:, hd * HEAD_PAD:(hd + 1) * HEAD_PAD]
            rot = blk * c_q + pltpu.roll(blk, HEAD_PAD - QK_ROPE, axis=1) * s_q
            q_ref[0, r, hd * HEAD_PAD:(hd + 1) * HEAD_PAD] = rot.astype(BF16)

        c_kv = _rms(z[gi][4], gkv_ref[...])
        lat_ref[0, r, :] = c_kv
        ckvb = c_kv.astype(BF16)
        zkr = z[gi][5]
        kr_blk = zkr * c_k + pltpu.roll(zkr, HEAD_PAD - QK_ROPE, axis=1) * s_k
        kr_ref[0, r, :] = kr_blk[:, QK_NOPE:QK_DIM]
        if with_kv:
            kn = _dot(ckvb, wk_ref[...])
            for hd in range(N_HEADS):
                k_ref[0, r, hd * HEAD_PAD:(hd + 1) * HEAD_PAD] = (
                    kn[:, hd * HEAD_PAD:(hd + 1) * HEAD_PAD] + kr_blk).astype(BF16)
            vt = _dot_nt(wvt_ref[...], ckvb).astype(BF16)
            tk = vt_ref.shape[3]
            start = gi * gs
            vt_ref[0, start // tk, :, start % tk:start % tk + gs] = vt


def _in_proj(x, tables, conv_init, conv_layer, lw, layer, depth, ts, tk=None, stacked=None):
    n, s, _ = x.shape
    with_kv = tk is not None
    grid = (n, s // ts)
    tok = lambda w: pl.BlockSpec((1, ts, w), lambda b, i: (b, i, 0))
    lay = lambda rows, w, tiled: pl.BlockSpec(
        (None, 1, rows, w), (lambda b, i: (layer, b, i, 0)) if tiled else (lambda b, i: (layer, b, 0, 0)))
    out_shape = [
        jax.ShapeDtypeStruct((n, s, D_CONV), BF16),
        jax.ShapeDtypeStruct((n, s, D_HEADS_PAD), BF16),
        jax.ShapeDtypeStruct((depth, n, s, KV_LORA), F32),
        jax.ShapeDtypeStruct((depth, n, s, QK_ROPE), F32),
        jax.ShapeDtypeStruct((depth, n, CONV_W - 1, D_CONV), F32),
    ]
    in_specs = [
        tok(D_MODEL),
        pl.BlockSpec((4, ts, LANES), lambda b, i: (0, i, 0)),
        pl.BlockSpec((None, 1, CONV_W - 1, D_CONV), lambda b, i: (conv_layer, b, 0, 0)),
        _const_spec((1, D_MODEL)),
        _const_spec((D_MODEL, IN_EXT)),
        _const_spec((CONV_W, D_CONV)),
        _const_spec((1, Q_LORA)),
        _const_spec((Q_LORA, D_HEADS_PAD)),
        _const_spec((1, KV_LORA)),
        _const_spec((1, D_CONV)),
    ]
    out_specs = [
        tok(D_CONV), tok(D_HEADS_PAD), lay(ts, KV_LORA, True), lay(ts, QK_ROPE, True),
        lay(CONV_W - 1, D_CONV, False),
    ]
    args = [x, tables, conv_init, lw["g_mix_pre"], lw["w_in"], lw["w_conv"], lw["g_q"], lw["w_q"],
            lw["g_kv"], lw["g_conv_out"]]
    if with_kv:
        in_specs += [_const_spec((KV_LORA, D_HEADS_PAD)), _const_spec((D_ATTN, KV_LORA))]
        args += [lw["w_k"], lw["w_vt"]]
        out_shape += [jax.ShapeDtypeStruct((n, s, D_HEADS_PAD), BF16),
                      jax.ShapeDtypeStruct((n, s // tk, D_ATTN, tk), BF16)]
        out_specs += [tok(D_HEADS_PAD),
                      pl.BlockSpec((1, ts // tk, D_ATTN, tk), lambda b, i: (b, i, 0, 0))]
    aliases = {}
    if stacked is not None:
        for j, arr in enumerate(stacked):
            aliases[len(args)] = 2 + j
            in_specs.append(pl.BlockSpec(memory_space=pl.ANY))
            args.append(arr)
    return pl.pallas_call(
        functools.partial(_in_proj_kernel, with_kv=with_kv, n_alias=len(aliases)),
        grid=grid,
        in_specs=in_specs,
        out_specs=tuple(out_specs),
        out_shape=tuple(out_shape),
        input_output_aliases=aliases,
        scratch_shapes=[pltpu.VMEM((8, D_CONV), F32)],
        compiler_params=pltpu.CompilerParams(
            dimension_semantics=("arbitrary", "arbitrary"),
            vmem_limit_bytes=48 * 1024 * 1024),
        name="in_proj_kv" if with_kv else "in_proj",
    )(*args)


def _attn_prompt_kernel(q_ref, k_ref, vt_ref, g_ref, o_ref, acc_ref, m_ref, ot_ref, ahead_ref, *, tq):
    tk = vt_ref.shape[3]
    pair = pl.program_id(1)
    m_ref[...] = jnp.full(m_ref.shape, NEG_BIG, F32)
    acc_ref[...] = jnp.zeros(acc_ref.shape, F32)
    items = [(t, hd) for hd in range(N_HEADS) for t in range(Q_TILES)]

    def scores(t, kb, hd, width):
        cols = slice(hd * HEAD_PAD, (hd + 1) * HEAD_PAD)
        k0 = pl.multiple_of(kb * tk, tk)
        return _dot_nt(k_ref[0, pl.ds(k0, width), cols],
                       q_ref[0, t * tq:(t + 1) * tq, cols])

    def visible(t, width):
        ck = lax.broadcasted_iota(jnp.int32, (width, tq), 0) // CHUNK
        rq = (lax.broadcasted_iota(jnp.int32, (width, tq), 1) + t * tq) // CHUNK
        return ck <= rq

    def softmax(t, hd, st, vis):
        if vis is not None:
            st = jnp.where(vis, st, NEG_BIG)
        m_prev = m_ref[t, hd:hd + 1, :]
        m_new = jnp.maximum(m_prev, jnp.max(st, axis=0, keepdims=True))
        m_ref[t, hd:hd + 1, :] = m_new
        return jnp.exp2(m_prev - m_new), jnp.exp2(st - m_new).astype(BF16)

    def weighted_values(t, hd, kb, alpha, pt):
        width = pt.shape[0]
        lhs = jnp.concatenate([vt_ref[0, kb, hd * V_DIM:(hd + 1) * V_DIM, 0:width],
                               jnp.ones((SUM_ROWS, width), BF16)], axis=0)
        acc_ref[t, hd] = alpha * acc_ref[t, hd] + _dot(lhs, pt)

    for n in range(SCORE_LEAD):
        ahead_ref[n] = scores(items[n][0], 0, items[n][1], tk)

    def block(kb, widths, masked, next_kb):
        sts = {n: ahead_ref[n, 0:widths[items[n][0]], :] for n in range(SCORE_LEAD)}
        vis = [visible(t, widths[t]) if masked else None for t in range(Q_TILES)]
        for n, (t, hd) in enumerate(items):
            lead = n + SCORE_LEAD
            if lead < len(items):
                sts[lead] = scores(items[lead][0], kb, items[lead][1], widths[items[lead][0]])
            elif next_kb is not None:
                nxt = items[lead - len(items)]
                ahead_ref[lead - len(items)] = scores(nxt[0], next_kb, nxt[1], tk)
            weighted_values(t, hd, kb, *softmax(t, hd, sts.pop(n), vis[t]))

    def loop_body(kb, carry):
        block(kb, (tk, tk), False, kb + 1)
        return carry

    lax.fori_loop(0, pair, loop_body, 0)
    block(pair, (tq, tk), True, None)

    for t in range(Q_TILES):
        ssq = jnp.zeros((1, tq), F32)
        for hd in range(N_HEADS):
            a = acc_ref[t, hd]
            o = a[0:V_DIM, :] / a[V_DIM:V_DIM + 1, :]
            ssq = ssq + jnp.sum(o * o, axis=0, keepdims=True)
            ot_ref[t, hd * V_DIM:(hd + 1) * V_DIM, :] = o
        rs = lax.rsqrt(ssq * (1.0 / D_ATTN) + EPS)
        o_ref[0, t * tq:(t + 1) * tq, :] = ((ot_ref[t] * rs).T * g_ref[...]).astype(BF16)


def _attn_prompt(q, k, vt, g_attn, tq):
    n, s, _ = q.shape
    tk = vt.shape[3]
    assert tk == Q_TILES * tq and s % tk == 0
    return pl.pallas_call(
        functools.partial(_attn_prompt_kernel, tq=tq),
        grid=(n, s // tk),
        in_specs=[
            pl.BlockSpec((1, tk, D_HEADS_PAD), lambda b, i: (b, i, 0)),
            pl.BlockSpec((1, s, D_HEADS_PAD), lambda b, i: (b, 0, 0)),
            pl.BlockSpec((1, s // tk, D_ATTN, tk), lambda b, i: (b, 0, 0, 0)),
            _const_spec((1, D_ATTN)),
        ],
        out_specs=pl.BlockSpec((1, tk, D_ATTN), lambda b, i: (b, i, 0)),
        out_shape=jax.ShapeDtypeStruct((n, s, D_ATTN), BF16),
        scratch_shapes=[pltpu.VMEM((Q_TILES, N_HEADS, V_DIM + SUM_ROWS, tq), F32),
                        pltpu.VMEM((Q_TILES, N_HEADS, tq), F32),
                        pltpu.VMEM((Q_TILES, D_ATTN, tq), F32),
                        pltpu.VMEM((SCORE_LEAD, tk, tq), F32)],
        compiler_params=pltpu.CompilerParams(
            dimension_semantics=("arbitrary", "arbitrary"),
            vmem_limit_bytes=48 * 1024 * 1024),
        name="attn_prompt",
    )(q, k, vt, g_attn)


def _attn_sample_kernel(q_ref, cpast_ref, krpast_ref, cnew_ref, krnew_ref, wabs_ref, psel_ref,
                        wuv_ref, g_ref, o_ref):
    l_new = q_ref.shape[1]
    q = q_ref[0]
    ql, qr = [], []
    for hd in range(N_HEADS):
        qh = q[:, hd * HEAD_PAD:(hd + 1) * HEAD_PAD]
        ql.append(_dot(qh, wabs_ref[hd]))
        qr.append(_dot(qh, psel_ref[...]))
    ql = jnp.concatenate(ql, axis=0).astype(BF16)
    qr = jnp.concatenate(qr, axis=0).astype(BF16)
    cp = cpast_ref[0].astype(BF16)
    cn = cnew_ref[0].astype(BF16)
    s_p = _dot_nt(ql, cp) + _dot_nt(qr, krpast_ref[0].astype(BF16))
    s_n = _dot_nt(ql, cn) + _dot_nt(qr, krnew_ref[0].astype(BF16))
    m = jnp.maximum(jnp.max(s_p, axis=-1, keepdims=True), jnp.max(s_n, axis=-1, keepdims=True))
    p_p = jnp.exp2(s_p - m)
    p_n = jnp.exp2(s_n - m)
    l = jnp.sum(p_p, axis=-1, keepdims=True) + jnp.sum(p_n, axis=-1, keepdims=True)
    o_lat = ((_dot(p_p.astype(BF16), cp) + _dot(p_n.astype(BF16), cn)) / l).astype(BF16)
    o = _dot(o_lat[0:l_new, :], wuv_ref[0])
    for hd in range(1, N_HEADS):
        o = o + _dot(o_lat[hd * l_new:(hd + 1) * l_new, :], wuv_ref[hd])
    o_ref[0] = _rms(o, g_ref[...]).astype(BF16)


def _attn_sample(q, c_past, kr_past, c_new, kr_new, layer, lw):
    n, l_new, _ = q.shape
    p_len = c_past.shape[2]
    per_b = lambda rows, w: pl.BlockSpec((1, rows, w), lambda b: (b, 0, 0))
    lay = lambda rows, w: pl.BlockSpec((None, 1, rows, w), lambda b: (layer, b, 0, 0))
    return pl.pallas_call(
        _attn_sample_kernel,
        grid=(n,),
        in_specs=[
            per_b(l_new, D_HEADS_PAD),
            lay(p_len, KV_LORA),
            lay(p_len, QK_ROPE),
            lay(l_new, KV_LORA),
            lay(l_new, QK_ROPE),
            _const_spec((N_HEADS, HEAD_PAD, KV_LORA)),
            _const_spec((HEAD_PAD, QK_ROPE)),
            _const_spec((N_HEADS, KV_LORA, D_ATTN)),
            _const_spec((1, D_ATTN)),
        ],
        out_specs=per_b(l_new, D_ATTN),
        out_shape=jax.ShapeDtypeStruct((n, l_new, D_ATTN), BF16),
        compiler_params=pltpu.CompilerParams(
            dimension_semantics=("arbitrary",),
            vmem_limit_bytes=48 * 1024 * 1024),
        name="attn_sample",
    )(q, c_past, kr_past, c_new, kr_new, lw["w_abs"], lw["p_sel"], lw["w_uv"], lw["g_attn"])


def _out_ffn_kernel(x_ref, convn_ref, attn_ref, p_ref, woc_ref, woa_ref, gpost_ref, gffn_ref,
                    wg_ref, wu_ref, wd_ref, gfpost_ref, wpg_ref, wpp_ref, y_ref, act_ref):
    tm = x_ref.shape[0]
    groups = [slice(i * (tm // ROW_GROUPS), (i + 1) * (tm // ROW_GROUPS)) for i in range(ROW_GROUPS)]
    mix = [_dot(convn_ref[r, :], woc_ref[...]) + _dot(attn_ref[r, :], woa_ref[...]) for r in groups]
    x1, h = [], []
    for r, mx in zip(groups, mix):
        x1.append(x_ref[r, :] + _rms(mx, gpost_ref[...]))
        h.append(_rms(x1[-1], gffn_ref[...]).astype(BF16))
    for r, hr in zip(groups, h):
        for c in range(D_FF // FF_CHUNK):
            cols = slice(c * FF_CHUNK, (c + 1) * FF_CHUNK)
            g = _dot(hr, wg_ref[:, cols])
            u = _dot(hr, wu_ref[:, cols])
            act_ref[r, cols] = (g * jax.nn.sigmoid(g) * u).astype(BF16)
    f = [_dot(act_ref[r, :], wd_ref[...]) for r in groups]
    for r, x1r, fr in zip(groups, x1, f):
        pp = _dot(p_ref[r, :].astype(BF16), wpp_ref[...])
        x2 = x1r + _rms(fr, gfpost_ref[...])
        gate = jax.nn.sigmoid(_dot(x2.astype(BF16), wpg_ref[...]))
        y_ref[r, :] = x2 + gate * pp


def _out_ffn(x, convn, attn, p, layer, lw, tm):
    m = x.shape[0]
    tok = lambda w: pl.BlockSpec((tm, w), lambda i: (i, 0))
    return pl.pallas_call(
        _out_ffn_kernel,
        grid=(m // tm,),
        in_specs=[
            tok(D_MODEL), tok(D_CONV), tok(D_ATTN),
            pl.BlockSpec((None, tm, D_PLE), lambda i: (layer, i, 0)),
            _const_spec((D_CONV, D_MODEL)),
            _const_spec((D_ATTN, D_MODEL)),
            _const_spec((1, D_MODEL)),
            _const_spec((1, D_MODEL)),
            _const_spec((D_MODEL, D_FF)),
            _const_spec((D_MODEL, D_FF)),
            _const_spec((D_FF, D_MODEL)),
            _const_spec((1, D_MODEL)),
            _const_spec((D_MODEL, D_MODEL)),
            _const_spec((D_PLE, D_MODEL)),
        ],
        out_specs=tok(D_MODEL),
        out_shape=jax.ShapeDtypeStruct((m, D_MODEL), F32),
        scratch_shapes=[pltpu.VMEM((tm, D_FF), BF16)],
        compiler_params=pltpu.CompilerParams(
            dimension_semantics=("arbitrary",),
            vmem_limit_bytes=56 * 1024 * 1024),
        name="out_ffn",
    )(x, convn, attn, p, lw["w_o_conv"], lw["w_o_attn"], lw["g_mix_post"], lw["g_ffn_pre"],
      lw["w_gate"], lw["w_up"], lw["w_down"], lw["g_ffn_post"], lw["w_ple_gate"], lw["w_ple_proj"])


def _swap_halves(a):
    half = a.shape[-1] // 2
    return jnp.concatenate([a[..., half:], a[..., :half]], axis=-1)


def _layer_weights(i, g_mix_pre, w_in, w_conv, g_q, w_uq, g_kv, w_ukv, g_conv_out, g_attn_out, w_o,
                   g_mix_post, g_ffn_pre, w_ffn_gate, w_ffn_up, w_ffn_down, g_ffn_post,
                   w_ple_proj, w_ple_gate):
    row = lambda g: g[i].reshape(1, -1).astype(F32)
    w_kr = w_in[i][:, OFF_KR:IN_COLS]
    w_in_ext = jnp.concatenate(
        [w_in[i][:, :OFF_KR], jnp.zeros((D_MODEL, QK_NOPE), F32), w_kr, _swap_halves(w_kr)], axis=1)
    wq3 = w_uq[i].reshape(Q_LORA, N_HEADS, QK_DIM)
    w_q = jnp.concatenate(
        [wq3, _swap_halves(wq3[..., QK_NOPE:])], axis=-1).reshape(Q_LORA, D_HEADS_PAD)
    wkv3 = w_ukv[i].reshape(KV_LORA, N_HEADS, QK_NOPE + V_DIM)
    w_uk = wkv3[..., :QK_NOPE]
    w_uv = wkv3[..., QK_NOPE:]
    zpad = jnp.zeros((KV_LORA, N_HEADS, HEAD_PAD - QK_NOPE), F32)
    w_k = jnp.concatenate([w_uk, zpad], axis=-1).reshape(KV_LORA, D_HEADS_PAD)
    w_vt = w_uv.reshape(KV_LORA, D_ATTN).T
    w_abs = jnp.concatenate([w_uk, zpad], axis=-1).transpose(1, 2, 0)
    head_of_col = jnp.arange(D_ATTN) // V_DIM
    w_uv_heads = jnp.where(head_of_col[None, None, :] == jnp.arange(N_HEADS)[:, None, None],
                           w_uv.reshape(1, KV_LORA, D_ATTN), 0.0)
    p_sel = jnp.zeros((HEAD_PAD, QK_ROPE), F32).at[QK_NOPE:QK_DIM, :].set(jnp.eye(QK_ROPE, dtype=F32))
    return dict(
        g_mix_pre=row(g_mix_pre), w_in=w_in_ext.astype(BF16), w_conv=w_conv[i].astype(F32),
        g_q=row(g_q), w_q=w_q.astype(BF16), g_kv=row(g_kv), w_k=w_k.astype(BF16), w_vt=w_vt.astype(BF16),
        g_conv_out=row(g_conv_out), g_attn=row(g_attn_out),
        w_abs=w_abs.astype(BF16), w_uv=w_uv_heads.astype(BF16), p_sel=p_sel.astype(BF16),
        w_o_conv=w_o[i][:D_CONV].astype(BF16), w_o_attn=w_o[i][D_CONV:].astype(BF16),
        g_mix_post=row(g_mix_post), g_ffn_pre=row(g_ffn_pre),
        w_gate=w_ffn_gate[i].astype(BF16), w_up=w_ffn_up[i].astype(BF16), w_down=w_ffn_down[i].astype(BF16),
        g_ffn_post=row(g_ffn_post), w_ple_gate=w_ple_gate[i].astype(BF16), w_ple_proj=w_ple_proj[i].astype(BF16),
    )


def _rope_tables(pos):
    half = QK_ROPE // 2
    inv = ROPE_THETA ** (-jnp.arange(half, dtype=F32) / half)
    ang = pos.astype(F32)[:, None] * inv[None, :]
    cos, sin = jnp.cos(ang), jnp.sin(ang)
    n = pos.shape[0]
    cc = jnp.concatenate([cos, cos], axis=1)
    ss = jnp.concatenate([-sin, sin], axis=1)
    z_lo = jnp.zeros((n, QK_NOPE), F32)
    z_hi = jnp.zeros((n, HEAD_PAD - QK_DIM), F32)
    c_q = jnp.concatenate([jnp.ones((n, QK_NOPE), F32), cc, z_hi], axis=1) * Q_PRESCALE
    s_q = jnp.concatenate([z_lo, ss, z_hi], axis=1)
    c_k = jnp.concatenate([z_lo, cc, z_hi], axis=1)
    return jnp.stack([c_q, s_q * Q_PRESCALE, c_k, s_q])


def _pick_tile(n, cap):
    t = min(n, cap)
    while n % t:
        t //= 2
    return t


def kernel(x_prompt, x_sample, cache_kv_latent, cache_k_rope, state_conv, p_prompt, p_sample, g_mix_pre, w_in, w_conv, g_q, w_uq, g_kv, w_ukv, g_conv_out, g_attn_out, w_o, g_mix_post, g_ffn_pre, w_ffn_gate, w_ffn_up, w_ffn_down, g_ffn_post, w_ple_proj, w_ple_gate):
    depth = w_in.shape[0]
    n_p, s_p, _ = x_prompt.shape
    n_d, s_d, _ = x_sample.shape
    past_len = cache_kv_latent.shape[2]
    tab_p = _rope_tables(jnp.arange(s_p))
    tab_d = _rope_tables(past_len + jnp.arange(s_d))
    conv_zero = jnp.zeros((1, n_p, CONV_W - 1, D_CONV), F32)
    ts_p = _pick_tile(s_p, 512)
    tq = _pick_tile(s_p, 256)
    tm_p = _pick_tile(n_p * s_p, 512)
    tm_d = _pick_tile(n_d * s_d, 256)
    pp_flat = p_prompt.reshape(depth, n_p * s_p, D_PLE)
    pd_flat = p_sample.reshape(depth, n_d * s_d, D_PLE)

    xp, xd = x_prompt, x_sample
    stack_p = stack_d = None
    for i in range(depth):
        lw = _layer_weights(i, g_mix_pre, w_in, w_conv, g_q, w_uq, g_kv, w_ukv, g_conv_out,
                            g_attn_out, w_o, g_mix_post, g_ffn_pre, w_ffn_gate, w_ffn_up,
                            w_ffn_down, g_ffn_post, w_ple_proj, w_ple_gate)
        convn, q, *stack_p, k, vt = _in_proj(xp, tab_p, conv_zero, 0, lw, i, depth, ts_p,
                                             tk=2 * tq, stacked=stack_p)
        attn = _attn_prompt(q, k, vt, lw["g_attn"], tq)
        xp = _out_ffn(xp.reshape(n_p * s_p, D_MODEL), convn.reshape(n_p * s_p, D_CONV),
                      attn.reshape(n_p * s_p, D_ATTN), pp_flat, i, lw, tm_p).reshape(n_p, s_p, D_MODEL)
        convn, q, *stack_d = _in_proj(xd, tab_d, state_conv, i, lw, i, depth, s_d, stacked=stack_d)
        attn = _attn_sample(q, cache_kv_latent, cache_k_rope, stack_d[0], stack_d[1], i, lw)
        xd = _out_ffn(xd.reshape(n_d * s_d, D_MODEL), convn.reshape(n_d * s_d, D_CONV),
                      attn.reshape(n_d * s_d, D_ATTN), pd_flat, i, lw, tm_d).reshape(n_d, s_d, D_MODEL)

    return (xp, xd, *stack_p, *stack_d)
```

```python
import functools

import jax
import jax.numpy as jnp
from jax import lax
from jax.experimental import pallas as pl
from jax.experimental.pallas import tpu as pltpu

D_MODEL = 1024
CHUNK = 64
D_PLE = 256
D_CONV = 512
CONV_W = 3
V_DIM = 64
QK_NOPE = 64
QK_ROPE = 32
QK_DIM = QK_NOPE + QK_ROPE
N_HEADS = 8
D_ATTN = N_HEADS * V_DIM
Q_LORA = 768
KV_LORA = 256
D_FF = 2816
ROPE_THETA = 10000.0
EPS = 1e-6
SCALE = QK_DIM ** -0.5

OFF_C = D_CONV
OFF_X = 2 * D_CONV
OFF_Q = 3 * D_CONV
OFF_KV = OFF_Q + Q_LORA
OFF_KR = OFF_KV + KV_LORA
IN_COLS = OFF_KR + QK_ROPE

LANES = 128
HEAD_PAD = LANES
D_HEADS_PAD = N_HEADS * HEAD_PAD
IN_EXT = OFF_KR + LANES
FF_CHUNK = 256
NEG_BIG = -1e30
Q_PRESCALE = SCALE * 1.4426950408889634
Q_TILES = 2
SCORE_LEAD = 3
ROW_GROUPS = 2
SUM_ROWS = 16

BF16 = jnp.bfloat16
F32 = jnp.float32


def _rms(x, g):
    ms = jnp.mean(x * x, axis=-1, keepdims=True)
    return x * lax.rsqrt(ms + EPS) * g


def _dot(a, b):
    return jnp.dot(a, b, preferred_element_type=F32)


def _dot_nt(a, b):
    return lax.dot_general(a, b, (((1,), (1,)), ((), ())), preferred_element_type=F32)


def _const_spec(shape):
    nd = len(shape)
    return pl.BlockSpec(shape, lambda *_: (0,) * nd, pipeline_mode=pl.Buffered(1))


def _in_proj_kernel(x_ref, tab_ref, cinit_ref, gpre_ref, win_ref, wconv_ref, gq_ref, wq_ref,
                    gkv_ref, gconv_ref, *rest, with_kv, n_alias):
    if with_kv:
        wk_ref, wvt_ref = rest[:2]
        rest = rest[2:]
    rest = rest[n_alias:]
    if with_kv:
        convn_ref, q_ref, lat_ref, kr_ref, newconv_ref, k_ref, vt_ref, carry_ref = rest
    else:
        convn_ref, q_ref, lat_ref, kr_ref, newconv_ref, carry_ref = rest
    ts = x_ref.shape[1]
    s_idx = pl.program_id(1)

    @pl.when(s_idx == 0)
    def _():
        carry_ref[0:2, :] = cinit_ref[0]

    ng = ROW_GROUPS if ts % (ROW_GROUPS * LANES) == 0 else 1
    gs = ts // ng
    groups = [slice(i * gs, (i + 1) * gs) for i in range(ng)]
    z = []
    for r in groups:
        h = _rms(x_ref[0, r, :], gpre_ref[...]).astype(BF16)
        z.append([_dot(h, win_ref[:, a:b]) for a, b in
                  ((0, OFF_C), (OFF_C, OFF_X), (OFF_X, OFF_Q), (OFF_Q, OFF_KV), (OFF_KV, OFF_KR),
                   (OFF_KR, IN_EXT))])
    cat = lambda j: z[0][j] if ng == 1 else jnp.concatenate([zg[j] for zg in z], axis=0)

    gb = cat(0)
    u = cat(1) * cat(2)
    row = lax.broadcasted_iota(jnp.int32, (ts, 1), 0)
    c0 = carry_ref[0:1, :]
    c1 = carry_ref[1:2, :]
    u_m1 = jnp.where(row == 0, c1, pltpu.roll(u, 1, axis=0))
    u_m2 = pltpu.roll(u, 2, axis=0)
    u_m2 = jnp.where(row == 0, c0, jnp.where(row == 1, c1, u_m2))
    conv = u_m2 * wconv_ref[0:1, :] + u_m1 * wconv_ref[1:2, :] + u * wconv_ref[2:3, :]
    convn_ref[0] = _rms(gb * conv, gconv_ref[...]).astype(BF16)
    tail = u[ts - 2:ts, :]
    carry_ref[0:2, :] = tail
    newconv_ref[0] = tail

    for gi, r in enumerate(groups):
        c_q = tab_ref[0, r, :]
        s_q = tab_ref[1, r, :]
        c_k = tab_ref[2, r, :]
        s_k = tab_ref[3, r, :]

        cqn = _rms(z[gi][3], gq_ref[...]).astype(BF16)
        qa = _dot(cqn, wq_ref[...])
        for hd in range(N_HEADS):
            blk = qa[:, hd * HEAD_PAD:(hd + 1) * HEAD_PAD]
            rot = blk * c_q + pltpu.roll(blk, HEAD_PAD - QK_ROPE, axis=1) * s_q
            q_ref[0, r, hd * HEAD_PAD:(hd + 1) * HEAD_PAD] = rot.astype(BF16)

        c_kv = _rms(z[gi][4], gkv_ref[...])
        lat_ref[0, r, :] = c_kv
        ckvb = c_kv.astype(BF16)
        zkr = z[gi][5]
        kr_blk = zkr * c_k + pltpu.roll(zkr, HEAD_PAD - QK_ROPE, axis=1) * s_k
        kr_ref[0, r, :] = kr_blk[:, QK_NOPE:QK_DIM]
        if with_kv:
            kn = _dot(ckvb, wk_ref[...])
            for hd in range(N_HEADS):
                k_ref[0, r, hd * HEAD_PAD:(hd + 1) * HEAD_PAD] = (
                    kn[:, hd * HEAD_PAD:(hd + 1) * HEAD_PAD] + kr_blk).astype(BF16)
            vt = _dot_nt(wvt_ref[...], ckvb).astype(BF16)
            tk = vt_ref.shape[3]
            start = gi * gs
            vt_ref[0, start // tk, :, start % tk:start % tk + gs] = vt


def _in_proj(x, tables, conv_init, conv_layer, lw, layer, depth, ts, tk=None, stacked=None):
    n, s, _ = x.shape
    with_kv = tk is not None
    grid = (n, s // ts)
    tok = lambda w: pl.BlockSpec((1, ts, w), lambda b, i: (b, i, 0))
    lay = lambda rows, w, tiled: pl.BlockSpec(
        (None, 1, rows, w), (lambda b, i: (layer, b, i, 0)) if tiled else (lambda b, i: (layer, b, 0, 0)))
    out_shape = [
        jax.ShapeDtypeStruct((n, s, D_CONV), BF16),
        jax.ShapeDtypeStruct((n, s, D_HEADS_PAD), BF16),
        jax.ShapeDtypeStruct((depth, n, s, KV_LORA), F32),
        jax.ShapeDtypeStruct((depth, n, s, QK_ROPE), F32),
        jax.ShapeDtypeStruct((depth, n, CONV_W - 1, D_CONV), F32),
    ]
    in_specs = [
        tok(D_MODEL),
        pl.BlockSpec((4, ts, LANES), lambda b, i: (0, i, 0)),
        pl.BlockSpec((None, 1, CONV_W - 1, D_CONV), lambda b, i: (conv_layer, b, 0, 0)),
        _const_spec((1, D_MODEL)),
        _const_spec((D_MODEL, IN_EXT)),
        _const_spec((CONV_W, D_CONV)),
        _const_spec((1, Q_LORA)),
        _const_spec((Q_LORA, D_HEADS_PAD)),
        _const_spec((1, KV_LORA)),
        _const_spec((1, D_CONV)),
    ]
    out_specs = [
        tok(D_CONV), tok(D_HEADS_PAD), lay(ts, KV_LORA, True), lay(ts, QK_ROPE, True),
        lay(CONV_W - 1, D_CONV, False),
    ]
    args = [x, tables, conv_init, lw["g_mix_pre"], lw["w_in"], lw["w_conv"], lw["g_q"], lw["w_q"],
            lw["g_kv"], lw["g_conv_out"]]
    if with_kv:
        in_specs += [_const_spec((KV_LORA, D_HEADS_PAD)), _const_spec((D_ATTN, KV_LORA))]
        args += [lw["w_k"], lw["w_vt"]]
        out_shape += [jax.ShapeDtypeStruct((n, s, D_HEADS_PAD), BF16),
                      jax.ShapeDtypeStruct((n, s // tk, D_ATTN, tk), BF16)]
        out_specs += [tok(D_HEADS_PAD),
                      pl.BlockSpec((1, ts // tk, D_ATTN, tk), lambda b, i: (b, i, 0, 0))]
    aliases = {}
    if stacked is not None:
        for j, arr in enumerate(stacked):
            aliases[len(args)] = 2 + j
            in_specs.append(pl.BlockSpec(memory_space=pl.ANY))
            args.append(arr)
    return pl.pallas_call(
        functools.partial(_in_proj_kernel, with_kv=with_kv, n_alias=len(aliases)),
        grid=grid,
        in_specs=in_specs,
        out_specs=tuple(out_specs),
        out_shape=tuple(out_shape),
        input_output_aliases=aliases,
        scratch_shapes=[pltpu.VMEM((8, D_CONV), F32)],
        compiler_params=pltpu.CompilerParams(
            dimension_semantics=("arbitrary", "arbitrary"),
            vmem_limit_bytes=48 * 1024 * 1024),
        name="in_proj_kv" if with_kv else "in_proj",
    )(*args)


def _attn_prompt_kernel(qa_ref, qb_ref, k_ref, vt_ref, g_ref, o_ref, acc_ref, m_ref, ot_ref, ahead_ref,
                        *, tq, n_pairs):
    tk = vt_ref.shape[3]
    step = pl.program_id(1)
    pairs = (step, n_pairs - 1 - step)
    q_refs = (qa_ref, qb_ref)
    m_ref[...] = jnp.full(m_ref.shape, NEG_BIG, F32)
    acc_ref[...] = jnp.zeros(acc_ref.shape, F32)
    items = [(t, hd) for hd in range(N_HEADS) for t in range(Q_TILES)]

    def scores(slot, t, kb, hd, width):
        cols = slice(hd * HEAD_PAD, (hd + 1) * HEAD_PAD)
        k0 = pl.multiple_of(kb * tk, tk)
        return _dot_nt(k_ref[0, pl.ds(k0, width), cols],
                       q_refs[slot][0, t * tq:(t + 1) * tq, cols])

    def visible(t, width):
        ck = lax.broadcasted_iota(jnp.int32, (width, tq), 0) // CHUNK
        rq = (lax.broadcasted_iota(jnp.int32, (width, tq), 1) + t * tq) // CHUNK
        return ck <= rq

    def softmax(slot, t, hd, st, vis):
        if vis is not None:
            st = jnp.where(vis, st, NEG_BIG)
        m_prev = m_ref[slot, t, hd:hd + 1, :]
        m_new = jnp.maximum(m_prev, jnp.max(st, axis=0, keepdims=True))
        m_ref[slot, t, hd:hd + 1, :] = m_new
        return jnp.exp2(m_prev - m_new), jnp.exp2(st - m_new).astype(BF16)

    def weighted_values(slot, t, hd, kb, alpha, pt):
        width = pt.shape[0]
        lhs = jnp.concatenate([vt_ref[0, kb, hd * V_DIM:(hd + 1) * V_DIM, 0:width],
                               jnp.ones((SUM_ROWS, width), BF16)], axis=0)
        acc_ref[slot, t, hd] = alpha * acc_ref[slot, t, hd] + _dot(lhs, pt)

    def block(slot, kb, widths, masked, nxt):
        sts = {n: ahead_ref[slot, n, 0:widths[items[n][0]], :] for n in range(SCORE_LEAD)}
        vis = [visible(t, widths[t]) if masked else None for t in range(Q_TILES)]
        for n, (t, hd) in enumerate(items):
            lead = n + SCORE_LEAD
            if lead < len(items):
                sts[lead] = scores(slot, items[lead][0], kb, items[lead][1], widths[items[lead][0]])
            elif nxt is not None:
                tn, hn = items[lead - len(items)]
                ahead_ref[nxt[0], lead - len(items)] = scores(nxt[0], tn, nxt[1], hn, tk)
            weighted_values(slot, t, hd, kb, *softmax(slot, t, hd, sts.pop(n), vis[t]))

    def write_out(slot):
        for t in range(Q_TILES):
            ssq = jnp.zeros((1, tq), F32)
            for hd in range(N_HEADS):
                a = acc_ref[slot, t, hd]
                o = a[0:V_DIM, :] / a[V_DIM:V_DIM + 1, :]
                ssq = ssq + jnp.sum(o * o, axis=0, keepdims=True)
                ot_ref[slot, t, hd * V_DIM:(hd + 1) * V_DIM, :] = o
            rs = lax.rsqrt(ssq * (1.0 / D_ATTN) + EPS)
            o_ref[0, slot, 0, t * tq:(t + 1) * tq, :] = (
                (ot_ref[slot, t] * rs).T * g_ref[...]).astype(BF16)

    for n in range(SCORE_LEAD):
        ahead_ref[0, n] = scores(0, items[n][0], 0, items[n][1], tk)
    for slot in range(2):
        lax.fori_loop(
            0, pairs[slot],
            lambda kb, c, slot=slot: (block(slot, kb, (tk, tk), False, (slot, kb + 1)), c)[1], 0)
        if slot == 1:
            write_out(0)
        block(slot, pairs[slot], (tq, tk), True, (1, 0) if slot == 0 else None)
    write_out(1)


def _attn_prompt(q, k, vt, g_attn, tq):
    n, s, _ = q.shape
    tk = vt.shape[3]
    n_pairs = s // tk
    assert tk == Q_TILES * tq and s % tk == 0 and n_pairs % 2 == 0
    return pl.pallas_call(
        functools.partial(_attn_prompt_kernel, tq=tq, n_pairs=n_pairs),
        grid=(n, n_pairs // 2),
        in_specs=[
            pl.BlockSpec((1, tk, D_HEADS_PAD), lambda b, i: (b, i, 0)),
            pl.BlockSpec((1, tk, D_HEADS_PAD), lambda b, i: (b, n_pairs - 1 - i, 0)),
            pl.BlockSpec((1, s, D_HEADS_PAD), lambda b, i: (b, 0, 0)),
            pl.BlockSpec((1, n_pairs, D_ATTN, tk), lambda b, i: (b, 0, 0, 0)),
            _const_spec((1, D_ATTN)),
        ],
        out_specs=pl.BlockSpec((1, 2, 1, tk, D_ATTN), lambda b, i: (b, 0, i, 0, 0)),
        out_shape=jax.ShapeDtypeStruct((n, 2, n_pairs // 2, tk, D_ATTN), BF16),
        scratch_shapes=[pltpu.VMEM((2, Q_TILES, N_HEADS, V_DIM + SUM_ROWS, tq), F32),
                        pltpu.VMEM((2, Q_TILES, N_HEADS, tq), F32),
                        pltpu.VMEM((2, Q_TILES, D_ATTN, tq), F32),
                        pltpu.VMEM((2, SCORE_LEAD, tk, tq), F32)],
        compiler_params=pltpu.CompilerParams(
            dimension_semantics=("arbitrary", "arbitrary"),
            vmem_limit_bytes=48 * 1024 * 1024),
        name="attn_prompt",
    )(q, q, k, vt, g_attn)


def _attn_sample_kernel(q_ref, cpast_ref, krpast_ref, cnew_ref, krnew_ref, wabs_ref, psel_ref,
                        wuv_ref, g_ref, o_ref):
    l_new = q_ref.shape[1]
    q = q_ref[0]
    ql, qr = [], []
    for hd in range(N_HEADS):
        qh = q[:, hd * HEAD_PAD:(hd + 1) * HEAD_PAD]
        ql.append(_dot(qh, wabs_ref[hd]))
        qr.append(_dot(qh, psel_ref[...]))
    ql = jnp.concatenate(ql, axis=0).astype(BF16)
    qr = jnp.concatenate(qr, axis=0).astype(BF16)
    cp = cpast_ref[0].astype(BF16)
    cn = cnew_ref[0].astype(BF16)
    s_p = _dot_nt(ql, cp) + _dot_nt(qr, krpast_ref[0].astype(BF16))
    s_n = _dot_nt(ql, cn) + _dot_nt(qr, krnew_ref[0].astype(BF16))
    m = jnp.maximum(jnp.max(s_p, axis=-1, keepdims=True), jnp.max(s_n, axis=-1, keepdims=True))
    p_p = jnp.exp2(s_p - m)
    p_n = jnp.exp2(s_n - m)
    l = jnp.sum(p_p, axis=-1, keepdims=True) + jnp.sum(p_n, axis=-1, keepdims=True)
    o_lat = ((_dot(p_p.astype(BF16), cp) + _dot(p_n.astype(BF16), cn)) / l).astype(BF16)
    o = _dot(o_lat[0:l_new, :], wuv_ref[0])
    for hd in range(1, N_HEADS):
        o = o + _dot(o_lat[hd * l_new:(hd + 1) * l_new, :], wuv_ref[hd])
    o_ref[0] = _rms(o, g_ref[...]).astype(BF16)


def _attn_sample(q, c_past, kr_past, c_new, kr_new, layer, lw):
    n, l_new, _ = q.shape
    p_len = c_past.shape[2]
    per_b = lambda rows, w: pl.BlockSpec((1, rows, w), lambda b: (b, 0, 0))
    lay = lambda rows, w: pl.BlockSpec((None, 1, rows, w), lambda b: (layer, b, 0, 0))
    return pl.pallas_call(
        _attn_sample_kernel,
        grid=(n,),
        in_specs=[
            per_b(l_new, D_HEADS_PAD),
            lay(p_len, KV_LORA),
            lay(p_len, QK_ROPE),
            lay(l_new, KV_LORA),
            lay(l_new, QK_ROPE),
            _const_spec((N_HEADS, HEAD_PAD, KV_LORA)),
            _const_spec((HEAD_PAD, QK_ROPE)),
            _const_spec((N_HEADS, KV_LORA, D_ATTN)),
            _const_spec((1, D_ATTN)),
        ],
        out_specs=per_b(l_new, D_ATTN),
        out_shape=jax.ShapeDtypeStruct((n, l_new, D_ATTN), BF16),
        compiler_params=pltpu.CompilerParams(
            dimension_semantics=("arbitrary",),
            vmem_limit_bytes=48 * 1024 * 1024),
        name="attn_sample",
    )(q, c_past, kr_past, c_new, kr_new, lw["w_abs"], lw["p_sel"], lw["w_uv"], lw["g_attn"])


def _out_ffn_kernel(x_ref, convn_ref, attn_ref, p_ref, woc_ref, woa_ref, gpost_ref, gffn_ref,
                    wg_ref, wu_ref, wd_ref, gfpost_ref, wpg_ref, wpp_ref, y_ref, act_ref):
    tm = x_ref.shape[0]
    groups = [slice(i * (tm // ROW_GROUPS), (i + 1) * (tm // ROW_GROUPS)) for i in range(ROW_GROUPS)]
    mix = [_dot(convn_ref[r, :], woc_ref[...]) + _dot(attn_ref[r, :], woa_ref[...]) for r in groups]
    x1, h = [], []
    for r, mx in zip(groups, mix):
        x1.append(x_ref[r, :] + _rms(mx, gpost_ref[...]))
        h.append(_rms(x1[-1], gffn_ref[...]).astype(BF16))
    for r, hr in zip(groups, h):
        for c in range(D_FF // FF_CHUNK):
            cols = slice(c * FF_CHUNK, (c + 1) * FF_CHUNK)
            g = _dot(hr, wg_ref[:, cols])
            u = _dot(hr, wu_ref[:, cols])
            act_ref[r, cols] = (g * jax.nn.sigmoid(g) * u).astype(BF16)
    f = [_dot(act_ref[r, :], wd_ref[...]) for r in groups]
    for r, x1r, fr in zip(groups, x1, f):
        pp = _dot(p_ref[r, :].astype(BF16), wpp_ref[...])
        x2 = x1r + _rms(fr, gfpost_ref[...])
        gate = jax.nn.sigmoid(_dot(x2.astype(BF16), wpg_ref[...]))
        y_ref[r, :] = x2 + gate * pp


def _out_ffn(x, convn, attn, attn_index, p, layer, lw, tm):
    m = x.shape[0]
    tok = lambda w: pl.BlockSpec((tm, w), lambda i: (i, 0))
    return pl.pallas_call(
        _out_ffn_kernel,
        grid=(m // tm,),
        in_specs=[
            tok(D_MODEL), tok(D_CONV),
            pl.BlockSpec((None, None, None, tm, D_ATTN), lambda i: (*attn_index(i), 0)),
            pl.BlockSpec((None, tm, D_PLE), lambda i: (layer, i, 0)),
            _const_spec((D_CONV, D_MODEL)),
            _const_spec((D_ATTN, D_MODEL)),
            _const_spec((1, D_MODEL)),
            _const_spec((1, D_MODEL)),
            _const_spec((D_MODEL, D_FF)),
            _const_spec((D_MODEL, D_FF)),
            _const_spec((D_FF, D_MODEL)),
            _const_spec((1, D_MODEL)),
            _const_spec((D_MODEL, D_MODEL)),
            _const_spec((D_PLE, D_MODEL)),
        ],
        out_specs=tok(D_MODEL),
        out_shape=jax.ShapeDtypeStruct((m, D_MODEL), F32),
        scratch_shapes=[pltpu.VMEM((tm, D_FF), BF16)],
        compiler_params=pltpu.CompilerParams(
            dimension_semantics=("arbitrary",),
            vmem_limit_bytes=56 * 1024 * 1024),
        name="out_ffn",
    )(x, convn, attn, p, lw["w_o_conv"], lw["w_o_attn"], lw["g_mix_post"], lw["g_ffn_pre"],
      lw["w_gate"], lw["w_up"], lw["w_down"], lw["g_ffn_post"], lw["w_ple_gate"], lw["w_ple_proj"])


def _swap_halves(a):
    half = a.shape[-1] // 2
    return jnp.concatenate([a[..., half:], a[..., :half]], axis=-1)


def _layer_weights(i, g_mix_pre, w_in, w_conv, g_q, w_uq, g_kv, w_ukv, g_conv_out, g_attn_out, w_o,
                   g_mix_post, g_ffn_pre, w_ffn_gate, w_ffn_up, w_ffn_down, g_ffn_post,
                   w_ple_proj, w_ple_gate):
    row = lambda g: g[i].reshape(1, -1).astype(F32)
    w_kr = w_in[i][:, OFF_KR:IN_COLS]
    w_in_ext = jnp.concatenate(
        [w_in[i][:, :OFF_KR], jnp.zeros((D_MODEL, QK_NOPE), F32), w_kr, _swap_halves(w_kr)], axis=1)
    wq3 = w_uq[i].reshape(Q_LORA, N_HEADS, QK_DIM)
    w_q = jnp.concatenate(
        [wq3, _swap_halves(wq3[..., QK_NOPE:])], axis=-1).reshape(Q_LORA, D_HEADS_PAD)
    wkv3 = w_ukv[i].reshape(KV_LORA, N_HEADS, QK_NOPE + V_DIM)
    w_uk = wkv3[..., :QK_NOPE]
    w_uv = wkv3[..., QK_NOPE:]
    zpad = jnp.zeros((KV_LORA, N_HEADS, HEAD_PAD - QK_NOPE), F32)
    w_k = jnp.concatenate([w_uk, zpad], axis=-1).reshape(KV_LORA, D_HEADS_PAD)
    w_vt = w_uv.reshape(KV_LORA, D_ATTN).T
    w_abs = jnp.concatenate([w_uk, zpad], axis=-1).transpose(1, 2, 0)
    head_of_col = jnp.arange(D_ATTN) // V_DIM
    w_uv_heads = jnp.where(head_of_col[None, None, :] == jnp.arange(N_HEADS)[:, None, None],
                           w_uv.reshape(1, KV_LORA, D_ATTN), 0.0)
    p_sel = jnp.zeros((HEAD_PAD, QK_ROPE), F32).at[QK_NOPE:QK_DIM, :].set(jnp.eye(QK_ROPE, dtype=F32))
    return dict(
        g_mix_pre=row(g_mix_pre), w_in=w_in_ext.astype(BF16), w_conv=w_conv[i].astype(F32),
        g_q=row(g_q), w_q=w_q.astype(BF16), g_kv=row(g_kv), w_k=w_k.astype(BF16), w_vt=w_vt.astype(BF16),
        g_conv_out=row(g_conv_out), g_attn=row(g_attn_out),
        w_abs=w_abs.astype(BF16), w_uv=w_uv_heads.astype(BF16), p_sel=p_sel.astype(BF16),
        w_o_conv=w_o[i][:D_CONV].astype(BF16), w_o_attn=w_o[i][D_CONV:].astype(BF16),
        g_mix_post=row(g_mix_post), g_ffn_pre=row(g_ffn_pre),
        w_gate=w_ffn_gate[i].astype(BF16), w_up=w_ffn_up[i].astype(BF16), w_down=w_ffn_down[i].astype(BF16),
        g_ffn_post=row(g_ffn_post), w_ple_gate=w_ple_gate[i].astype(BF16), w_ple_proj=w_ple_proj[i].astype(BF16),
    )


def _rope_tables(pos):
    half = QK_ROPE // 2
    inv = ROPE_THETA ** (-jnp.arange(half, dtype=F32) / half)
    ang = pos.astype(F32)[:, None] * inv[None, :]
    cos, sin = jnp.cos(ang), jnp.sin(ang)
    n = pos.shape[0]
    cc = jnp.concatenate([cos, cos], axis=1)
    ss = jnp.concatenate([-sin, sin], axis=1)
    z_lo = jnp.zeros((n, QK_NOPE), F32)
    z_hi = jnp.zeros((n, HEAD_PAD - QK_DIM), F32)
    c_q = jnp.concatenate([jnp.ones((n, QK_NOPE), F32), cc, z_hi], axis=1) * Q_PRESCALE
    s_q = jnp.concatenate([z_lo, ss, z_hi], axis=1)
    c_k = jnp.concatenate([z_lo, cc, z_hi], axis=1)
    return jnp.stack([c_q, s_q * Q_PRESCALE, c_k, s_q])


def _pick_tile(n, cap):
    t = min(n, cap)
    while n % t:
        t //= 2
    return t


def kernel(x_prompt, x_sample, cache_kv_latent, cache_k_rope, state_conv, p_prompt, p_sample, g_mix_pre, w_in, w_conv, g_q, w_uq, g_kv, w_ukv, g_conv_out, g_attn_out, w_o, g_mix_post, g_ffn_pre, w_ffn_gate, w_ffn_up, w_ffn_down, g_ffn_post, w_ple_proj, w_ple_gate):
    depth = w_in.shape[0]
    n_p, s_p, _ = x_prompt.shape
    n_d, s_d, _ = x_sample.shape
    past_len = cache_kv_latent.shape[2]
    tab_p = _rope_tables(jnp.arange(s_p))
    tab_d = _rope_tables(past_len + jnp.arange(s_d))
    conv_zero = jnp.zeros((1, n_p, CONV_W - 1, D_CONV), F32)
    ts_p = _pick_tile(s_p, 512)
    tq = _pick_tile(s_p, 256)
    tm_p = _pick_tile(n_p * s_p, 512)
    tm_d = _pick_tile(n_d * s_d, 256)
    n_pairs = s_p // (Q_TILES * tq)
    assert tm_p == Q_TILES * tq

    def attn_tile_index(t):
        b, p = t // n_pairs, t % n_pairs
        second = p // (n_pairs // 2)
        return b, second, jnp.where(second == 1, n_pairs - 1 - p, p), 0

    pp_flat = p_prompt.reshape(depth, n_p * s_p, D_PLE)
    pd_flat = p_sample.reshape(depth, n_d * s_d, D_PLE)

    xp, xd = x_prompt, x_sample
    stack_p = stack_d = None
    for i in range(depth):
        lw = _layer_weights(i, g_mix_pre, w_in, w_conv, g_q, w_uq, g_kv, w_ukv, g_conv_out,
                            g_attn_out, w_o, g_mix_post, g_ffn_pre, w_ffn_gate, w_ffn_up,
                            w_ffn_down, g_ffn_post, w_ple_proj, w_ple_gate)
        convn, q, *stack_p, k, vt = _in_proj(xp, tab_p, conv_zero, 0, lw, i, depth, ts_p,
                                             tk=2 * tq, stacked=stack_p)
        attn = _attn_prompt(q, k, vt, lw["g_attn"], tq)
        xp = _out_ffn(xp.reshape(n_p * s_p, D_MODEL), convn.reshape(n_p * s_p, D_CONV),
                      attn, attn_tile_index, pp_flat, i, lw, tm_p).reshape(n_p, s_p, D_MODEL)
        convn, q, *stack_d = _in_proj(xd, tab_d, state_conv, i, lw, i, depth, s_d, stacked=stack_d)
        attn = _attn_sample(q, cache_kv_latent, cache_k_rope, stack_d[0], stack_d[1], i, lw)
        xd = _out_ffn(xd.reshape(n_d * s_d, D_MODEL), convn.reshape(n_d * s_d, D_CONV),
                      attn.reshape(1, 1, 1, n_d * s_d, D_ATTN), lambda t: (0, 0, 0, t), pd_flat, i, lw,
                      tm_d).reshape(n_d, s_d, D_MODEL)

    return (xp, xd, *stack_p, *stack_d)
```

```python
import functools

import jax
import jax.numpy as jnp
from jax import lax
from jax.experimental import pallas as pl
from jax.experimental.pallas import tpu as pltpu

D_MODEL = 1024
CHUNK = 64
D_PLE = 256
D_CONV = 512
CONV_W = 3
V_DIM = 64
QK_NOPE = 64
QK_ROPE = 32
QK_DIM = QK_NOPE + QK_ROPE
N_HEADS = 8
D_ATTN = N_HEADS * V_DIM
Q_LORA = 768
KV_LORA = 256
D_FF = 2816
ROPE_THETA = 10000.0
EPS = 1e-6
SCALE = QK_DIM ** -0.5

OFF_C = D_CONV
OFF_X = 2 * D_CONV
OFF_Q = 3 * D_CONV
OFF_KV = OFF_Q + Q_LORA
OFF_KR = OFF_KV + KV_LORA
IN_COLS = OFF_KR + QK_ROPE

LANES = 128
HEAD_PAD = LANES
D_HEADS_PAD = N_HEADS * HEAD_PAD
IN_EXT = OFF_KR + LANES
FF_CHUNK = 256
NEG_BIG = -1e30
Q_PRESCALE = SCALE * 1.4426950408889634
Q_TILES = 2
SCORE_LEAD = 3
ROW_GROUPS = 2
SUM_ROWS = 16

BF16 = jnp.bfloat16
F32 = jnp.float32


def _rms(x, g):
    ms = jnp.mean(x * x, axis=-1, keepdims=True)
    return x * lax.rsqrt(ms + EPS) * g


def _dot(a, b):
    return jnp.dot(a, b, preferred_element_type=F32)


def _dot_nt(a, b):
    return lax.dot_general(a, b, (((1,), (1,)), ((), ())), preferred_element_type=F32)


def _const_spec(shape):
    nd = len(shape)
    return pl.BlockSpec(shape, lambda *_: (0,) * nd, pipeline_mode=pl.Buffered(1))


def _in_proj_kernel(x_ref, tab_ref, cinit_ref, gpre_ref, win_ref, wconv_ref, gq_ref, wq_ref,
                    gkv_ref, gconv_ref, *rest, with_kv, n_alias):
    if with_kv:
        wk_ref, wvt_ref = rest[:2]
        rest = rest[2:]
    rest = rest[n_alias:]
    if with_kv:
        convn_ref, q_ref, lat_ref, kr_ref, newconv_ref, k_ref, vt_ref, carry_ref = rest
    else:
        convn_ref, q_ref, lat_ref, kr_ref, newconv_ref, carry_ref = rest
    ts = x_ref.shape[1]
    s_idx = pl.program_id(1)

    @pl.when(s_idx == 0)
    def _():
        carry_ref[0:2, :] = cinit_ref[0]

    ng = ROW_GROUPS if ts % (ROW_GROUPS * LANES) == 0 else 1
    gs = ts // ng
    groups = [slice(i * gs, (i + 1) * gs) for i in range(ng)]
    z = []
    col_groups = ((0, OFF_C), (OFF_C, OFF_X), (OFF_X, OFF_Q), (OFF_Q, OFF_KV), (OFF_KV, OFF_KR),
                  (OFF_KR, IN_EXT))
    for r in groups:
        h = _rms(x_ref[0, r, :], gpre_ref[...]).astype(BF16)
        z.append([_dot(h, win_ref[:, a:b]) for a, b in col_groups])
    cat = lambda j: z[0][j] if ng == 1 else jnp.concatenate([zg[j] for zg in z], axis=0)

    gb = cat(0)
    u = cat(1) * cat(2)
    row = lax.broadcasted_iota(jnp.int32, (ts, 1), 0)
    c0 = carry_ref[0:1, :]
    c1 = carry_ref[1:2, :]
    u_m1 = jnp.where(row == 0, c1, pltpu.roll(u, 1, axis=0))
    u_m2 = pltpu.roll(u, 2, axis=0)
    u_m2 = jnp.where(row == 0, c0, jnp.where(row == 1, c1, u_m2))
    conv = u_m2 * wconv_ref[0:1, :] + u_m1 * wconv_ref[1:2, :] + u * wconv_ref[2:3, :]
    convn_ref[0] = _rms(gb * conv, gconv_ref[...]).astype(BF16)
    tail = u[ts - 2:ts, :]
    carry_ref[0:2, :] = tail
    newconv_ref[0] = tail

    for gi, r in enumerate(groups):
        c_q = tab_ref[0, r, :]
        s_q = tab_ref[1, r, :]
        c_k = tab_ref[2, r, :]
        s_k = tab_ref[3, r, :]

        cqn = _rms(z[gi][3], gq_ref[...]).astype(BF16)
        qa = _dot(cqn, wq_ref[...])
        for hd in range(N_HEADS):
            blk = qa[:, hd * HEAD_PAD:(hd + 1) * HEAD_PAD]
            rot = blk * c_q + pltpu.roll(blk, HEAD_PAD - QK_ROPE, axis=1) * s_q
            q_ref[0, r, hd * HEAD_PAD:(hd + 1) * HEAD_PAD] = rot.astype(BF16)

        c_kv = _rms(z[gi][4], gkv_ref[...])
        lat_ref[0, r, :] = c_kv
        ckvb = c_kv.astype(BF16)
        zkr = z[gi][5]
        kr_blk = zkr * c_k + pltpu.roll(zkr, HEAD_PAD - QK_ROPE, axis=1) * s_k
        kr_ref[0, r, :] = kr_blk[:, QK_NOPE:QK_DIM]
        if with_kv:
            kn = _dot(ckvb, wk_ref[...])
            for hd in range(N_HEADS):
                k_ref[0, r, hd * HEAD_PAD:(hd + 1) * HEAD_PAD] = (
                    kn[:, hd * HEAD_PAD:(hd + 1) * HEAD_PAD] + kr_blk).astype(BF16)
            vt_ref[0, :, r] = _dot_nt(wvt_ref[...], ckvb).astype(BF16)


def _in_proj(x, tables, conv_init, conv_layer, lw, layer, depth, ts, with_kv, stacked=None):
    n, s, _ = x.shape
    grid = (n, s // ts)
    tok = lambda w: pl.BlockSpec((1, ts, w), lambda b, i: (b, i, 0))
    lay = lambda rows, w, tiled: pl.BlockSpec(
        (None, 1, rows, w), (lambda b, i: (layer, b, i, 0)) if tiled else (lambda b, i: (layer, b, 0, 0)))
    out_shape = [
        jax.ShapeDtypeStruct((n, s, D_CONV), BF16),
        jax.ShapeDtypeStruct((n, s, D_HEADS_PAD), BF16),
        jax.ShapeDtypeStruct((depth, n, s, KV_LORA), F32),
        jax.ShapeDtypeStruct((depth, n, s, QK_ROPE), F32),
        jax.ShapeDtypeStruct((depth, n, CONV_W - 1, D_CONV), F32),
    ]
    in_specs = [
        tok(D_MODEL),
        pl.BlockSpec((4, ts, LANES), lambda b, i: (0, i, 0)),
        pl.BlockSpec((None, 1, CONV_W - 1, D_CONV), lambda b, i: (conv_layer, b, 0, 0)),
        _const_spec((1, D_MODEL)),
        _const_spec((D_MODEL, IN_EXT)),
        _const_spec((CONV_W, D_CONV)),
        _const_spec((1, Q_LORA)),
        _const_spec((Q_LORA, D_HEADS_PAD)),
        _const_spec((1, KV_LORA)),
        _const_spec((1, D_CONV)),
    ]
    out_specs = [
        tok(D_CONV), tok(D_HEADS_PAD), lay(ts, KV_LORA, True), lay(ts, QK_ROPE, True),
        lay(CONV_W - 1, D_CONV, False),
    ]
    args = [x, tables, conv_init, lw["g_mix_pre"], lw["w_in"], lw["w_conv"], lw["g_q"], lw["w_q"],
            lw["g_kv"], lw["g_conv_out"]]
    if with_kv:
        in_specs += [_const_spec((KV_LORA, D_HEADS_PAD)), _const_spec((D_ATTN, KV_LORA))]
        args += [lw["w_k"], lw["w_vt"]]
        out_shape += [jax.ShapeDtypeStruct((n, s, D_HEADS_PAD), BF16),
                      jax.ShapeDtypeStruct((n, D_ATTN, s), BF16)]
        out_specs += [tok(D_HEADS_PAD),
                      pl.BlockSpec((1, D_ATTN, ts), lambda b, i: (b, 0, i))]
    aliases = {}
    if stacked is not None:
        for j, arr in enumerate(stacked):
            aliases[len(args)] = 2 + j
            in_specs.append(pl.BlockSpec(memory_space=pl.ANY))
            args.append(arr)
    return pl.pallas_call(
        functools.partial(_in_proj_kernel, with_kv=with_kv, n_alias=len(aliases)),
        grid=grid,
        in_specs=in_specs,
        out_specs=tuple(out_specs),
        out_shape=tuple(out_shape),
        input_output_aliases=aliases,
        scratch_shapes=[pltpu.VMEM((8, D_CONV), F32)],
        compiler_params=pltpu.CompilerParams(
            dimension_semantics=("arbitrary", "arbitrary"),
            vmem_limit_bytes=48 * 1024 * 1024),
        name="in_proj_kv" if with_kv else "in_proj",
    )(*args)


def _attn_prompt_kernel(q_ref, k_ref, vt_ref, g_ref, *rest, tq, n_alias):
    o_ref, ot_ref = rest[n_alias:]
    keys = k_ref.shape[1]
    items = [(t, hd) for hd in range(N_HEADS) for t in range(Q_TILES)]
    vis = (lax.broadcasted_iota(jnp.int32, (tq, tq), 0) // CHUNK
           <= lax.broadcasted_iota(jnp.int32, (tq, tq), 1) // CHUNK)

    def scores(t, hd):
        cols = slice(hd * HEAD_PAD, (hd + 1) * HEAD_PAD)
        n_keys = keys - (Q_TILES - 1 - t) * tq
        return _dot_nt(k_ref[0, 0:n_keys, cols], q_ref[0, t * tq:(t + 1) * tq, cols])

    def softmax(st):
        n_keys = st.shape[0]
        last = jnp.where(vis, st[n_keys - tq:, :], NEG_BIG)
        m = jnp.max(last, axis=0, keepdims=True)
        if n_keys == tq:
            return jnp.exp2(last - m).astype(BF16)
        top = st[0:n_keys - tq, :]
        m = jnp.maximum(m, jnp.max(top, axis=0, keepdims=True))
        return jnp.concatenate([jnp.exp2(top - m), jnp.exp2(last - m)], axis=0).astype(BF16)

    def weighted_values(t, hd, pt):
        n_keys = pt.shape[0]
        lhs = jnp.concatenate([vt_ref[0, hd * V_DIM:(hd + 1) * V_DIM, 0:n_keys],
                               jnp.ones((SUM_ROWS, n_keys), BF16)], axis=0)
        a = _dot(lhs, pt)
        ot_ref[t, hd * V_DIM:(hd + 1) * V_DIM, :] = a[0:V_DIM, :] / a[V_DIM:V_DIM + 1, :]

    sts = {n: scores(*items[n]) for n in range(SCORE_LEAD)}
    for n, (t, hd) in enumerate(items):
        lead = n + SCORE_LEAD
        if lead < len(items):
            sts[lead] = scores(*items[lead])
        weighted_values(t, hd, softmax(sts.pop(n)))

    for t in range(Q_TILES):
        o = ot_ref[t]
        rs = lax.rsqrt(jnp.sum(o * o, axis=0, keepdims=True) * (1.0 / D_ATTN) + EPS)
        o_ref[0, t * tq:(t + 1) * tq, :] = ((o * rs).T * g_ref[...]).astype(BF16)


def _attn_prompt(q, k, vt, g_attn, tq):
    n, s, _ = q.shape
    tk = Q_TILES * tq
    assert s % tk == 0
    out = None
    for p in range(s // tk):
        keys = (p + 1) * tk
        in_specs = [
            pl.BlockSpec((1, tk, D_HEADS_PAD), lambda b, p=p: (b, p, 0)),
            pl.BlockSpec((1, keys, D_HEADS_PAD), lambda b: (b, 0, 0)),
            pl.BlockSpec((1, D_ATTN, keys), lambda b: (b, 0, 0)),
            _const_spec((1, D_ATTN)),
        ]
        args = [q, k, vt, g_attn]
        aliases = {}
        if out is not None:
            aliases = {len(args): 0}
            in_specs.append(pl.BlockSpec(memory_space=pl.ANY))
            args.append(out)
        out = pl.pallas_call(
            functools.partial(_attn_prompt_kernel, tq=tq, n_alias=len(aliases)),
            grid=(n,),
            in_specs=in_specs,
            out_specs=pl.BlockSpec((1, tk, D_ATTN), lambda b, p=p: (b, p, 0)),
            out_shape=jax.ShapeDtypeStruct((n, s, D_ATTN), BF16),
            input_output_aliases=aliases,
            scratch_shapes=[pltpu.VMEM((Q_TILES, D_ATTN, tq), F32)],
            compiler_params=pltpu.CompilerParams(
                dimension_semantics=("arbitrary",),
                vmem_limit_bytes=48 * 1024 * 1024),
            name=f"attn_prompt_p{p}",
        )(*args)
    return out


def _attn_sample_kernel(q_ref, cpast_ref, krpast_ref, cnew_ref, krnew_ref, wabs_ref, psel_ref,
                        wuv_ref, g_ref, o_ref):
    l_new = q_ref.shape[1]
    q = q_ref[0]
    ql, qr = [], []
    for hd in range(N_HEADS):
        qh = q[:, hd * HEAD_PAD:(hd + 1) * HEAD_PAD]
        ql.append(_dot(qh, wabs_ref[hd]))
        qr.append(_dot(qh, psel_ref[...]))
    ql = jnp.concatenate(ql, axis=0).astype(BF16)
    qr = jnp.concatenate(qr, axis=0).astype(BF16)
    cp = cpast_ref[0].astype(BF16)
    cn = cnew_ref[0].astype(BF16)
    s_p = _dot_nt(ql, cp) + _dot_nt(qr, krpast_ref[0].astype(BF16))
    s_n = _dot_nt(ql, cn) + _dot_nt(qr, krnew_ref[0].astype(BF16))
    m = jnp.maximum(jnp.max(s_p, axis=-1, keepdims=True), jnp.max(s_n, axis=-1, keepdims=True))
    p_p = jnp.exp2(s_p - m)
    p_n = jnp.exp2(s_n - m)
    l = jnp.sum(p_p, axis=-1, keepdims=True) + jnp.sum(p_n, axis=-1, keepdims=True)
    o_lat = ((_dot(p_p.astype(BF16), cp) + _dot(p_n.astype(BF16), cn)) / l).astype(BF16)
    o = _dot(o_lat[0:l_new, :], wuv_ref[0])
    for hd in range(1, N_HEADS):
        o = o + _dot(o_lat[hd * l_new:(hd + 1) * l_new, :], wuv_ref[hd])
    o_ref[0] = _rms(o, g_ref[...]).astype(BF16)


def _attn_sample(q, c_past, kr_past, c_new, kr_new, layer, lw):
    n, l_new, _ = q.shape
    p_len = c_past.shape[2]
    per_b = lambda rows, w: pl.BlockSpec((1, rows, w), lambda b: (b, 0, 0))
    lay = lambda rows, w: pl.BlockSpec((None, 1, rows, w), lambda b: (layer, b, 0, 0))
    return pl.pallas_call(
        _attn_sample_kernel,
        grid=(n,),
        in_specs=[
            per_b(l_new, D_HEADS_PAD),
            lay(p_len, KV_LORA),
            lay(p_len, QK_ROPE),
            lay(l_new, KV_LORA),
            lay(l_new, QK_ROPE),
            _const_spec((N_HEADS, HEAD_PAD, KV_LORA)),
            _const_spec((HEAD_PAD, QK_ROPE)),
            _const_spec((N_HEADS, KV_LORA, D_ATTN)),
            _const_spec((1, D_ATTN)),
        ],
        out_specs=per_b(l_new, D_ATTN),
        out_shape=jax.ShapeDtypeStruct((n, l_new, D_ATTN), BF16),
        compiler_params=pltpu.CompilerParams(
            dimension_semantics=("arbitrary",),
            vmem_limit_bytes=48 * 1024 * 1024),
        name="attn_sample",
    )(q, c_past, kr_past, c_new, kr_new, lw["w_abs"], lw["p_sel"], lw["w_uv"], lw["g_attn"])


def _out_ffn_kernel(x_ref, convn_ref, attn_ref, p_ref, woc_ref, woa_ref, gpost_ref, gffn_ref,
                    wg_ref, wu_ref, wd_ref, gfpost_ref, wpg_ref, wpp_ref, y_ref, act_ref):
    tm = x_ref.shape[0]
    groups = [slice(i * (tm // ROW_GROUPS), (i + 1) * (tm // ROW_GROUPS)) for i in range(ROW_GROUPS)]
    mix = [_dot(convn_ref[r, :], woc_ref[...]) + _dot(attn_ref[r, :], woa_ref[...]) for r in groups]
    x1, h = [], []
    for r, mx in zip(groups, mix):
        x1.append(x_ref[r, :] + _rms(mx, gpost_ref[...]))
        h.append(_rms(x1[-1], gffn_ref[...]).astype(BF16))
    for r, hr in zip(groups, h):
        for c in range(D_FF // FF_CHUNK):
            cols = slice(c * FF_CHUNK, (c + 1) * FF_CHUNK)
            g = _dot(hr, wg_ref[:, cols])
            u = _dot(hr, wu_ref[:, cols])
            act_ref[r, cols] = (g * jax.nn.sigmoid(g) * u).astype(BF16)
    f = [_dot(act_ref[r, :], wd_ref[...]) for r in groups]
    for r, x1r, fr in zip(groups, x1, f):
        pp = _dot(p_ref[r, :].astype(BF16), wpp_ref[...])
        x2 = x1r + _rms(fr, gfpost_ref[...])
        gate = jax.nn.sigmoid(_dot(x2.astype(BF16), wpg_ref[...]))
        y_ref[r, :] = x2 + gate * pp


def _out_ffn(x, convn, attn, p, layer, lw, tm):
    m = x.shape[0]
    tok = lambda w: pl.BlockSpec((tm, w), lambda i: (i, 0))
    return pl.pallas_call(
        _out_ffn_kernel,
        grid=(m // tm,),
        in_specs=[
            tok(D_MODEL), tok(D_CONV), tok(D_ATTN),
            pl.BlockSpec((None, tm, D_PLE), lambda i: (layer, i, 0)),
            _const_spec((D_CONV, D_MODEL)),
            _const_spec((D_ATTN, D_MODEL)),
            _const_spec((1, D_MODEL)),
            _const_spec((1, D_MODEL)),
            _const_spec((D_MODEL, D_FF)),
            _const_spec((D_MODEL, D_FF)),
            _const_spec((D_FF, D_MODEL)),
            _const_spec((1, D_MODEL)),
            _const_spec((D_MODEL, D_MODEL)),
            _const_spec((D_PLE, D_MODEL)),
        ],
        out_specs=tok(D_MODEL),
        out_shape=jax.ShapeDtypeStruct((m, D_MODEL), F32),
        scratch_shapes=[pltpu.VMEM((tm, D_FF), BF16)],
        compiler_params=pltpu.CompilerParams(
            dimension_semantics=("arbitrary",),
            vmem_limit_bytes=56 * 1024 * 1024),
        name="out_ffn",
    )(x, convn, attn, p, lw["w_o_conv"], lw["w_o_attn"], lw["g_mix_post"], lw["g_ffn_pre"],
      lw["w_gate"], lw["w_up"], lw["w_down"], lw["g_ffn_post"], lw["w_ple_gate"], lw["w_ple_proj"])


def _swap_halves(a):
    half = a.shape[-1] // 2
    return jnp.concatenate([a[..., half:], a[..., :half]], axis=-1)


def _layer_weights(i, g_mix_pre, w_in, w_conv, g_q, w_uq, g_kv, w_ukv, g_conv_out, g_attn_out, w_o,
                   g_mix_post, g_ffn_pre, w_ffn_gate, w_ffn_up, w_ffn_down, g_ffn_post,
                   w_ple_proj, w_ple_gate):
    row = lambda g: g[i].reshape(1, -1).astype(F32)
    w_kr = w_in[i][:, OFF_KR:IN_COLS]
    w_in_ext = jnp.concatenate(
        [w_in[i][:, :OFF_KR], jnp.zeros((D_MODEL, QK_NOPE), F32), w_kr, _swap_halves(w_kr)], axis=1)
    wq3 = w_uq[i].reshape(Q_LORA, N_HEADS, QK_DIM)
    w_q = jnp.concatenate(
        [wq3, _swap_halves(wq3[..., QK_NOPE:])], axis=-1).reshape(Q_LORA, D_HEADS_PAD)
    wkv3 = w_ukv[i].reshape(KV_LORA, N_HEADS, QK_NOPE + V_DIM)
    w_uk = wkv3[..., :QK_NOPE]
    w_uv = wkv3[..., QK_NOPE:]
    zpad = jnp.zeros((KV_LORA, N_HEADS, HEAD_PAD - QK_NOPE), F32)
    w_k = jnp.concatenate([w_uk, zpad], axis=-1).reshape(KV_LORA, D_HEADS_PAD)
    w_vt = w_uv.reshape(KV_LORA, D_ATTN).T
    w_abs = jnp.concatenate([w_uk, zpad], axis=-1).transpose(1, 2, 0)
    head_of_col = jnp.arange(D_ATTN) // V_DIM
    w_uv_heads = jnp.where(head_of_col[None, None, :] == jnp.arange(N_HEADS)[:, None, None],
                           w_uv.reshape(1, KV_LORA, D_ATTN), 0.0)
    p_sel = jnp.zeros((HEAD_PAD, QK_ROPE), F32).at[QK_NOPE:QK_DIM, :].set(jnp.eye(QK_ROPE, dtype=F32))
    return dict(
        g_mix_pre=row(g_mix_pre), w_in=w_in_ext.astype(BF16), w_conv=w_conv[i].astype(F32),
        g_q=row(g_q), w_q=w_q.astype(BF16), g_kv=row(g_kv), w_k=w_k.astype(BF16), w_vt=w_vt.astype(BF16),
        g_conv_out=row(g_conv_out), g_attn=row(g_attn_out),
        w_abs=w_abs.astype(BF16), w_uv=w_uv_heads.astype(BF16), p_sel=p_sel.astype(BF16),
        w_o_conv=w_o[i][:D_CONV].astype(BF16), w_o_attn=w_o[i][D_CONV:].astype(BF16),
        g_mix_post=row(g_mix_post), g_ffn_pre=row(g_ffn_pre),
        w_gate=w_ffn_gate[i].astype(BF16), w_up=w_ffn_up[i].astype(BF16), w_down=w_ffn_down[i].astype(BF16),
        g_ffn_post=row(g_ffn_post), w_ple_gate=w_ple_gate[i].astype(BF16), w_ple_proj=w_ple_proj[i].astype(BF16),
    )


def _rope_tables(pos):
    half = QK_ROPE // 2
    inv = ROPE_THETA ** (-jnp.arange(half, dtype=F32) / half)
    ang = pos.astype(F32)[:, None] * inv[None, :]
    cos, sin = jnp.cos(ang), jnp.sin(ang)
    n = pos.shape[0]
    cc = jnp.concatenate([cos, cos], axis=1)
    ss = jnp.concatenate([-sin, sin], axis=1)
    z_lo = jnp.zeros((n, QK_NOPE), F32)
    z_hi = jnp.zeros((n, HEAD_PAD - QK_DIM), F32)
    c_q = jnp.concatenate([jnp.ones((n, QK_NOPE), F32), cc, z_hi], axis=1) * Q_PRESCALE
    s_q = jnp.concatenate([z_lo, ss, z_hi], axis=1)
    c_k = jnp.concatenate([z_lo, cc, z_hi], axis=1)
    return jnp.stack([c_q, s_q * Q_PRESCALE, c_k, s_q])


def _pick_tile(n, cap):
    t = min(n, cap)
    while n % t:
        t //= 2
    return t


def kernel(x_prompt, x_sample, cache_kv_latent, cache_k_rope, state_conv, p_prompt, p_sample, g_mix_pre, w_in, w_conv, g_q, w_uq, g_kv, w_ukv, g_conv_out, g_attn_out, w_o, g_mix_post, g_ffn_pre, w_ffn_gate, w_ffn_up, w_ffn_down, g_ffn_post, w_ple_proj, w_ple_gate):
    depth = w_in.shape[0]
    n_p, s_p, _ = x_prompt.shape
    n_d, s_d, _ = x_sample.shape
    past_len = cache_kv_latent.shape[2]
    tab_p = _rope_tables(jnp.arange(s_p))
    tab_d = _rope_tables(past_len + jnp.arange(s_d))
    conv_zero = jnp.zeros((1, n_p, CONV_W - 1, D_CONV), F32)
    ts_p = _pick_tile(s_p, 512)
    tq = _pick_tile(s_p, 256)
    tm_p = _pick_tile(n_p * s_p, 512)
    tm_d = _pick_tile(n_d * s_d, 256)
    pp_flat = p_prompt.reshape(depth, n_p * s_p, D_PLE)
    pd_flat = p_sample.reshape(depth, n_d * s_d, D_PLE)

    xp, xd = x_prompt, x_sample
    stack_p = stack_d = None
    for i in range(depth):
        lw = _layer_weights(i, g_mix_pre, w_in, w_conv, g_q, w_uq, g_kv, w_ukv, g_conv_out,
                            g_attn_out, w_o, g_mix_post, g_ffn_pre, w_ffn_gate, w_ffn_up,
                            w_ffn_down, g_ffn_post, w_ple_proj, w_ple_gate)
        convn, q, *stack_p, k, vt = _in_proj(xp, tab_p, conv_zero, 0, lw, i, depth, ts_p, True,
                                             stacked=stack_p)
        attn = _attn_prompt(q, k, vt, lw["g_attn"], tq)
        xp = _out_ffn(xp.reshape(n_p * s_p, D_MODEL), convn.reshape(n_p * s_p, D_CONV),
                      attn.reshape(n_p * s_p, D_ATTN), pp_flat, i, lw, tm_p).reshape(n_p, s_p, D_MODEL)
        convn, q, *stack_d = _in_proj(xd, tab_d, state_conv, i, lw, i, depth, s_d, False,
                                      stacked=stack_d)
        attn = _attn_sample(q, cache_kv_latent, cache_k_rope, stack_d[0], stack_d[1], i, lw)
        xd = _out_ffn(xd.reshape(n_d * s_d, D_MODEL), convn.reshape(n_d * s_d, D_CONV),
                      attn.reshape(n_d * s_d, D_ATTN), pd_flat, i, lw, tm_d).reshape(n_d, s_d, D_MODEL)

    return (xp, xd, *stack_p, *stack_d)
```

```python
import functools

import jax
import jax.numpy as jnp
from jax import lax
from jax.experimental import pallas as pl
from jax.experimental.pallas import tpu as pltpu

D_MODEL = 1024
CHUNK = 64
D_PLE = 256
D_CONV = 512
CONV_W = 3
V_DIM = 64
QK_NOPE = 64
QK_ROPE = 32
QK_DIM = QK_NOPE + QK_ROPE
N_HEADS = 8
D_ATTN = N_HEADS * V_DIM
Q_LORA = 768
KV_LORA = 256
D_FF = 2816
ROPE_THETA = 10000.0
EPS = 1e-6
SCALE = QK_DIM ** -0.5

OFF_C = D_CONV
OFF_X = 2 * D_CONV
OFF_Q = 3 * D_CONV
OFF_KV = OFF_Q + Q_LORA
OFF_KR = OFF_KV + KV_LORA
IN_COLS = OFF_KR + QK_ROPE

LANES = 128
HEAD_PAD = LANES
D_HEADS_PAD = N_HEADS * HEAD_PAD
IN_EXT = OFF_KR + LANES
FF_CHUNK = 256
NEG_BIG = -1e30
Q_PRESCALE = SCALE * 1.4426950408889634
Q_TILES = 2
SCORE_LEAD = 3
ROW_GROUP = 256
SUM_ROWS = 16
SEQ_TILE = 512
Q_TILE = 256
TOKEN_TILE = 512
VMEM_LIMIT = 48 * 1024 * 1024
VMEM_LIMIT_FFN = 56 * 1024 * 1024

BF16 = jnp.bfloat16
F32 = jnp.float32


def _rms(x, g):
    ms = jnp.mean(x * x, axis=-1, keepdims=True)
    return x * lax.rsqrt(ms + EPS) * g


def _dot(a, b):
    return jnp.dot(a, b, preferred_element_type=F32)


def _dot_nt(a, b):
    return lax.dot_general(a, b, (((1,), (1,)), ((), ())), preferred_element_type=F32)


def _const_spec(shape):
    nd = len(shape)
    return pl.BlockSpec(shape, lambda *_: (0,) * nd, pipeline_mode=pl.Buffered(1))


def _in_proj_kernel(x_ref, tab_ref, cinit_ref, gpre_ref, win_ref, wconv_ref, gq_ref, wq_ref,
                    gkv_ref, gconv_ref, *rest, with_kv, n_alias):
    if with_kv:
        wk_ref, wvt_ref = rest[:2]
        rest = rest[2:]
    rest = rest[n_alias:]
    if with_kv:
        convn_ref, q_ref, lat_ref, kr_ref, newconv_ref, k_ref, vt_ref, carry_ref = rest
    else:
        convn_ref, q_ref, lat_ref, kr_ref, newconv_ref, carry_ref = rest
    ts = x_ref.shape[1]
    s_idx = pl.program_id(1)

    @pl.when(s_idx == 0)
    def _():
        carry_ref[0:2, :] = cinit_ref[0]

    ng = max(1, ts // ROW_GROUP)
    gs = ts // ng
    groups = [slice(i * gs, (i + 1) * gs) for i in range(ng)]
    z = []
    for r in groups:
        h = _rms(x_ref[0, r, :], gpre_ref[...]).astype(BF16)
        z.append([_dot(h, win_ref[:, a:b]) for a, b in
                  ((0, OFF_C), (OFF_C, OFF_X), (OFF_X, OFF_Q), (OFF_Q, OFF_KV), (OFF_KV, OFF_KR),
                   (OFF_KR, IN_EXT))])
    cat = lambda j: z[0][j] if ng == 1 else jnp.concatenate([zg[j] for zg in z], axis=0)

    gb = cat(0)
    u = cat(1) * cat(2)
    row = lax.broadcasted_iota(jnp.int32, (ts, 1), 0)
    c0 = carry_ref[0:1, :]
    c1 = carry_ref[1:2, :]
    u_m1 = jnp.where(row == 0, c1, pltpu.roll(u, 1, axis=0))
    u_m2 = pltpu.roll(u, 2, axis=0)
    u_m2 = jnp.where(row == 0, c0, jnp.where(row == 1, c1, u_m2))
    conv = u_m2 * wconv_ref[0:1, :] + u_m1 * wconv_ref[1:2, :] + u * wconv_ref[2:3, :]
    convn_ref[0] = _rms(gb * conv, gconv_ref[...]).astype(BF16)
    tail = u[ts - 2:ts, :]
    carry_ref[0:2, :] = tail
    newconv_ref[0] = tail

    for gi, r in enumerate(groups):
        c_q = tab_ref[0, r, :]
        s_q = tab_ref[1, r, :]
        c_k = tab_ref[2, r, :]
        s_k = tab_ref[3, r, :]

        cqn = _rms(z[gi][3], gq_ref[...]).astype(BF16)
        qa = _dot(cqn, wq_ref[...])
        for hd in range(N_HEADS):
            blk = qa[:, hd * HEAD_PAD:(hd + 1) * HEAD_PAD]
            rot = blk * c_q + pltpu.roll(blk, HEAD_PAD - QK_ROPE, axis=1) * s_q
            q_ref[0, r, hd * HEAD_PAD:(hd + 1) * HEAD_PAD] = rot.astype(BF16)

        c_kv = _rms(z[gi][4], gkv_ref[...])
        lat_ref[0, r, :] = c_kv
        ckvb = c_kv.astype(BF16)
        zkr = z[gi][5]
        kr_blk = zkr * c_k + pltpu.roll(zkr, HEAD_PAD - QK_ROPE, axis=1) * s_k
        kr_ref[0, r, :] = kr_blk[:, QK_NOPE:QK_DIM]
        if with_kv:
            kn = _dot(ckvb, wk_ref[...])
            for hd in range(N_HEADS):
                k_ref[0, r, hd * HEAD_PAD:(hd + 1) * HEAD_PAD] = (
                    kn[:, hd * HEAD_PAD:(hd + 1) * HEAD_PAD] + kr_blk).astype(BF16)
            vt = _dot_nt(wvt_ref[...], ckvb).astype(BF16)
            tk = vt_ref.shape[3]
            start = gi * gs
            vt_ref[0, start // tk, :, start % tk:start % tk + gs] = vt


def _in_proj(x, tables, conv_init, conv_layer, lw, layer, depth, ts, tk=None, stacked=None):
    n, s, _ = x.shape
    with_kv = tk is not None
    grid = (n, s // ts)
    tok = lambda w: pl.BlockSpec((1, ts, w), lambda b, i: (b, i, 0))
    lay = lambda rows, w, tiled: pl.BlockSpec(
        (None, 1, rows, w), (lambda b, i: (layer, b, i, 0)) if tiled else (lambda b, i: (layer, b, 0, 0)))
    out_shape = [
        jax.ShapeDtypeStruct((n, s, D_CONV), BF16),
        jax.ShapeDtypeStruct((n, s, D_HEADS_PAD), BF16),
        jax.ShapeDtypeStruct((depth, n, s, KV_LORA), F32),
        jax.ShapeDtypeStruct((depth, n, s, QK_ROPE), F32),
        jax.ShapeDtypeStruct((depth, n, CONV_W - 1, D_CONV), F32),
    ]
    in_specs = [
        tok(D_MODEL),
        pl.BlockSpec((4, ts, LANES), lambda b, i: (0, i, 0)),
        pl.BlockSpec((None, 1, CONV_W - 1, D_CONV), lambda b, i: (conv_layer, b, 0, 0)),
        _const_spec((1, D_MODEL)),
        _const_spec((D_MODEL, IN_EXT)),
        _const_spec((CONV_W, D_CONV)),
        _const_spec((1, Q_LORA)),
        _const_spec((Q_LORA, D_HEADS_PAD)),
        _const_spec((1, KV_LORA)),
        _const_spec((1, D_CONV)),
    ]
    out_specs = [
        tok(D_CONV), tok(D_HEADS_PAD), lay(ts, KV_LORA, True), lay(ts, QK_ROPE, True),
        lay(CONV_W - 1, D_CONV, False),
    ]
    args = [x, tables, conv_init, lw["g_mix_pre"], lw["w_in"], lw["w_conv"], lw["g_q"], lw["w_q"],
            lw["g_kv"], lw["g_conv_out"]]
    if with_kv:
        in_specs += [_const_spec((KV_LORA, D_HEADS_PAD)), _const_spec((D_ATTN, KV_LORA))]
        args += [lw["w_k"], lw["w_vt"]]
        out_shape += [jax.ShapeDtypeStruct((n, s, D_HEADS_PAD), BF16),
                      jax.ShapeDtypeStruct((n, s // tk, D_ATTN, tk), BF16)]
        out_specs += [tok(D_HEADS_PAD),
                      pl.BlockSpec((1, ts // tk, D_ATTN, tk), lambda b, i: (b, i, 0, 0))]
    aliases = {}
    if stacked is not None:
        for j, arr in enumerate(stacked):
            aliases[len(args)] = 2 + j
            in_specs.append(pl.BlockSpec(memory_space=pl.ANY))
            args.append(arr)
    return pl.pallas_call(
        functools.partial(_in_proj_kernel, with_kv=with_kv, n_alias=len(aliases)),
        grid=grid,
        in_specs=in_specs,
        out_specs=tuple(out_specs),
        out_shape=tuple(out_shape),
        input_output_aliases=aliases,
        scratch_shapes=[pltpu.VMEM((8, D_CONV), F32)],
        compiler_params=pltpu.CompilerParams(
            dimension_semantics=("arbitrary", "arbitrary"),
            vmem_limit_bytes=VMEM_LIMIT),
        name="in_proj_kv" if with_kv else "in_proj",
    )(*args)


def _attn_prompt_kernel(qa_ref, qb_ref, k_ref, vt_ref, g_ref, o_ref, acc_ref, m_ref, ot_ref, ahead_ref,
                        *, tq, n_pairs):
    tk = vt_ref.shape[3]
    step = pl.program_id(1)
    pairs = (step, n_pairs - 1 - step)
    q_refs = (qa_ref, qb_ref)
    m_ref[...] = jnp.full(m_ref.shape, NEG_BIG, F32)
    acc_ref[...] = jnp.zeros(acc_ref.shape, F32)
    items = [(t, hd) for hd in range(N_HEADS) for t in range(Q_TILES)]

    def scores(slot, t, kb, hd, width):
        cols = slice(hd * HEAD_PAD, (hd + 1) * HEAD_PAD)
        k0 = pl.multiple_of(kb * tk, tk)
        return _dot_nt(k_ref[0, pl.ds(k0, width), cols],
                       q_refs[slot][0, t * tq:(t + 1) * tq, cols])

    def visible(t, width):
        ck = lax.broadcasted_iota(jnp.int32, (width, tq), 0) // CHUNK
        rq = (lax.broadcasted_iota(jnp.int32, (width, tq), 1) + t * tq) // CHUNK
        return ck <= rq

    def softmax(slot, t, hd, st, vis):
        if vis is not None:
            st = jnp.where(vis, st, NEG_BIG)
        m_prev = m_ref[slot, t, hd:hd + 1, :]
        m_new = jnp.maximum(m_prev, jnp.max(st, axis=0, keepdims=True))
        m_ref[slot, t, hd:hd + 1, :] = m_new
        return jnp.exp2(m_prev - m_new), jnp.exp2(st - m_new).astype(BF16)

    def weighted_values(slot, t, hd, kb, alpha, pt):
        width = pt.shape[0]
        lhs = jnp.concatenate([vt_ref[0, kb, hd * V_DIM:(hd + 1) * V_DIM, 0:width],
                               jnp.ones((SUM_ROWS, width), BF16)], axis=0)
        acc_ref[slot, t, hd] = alpha * acc_ref[slot, t, hd] + _dot(lhs, pt)

    def block(slot, kb, widths, masked, nxt):
        sts = {n: ahead_ref[slot, n, 0:widths[items[n][0]], :] for n in range(SCORE_LEAD)}
        vis = [visible(t, widths[t]) if masked else None for t in range(Q_TILES)]
        for n, (t, hd) in enumerate(items):
            lead = n + SCORE_LEAD
            if lead < len(items):
                sts[lead] = scores(slot, items[lead][0], kb, items[lead][1], widths[items[lead][0]])
            elif nxt is not None:
                tn, hn = items[lead - len(items)]
                ahead_ref[nxt[0], lead - len(items)] = scores(nxt[0], tn, nxt[1], hn, tk)
            weighted_values(slot, t, hd, kb, *softmax(slot, t, hd, sts.pop(n), vis[t]))

    def write_out(slot):
        for t in range(Q_TILES):
            ssq = jnp.zeros((1, tq), F32)
            for hd in range(N_HEADS):
                a = acc_ref[slot, t, hd]
                o = a[0:V_DIM, :] / a[V_DIM:V_DIM + 1, :]
                ssq = ssq + jnp.sum(o * o, axis=0, keepdims=True)
                ot_ref[slot, t, hd * V_DIM:(hd + 1) * V_DIM, :] = o
            rs = lax.rsqrt(ssq * (1.0 / D_ATTN) + EPS)
            o_ref[0, slot, 0, t * tq:(t + 1) * tq, :] = (
                (ot_ref[slot, t] * rs).T * g_ref[...]).astype(BF16)

    for n in range(SCORE_LEAD):
        ahead_ref[0, n] = scores(0, items[n][0], 0, items[n][1], tk)
    for slot in range(2):
        lax.fori_loop(
            0, pairs[slot],
            lambda kb, c, slot=slot: (block(slot, kb, (tk, tk), False, (slot, kb + 1)), c)[1], 0)
        if slot == 1:
            write_out(0)
        block(slot, pairs[slot], (tq, tk), True, (1, 0) if slot == 0 else None)
    write_out(1)


def _attn_prompt(q, k, vt, g_attn, tq):
    n, s, _ = q.shape
    tk = vt.shape[3]
    n_pairs = s // tk
    assert tk == Q_TILES * tq and s % tk == 0 and n_pairs % 2 == 0
    return pl.pallas_call(
        functools.partial(_attn_prompt_kernel, tq=tq, n_pairs=n_pairs),
        grid=(n, n_pairs // 2),
        in_specs=[
            pl.BlockSpec((1, tk, D_HEADS_PAD), lambda b, i: (b, i, 0)),
            pl.BlockSpec((1, tk, D_HEADS_PAD), lambda b, i: (b, n_pairs - 1 - i, 0)),
            pl.BlockSpec((1, s, D_HEADS_PAD), lambda b, i: (b, 0, 0)),
            pl.BlockSpec((1, n_pairs, D_ATTN, tk), lambda b, i: (b, 0, 0, 0)),
            _const_spec((1, D_ATTN)),
        ],
        out_specs=pl.BlockSpec((1, 2, 1, tk, D_ATTN), lambda b, i: (b, 0, i, 0, 0)),
        out_shape=jax.ShapeDtypeStruct((n, 2, n_pairs // 2, tk, D_ATTN), BF16),
        scratch_shapes=[pltpu.VMEM((2, Q_TILES, N_HEADS, V_DIM + SUM_ROWS, tq), F32),
                        pltpu.VMEM((2, Q_TILES, N_HEADS, tq), F32),
                        pltpu.VMEM((2, Q_TILES, D_ATTN, tq), F32),
                        pltpu.VMEM((2, SCORE_LEAD, tk, tq), F32)],
        compiler_params=pltpu.CompilerParams(
            dimension_semantics=("arbitrary", "arbitrary"),
            vmem_limit_bytes=VMEM_LIMIT),
        name="attn_prompt",
    )(q, q, k, vt, g_attn)


def _attn_sample_kernel(q_ref, cpast_ref, krpast_ref, cnew_ref, krnew_ref, wabs_ref, psel_ref,
                        wuv_ref, g_ref, o_ref):
    l_new = q_ref.shape[1]
    q = q_ref[0]
    ql, qr = [], []
    for hd in range(N_HEADS):
        qh = q[:, hd * HEAD_PAD:(hd + 1) * HEAD_PAD]
        ql.append(_dot(qh, wabs_ref[hd]))
        qr.append(_dot(qh, psel_ref[...]))
    ql = jnp.concatenate(ql, axis=0).astype(BF16)
    qr = jnp.concatenate(qr, axis=0).astype(BF16)
    cp = cpast_ref[0].astype(BF16)
    cn = cnew_ref[0].astype(BF16)
    s_p = _dot_nt(ql, cp) + _dot_nt(qr, krpast_ref[0].astype(BF16))
    s_n = _dot_nt(ql, cn) + _dot_nt(qr, krnew_ref[0].astype(BF16))
    m = jnp.maximum(jnp.max(s_p, axis=-1, keepdims=True), jnp.max(s_n, axis=-1, keepdims=True))
    p_p = jnp.exp2(s_p - m)
    p_n = jnp.exp2(s_n - m)
    l = jnp.sum(p_p, axis=-1, keepdims=True) + jnp.sum(p_n, axis=-1, keepdims=True)
    o_lat = ((_dot(p_p.astype(BF16), cp) + _dot(p_n.astype(BF16), cn)) / l).astype(BF16)
    o = _dot(o_lat[0:l_new, :], wuv_ref[0])
    for hd in range(1, N_HEADS):
        o = o + _dot(o_lat[hd * l_new:(hd + 1) * l_new, :], wuv_ref[hd])
    o_ref[0] = _rms(o, g_ref[...]).astype(BF16)


def _attn_sample(q, c_past, kr_past, c_new, kr_new, layer, lw):
    n, l_new, _ = q.shape
    p_len = c_past.shape[2]
    per_b = lambda rows, w: pl.BlockSpec((1, rows, w), lambda b: (b, 0, 0))
    lay = lambda rows, w: pl.BlockSpec((None, 1, rows, w), lambda b: (layer, b, 0, 0))
    return pl.pallas_call(
        _attn_sample_kernel,
        grid=(n,),
        in_specs=[
            per_b(l_new, D_HEADS_PAD),
            lay(p_len, KV_LORA),
            lay(p_len, QK_ROPE),
            lay(l_new, KV_LORA),
            lay(l_new, QK_ROPE),
            _const_spec((N_HEADS, HEAD_PAD, KV_LORA)),
            _const_spec((HEAD_PAD, QK_ROPE)),
            _const_spec((N_HEADS, KV_LORA, D_ATTN)),
            _const_spec((1, D_ATTN)),
        ],
        out_specs=per_b(l_new, D_ATTN),
        out_shape=jax.ShapeDtypeStruct((n, l_new, D_ATTN), BF16),
        compiler_params=pltpu.CompilerParams(
            dimension_semantics=("arbitrary",),
            vmem_limit_bytes=VMEM_LIMIT),
        name="attn_sample",
    )(q, c_past, kr_past, c_new, kr_new, lw["w_abs"], lw["p_sel"], lw["w_uv"], lw["g_attn"])


def _out_ffn_kernel(x_ref, convn_ref, attn_ref, p_ref, woc_ref, woa_ref, gpost_ref, gffn_ref,
                    wg_ref, wu_ref, wd_ref, gfpost_ref, wpg_ref, wpp_ref, y_ref, act_ref):
    tm = x_ref.shape[0]
    ng = max(1, tm // ROW_GROUP)
    groups = [slice(i * (tm // ng), (i + 1) * (tm // ng)) for i in range(ng)]
    mix = [_dot(convn_ref[r, :], woc_ref[...]) + _dot(attn_ref[r, :], woa_ref[...]) for r in groups]
    x1, h = [], []
    for r, mx in zip(groups, mix):
        x1.append(x_ref[r, :] + _rms(mx, gpost_ref[...]))
        h.append(_rms(x1[-1], gffn_ref[...]).astype(BF16))
    for r, hr in zip(groups, h):
        for c in range(D_FF // FF_CHUNK):
            cols = slice(c * FF_CHUNK, (c + 1) * FF_CHUNK)
            g = _dot(hr, wg_ref[:, cols])
            u = _dot(hr, wu_ref[:, cols])
            act_ref[r, cols] = (g * jax.nn.sigmoid(g) * u).astype(BF16)
    f = [_dot(act_ref[r, :], wd_ref[...]) for r in groups]
    for r, x1r, fr in zip(groups, x1, f):
        pp = _dot(p_ref[r, :].astype(BF16), wpp_ref[...])
        x2 = x1r + _rms(fr, gfpost_ref[...])
        gate = jax.nn.sigmoid(_dot(x2.astype(BF16), wpg_ref[...]))
        y_ref[r, :] = x2 + gate * pp


def _out_ffn(x, convn, attn, attn_index, p, layer, lw, tm):
    m = x.shape[0]
    tok = lambda w: pl.BlockSpec((tm, w), lambda i: (i, 0))
    return pl.pallas_call(
        _out_ffn_kernel,
        grid=(m // tm,),
        in_specs=[
            tok(D_MODEL), tok(D_CONV),
            pl.BlockSpec((None, None, None, tm, D_ATTN), lambda i: (*attn_index(i), 0)),
            pl.BlockSpec((None, tm, D_PLE), lambda i: (layer, i, 0)),
            _const_spec((D_CONV, D_MODEL)),
            _const_spec((D_ATTN, D_MODEL)),
            _const_spec((1, D_MODEL)),
            _const_spec((1, D_MODEL)),
            _const_spec((D_MODEL, D_FF)),
            _const_spec((D_MODEL, D_FF)),
            _const_spec((D_FF, D_MODEL)),
            _const_spec((1, D_MODEL)),
            _const_spec((D_MODEL, D_MODEL)),
            _const_spec((D_PLE, D_MODEL)),
        ],
        out_specs=tok(D_MODEL),
        out_shape=jax.ShapeDtypeStruct((m, D_MODEL), F32),
        scratch_shapes=[pltpu.VMEM((tm, D_FF), BF16)],
        compiler_params=pltpu.CompilerParams(
            dimension_semantics=("arbitrary",),
            vmem_limit_bytes=VMEM_LIMIT_FFN),
        name="out_ffn",
    )(x, convn, attn, p, lw["w_o_conv"], lw["w_o_attn"], lw["g_mix_post"], lw["g_ffn_pre"],
      lw["w_gate"], lw["w_up"], lw["w_down"], lw["g_ffn_post"], lw["w_ple_gate"], lw["w_ple_proj"])


def _swap_halves(a):
    half = a.shape[-1] // 2
    return jnp.concatenate([a[..., half:], a[..., :half]], axis=-1)


def _layer_weights(i, g_mix_pre, w_in, w_conv, g_q, w_uq, g_kv, w_ukv, g_conv_out, g_attn_out, w_o,
                   g_mix_post, g_ffn_pre, w_ffn_gate, w_ffn_up, w_ffn_down, g_ffn_post,
                   w_ple_proj, w_ple_gate):
    row = lambda g: g[i].reshape(1, -1).astype(F32)
    w_kr = w_in[i][:, OFF_KR:IN_COLS]
    w_in_ext = jnp.concatenate(
        [w_in[i][:, :OFF_KR], jnp.zeros((D_MODEL, QK_NOPE), F32), w_kr, _swap_halves(w_kr)], axis=1)
    wq3 = w_uq[i].reshape(Q_LORA, N_HEADS, QK_DIM)
    w_q = jnp.concatenate(
        [wq3, _swap_halves(wq3[..., QK_NOPE:])], axis=-1).reshape(Q_LORA, D_HEADS_PAD)
    wkv3 = w_ukv[i].reshape(KV_LORA, N_HEADS, QK_NOPE + V_DIM)
    w_uk = wkv3[..., :QK_NOPE]
    w_uv = wkv3[..., QK_NOPE:]
    zpad = jnp.zeros((KV_LORA, N_HEADS, HEAD_PAD - QK_NOPE), F32)
    w_k = jnp.concatenate([w_uk, zpad], axis=-1).reshape(KV_LORA, D_HEADS_PAD)
    w_vt = w_uv.reshape(KV_LORA, D_ATTN).T
    w_abs = jnp.concatenate([w_uk, zpad], axis=-1).transpose(1, 2, 0)
    head_of_col = jnp.arange(D_ATTN) // V_DIM
    w_uv_heads = jnp.where(head_of_col[None, None, :] == jnp.arange(N_HEADS)[:, None, None],
                           w_uv.reshape(1, KV_LORA, D_ATTN), 0.0)
    p_sel = jnp.zeros((HEAD_PAD, QK_ROPE), F32).at[QK_NOPE:QK_DIM, :].set(jnp.eye(QK_ROPE, dtype=F32))
    return dict(
        g_mix_pre=row(g_mix_pre), w_in=w_in_ext.astype(BF16), w_conv=w_conv[i].astype(F32),
        g_q=row(g_q), w_q=w_q.astype(BF16), g_kv=row(g_kv), w_k=w_k.astype(BF16), w_vt=w_vt.astype(BF16),
        g_conv_out=row(g_conv_out), g_attn=row(g_attn_out),
        w_abs=w_abs.astype(BF16), w_uv=w_uv_heads.astype(BF16), p_sel=p_sel.astype(BF16),
        w_o_conv=w_o[i][:D_CONV].astype(BF16), w_o_attn=w_o[i][D_CONV:].astype(BF16),
        g_mix_post=row(g_mix_post), g_ffn_pre=row(g_ffn_pre),
        w_gate=w_ffn_gate[i].astype(BF16), w_up=w_ffn_up[i].astype(BF16), w_down=w_ffn_down[i].astype(BF16),
        g_ffn_post=row(g_ffn_post), w_ple_gate=w_ple_gate[i].astype(BF16), w_ple_proj=w_ple_proj[i].astype(BF16),
    )


def _rope_tables(pos):
    half = QK_ROPE // 2
    inv = ROPE_THETA ** (-jnp.arange(half, dtype=F32) / half)
    ang = pos.astype(F32)[:, None] * inv[None, :]
    cos, sin = jnp.cos(ang), jnp.sin(ang)
    n = pos.shape[0]
    cc = jnp.concatenate([cos, cos], axis=1)
    ss = jnp.concatenate([-sin, sin], axis=1)
    z_lo = jnp.zeros((n, QK_NOPE), F32)
    z_hi = jnp.zeros((n, HEAD_PAD - QK_DIM), F32)
    c_q = jnp.concatenate([jnp.ones((n, QK_NOPE), F32), cc, z_hi], axis=1) * Q_PRESCALE
    s_q = jnp.concatenate([z_lo, ss, z_hi], axis=1)
    c_k = jnp.concatenate([z_lo, cc, z_hi], axis=1)
    return jnp.stack([c_q, s_q * Q_PRESCALE, c_k, s_q])


def _pick_tile(n, cap):
    t = min(n, cap)
    while n % t:
        t //= 2
    return t


def kernel(x_prompt, x_sample, cache_kv_latent, cache_k_rope, state_conv, p_prompt, p_sample, g_mix_pre, w_in, w_conv, g_q, w_uq, g_kv, w_ukv, g_conv_out, g_attn_out, w_o, g_mix_post, g_ffn_pre, w_ffn_gate, w_ffn_up, w_ffn_down, g_ffn_post, w_ple_proj, w_ple_gate):
    depth = w_in.shape[0]
    n_p, s_p, _ = x_prompt.shape
    n_d, s_d, _ = x_sample.shape
    past_len = cache_kv_latent.shape[2]
    tab_p = _rope_tables(jnp.arange(s_p))
    tab_d = _rope_tables(past_len + jnp.arange(s_d))
    conv_zero = jnp.zeros((1, n_p, CONV_W - 1, D_CONV), F32)
    ts_p = _pick_tile(s_p, SEQ_TILE)
    tq = _pick_tile(s_p, Q_TILE)
    tm_p = _pick_tile(n_p * s_p, TOKEN_TILE)
    tm_d = _pick_tile(n_d * s_d, TOKEN_TILE)
    n_pairs = s_p // (Q_TILES * tq)
    assert tm_p == Q_TILES * tq

    def attn_tile_index(t):
        b, p = t // n_pairs, t % n_pairs
        second = p // (n_pairs // 2)
        return b, second, jnp.where(second == 1, n_pairs - 1 - p, p), 0

    pp_flat = p_prompt.reshape(depth, n_p * s_p, D_PLE)
    pd_flat = p_sample.reshape(depth, n_d * s_d, D_PLE)

    xp, xd = x_prompt, x_sample
    stack_p = stack_d = None
    for i in range(depth):
        lw = _layer_weights(i, g_mix_pre, w_in, w_conv, g_q, w_uq, g_kv, w_ukv, g_conv_out,
                            g_attn_out, w_o, g_mix_post, g_ffn_pre, w_ffn_gate, w_ffn_up,
                            w_ffn_down, g_ffn_post, w_ple_proj, w_ple_gate)
        convn, q, *stack_p, k, vt = _in_proj(xp, tab_p, conv_zero, 0, lw, i, depth, ts_p,
                                             tk=Q_TILES * tq, stacked=stack_p)
        attn = _attn_prompt(q, k, vt, lw["g_attn"], tq)
        xp = _out_ffn(xp.reshape(n_p * s_p, D_MODEL), convn.reshape(n_p * s_p, D_CONV),
                      attn, attn_tile_index, pp_flat, i, lw, tm_p).reshape(n_p, s_p, D_MODEL)
        convn, q, *stack_d = _in_proj(xd, tab_d, state_conv, i, lw, i, depth, s_d, stacked=stack_d)
        attn = _attn_sample(q, cache_kv_latent, cache_k_rope, stack_d[0], stack_d[1], i, lw)
        xd = _out_ffn(xd.reshape(n_d * s_d, D_MODEL), convn.reshape(n_d * s_d, D_CONV),
                      attn.reshape(1, 1, 1, n_d * s_d, D_ATTN), lambda t: (0, 0, 0, t), pd_flat, i, lw,
                      tm_d).reshape(n_d, s_d, D_MODEL)

    return (xp, xd, *stack_p, *stack_d)
```

```python
import functools

import jax
import jax.numpy as jnp
from jax import lax
from jax.experimental import pallas as pl
from jax.experimental.pallas import tpu as pltpu

D_MODEL = 1024
CHUNK = 64
D_PLE = 256
D_CONV = 512
CONV_W = 3
V_DIM = 64
QK_NOPE = 64
QK_ROPE = 32
QK_DIM = QK_NOPE + QK_ROPE
N_HEADS = 8
D_ATTN = N_HEADS * V_DIM
Q_LORA = 768
KV_LORA = 256
D_FF = 2816
ROPE_THETA = 10000.0
EPS = 1e-6
SCALE = QK_DIM ** -0.5

OFF_C = D_CONV
OFF_X = 2 * D_CONV
OFF_Q = 3 * D_CONV
OFF_KV = OFF_Q + Q_LORA
OFF_KR = OFF_KV + KV_LORA
IN_COLS = OFF_KR + QK_ROPE

LANES = 128
SUBLANES = 8
HEAD_PAD = LANES
D_HEADS_PAD = N_HEADS * HEAD_PAD
IN_EXT = OFF_KR + LANES
FF_CHUNK = 256
NEG_BIG = -1e30
Q_PRESCALE = SCALE * 1.4426950408889634
Q_TILES = 2
SCORE_LEAD = 3
ROW_GROUP = 256
SUM_ROWS = 16
SEQ_TILE = 512
Q_TILE = 256
TOKEN_TILE = 512
VMEM_LIMIT = 48 * 1024 * 1024
VMEM_LIMIT_FFN = 56 * 1024 * 1024

BF16 = jnp.bfloat16
F32 = jnp.float32


def _rms(x, g):
    ms = jnp.mean(x * x, axis=-1, keepdims=True)
    return x * lax.rsqrt(ms + EPS) * g


def _dot(a, b):
    return jnp.dot(a, b, preferred_element_type=F32)


def _dot_nt(a, b):
    return lax.dot_general(a, b, (((1,), (1,)), ((), ())), preferred_element_type=F32)


def _const_spec(shape):
    nd = len(shape)
    return pl.BlockSpec(shape, lambda *_: (0,) * nd, pipeline_mode=pl.Buffered(1))


def _in_proj_kernel(x_ref, tab_ref, cinit_ref, gpre_ref, win_ref, wconv_ref, gq_ref, wq_ref,
                    gkv_ref, gconv_ref, *rest, with_kv, n_alias):
    if with_kv:
        wk_ref, wvt_ref = rest[:2]
        rest = rest[2:]
    rest = rest[n_alias:]
    if with_kv:
        convn_ref, q_ref, lat_ref, kr_ref, newconv_ref, k_ref, vt_ref, carry_ref = rest
    else:
        convn_ref, q_ref, lat_ref, kr_ref, newconv_ref, carry_ref = rest
    ts = x_ref.shape[1]
    s_idx = pl.program_id(1)

    @pl.when(s_idx == 0)
    def _():
        carry_ref[0:2, :] = cinit_ref[0]

    ng = max(1, ts // ROW_GROUP)
    gs = ts // ng
    groups = [slice(i * gs, (i + 1) * gs) for i in range(ng)]
    z = []
    for r in groups:
        h = _rms(x_ref[0, r, :], gpre_ref[...]).astype(BF16)
        z.append([_dot(h, win_ref[:, a:b]) for a, b in
                  ((0, OFF_C), (OFF_C, OFF_X), (OFF_X, OFF_Q), (OFF_Q, OFF_KV), (OFF_KV, OFF_KR),
                   (OFF_KR, IN_EXT))])
    cat = lambda j: z[0][j] if ng == 1 else jnp.concatenate([zg[j] for zg in z], axis=0)

    assert CONV_W == 3
    gb = cat(0)
    u = cat(1) * cat(2)
    row = lax.broadcasted_iota(jnp.int32, (ts, 1), 0)
    c0 = carry_ref[0:1, :]
    c1 = carry_ref[1:2, :]
    u_m1 = jnp.where(row == 0, c1, pltpu.roll(u, 1, axis=0))
    u_m2 = pltpu.roll(u, 2, axis=0)
    u_m2 = jnp.where(row == 0, c0, jnp.where(row == 1, c1, u_m2))
    conv = u_m2 * wconv_ref[0:1, :] + u_m1 * wconv_ref[1:2, :] + u * wconv_ref[2:3, :]
    convn_ref[0] = _rms(gb * conv, gconv_ref[...]).astype(BF16)
    tail = u[ts - 2:ts, :]
    carry_ref[0:2, :] = tail
    newconv_ref[0] = tail

    for gi, r in enumerate(groups):
        c_q = tab_ref[0, r, :]
        s_q = tab_ref[1, r, :]
        c_k = tab_ref[2, r, :]
        s_k = tab_ref[3, r, :]

        cqn = _rms(z[gi][3], gq_ref[...]).astype(BF16)
        qa = _dot(cqn, wq_ref[...])
        for hd in range(N_HEADS):
            blk = qa[:, hd * HEAD_PAD:(hd + 1) * HEAD_PAD]
            rot = blk * c_q + pltpu.roll(blk, HEAD_PAD - QK_ROPE, axis=1) * s_q
            q_ref[0, r, hd * HEAD_PAD:(hd + 1) * HEAD_PAD] = rot.astype(BF16)

        c_kv = _rms(z[gi][4], gkv_ref[...])
        lat_ref[0, r, :] = c_kv
        ckvb = c_kv.astype(BF16)
        zkr = z[gi][5]
        kr_blk = zkr * c_k + pltpu.roll(zkr, HEAD_PAD - QK_ROPE, axis=1) * s_k
        kr_ref[0, r, :] = kr_blk[:, QK_NOPE:QK_DIM]
        if with_kv:
            kn = _dot(ckvb, wk_ref[...])
            for hd in range(N_HEADS):
                k_ref[0, r, hd * HEAD_PAD:(hd + 1) * HEAD_PAD] = (
                    kn[:, hd * HEAD_PAD:(hd + 1) * HEAD_PAD] + kr_blk).astype(BF16)
            vt = _dot_nt(wvt_ref[...], ckvb).astype(BF16)
            tk = vt_ref.shape[3]
            start = gi * gs
            vt_ref[0, start // tk, :, start % tk:start % tk + gs] = vt


def _in_proj(x, tables, conv_init, conv_layer, lw, layer, depth, ts, tk=None, stacked=None):
    n, s, _ = x.shape
    with_kv = tk is not None
    grid = (n, s // ts)
    tok = lambda w: pl.BlockSpec((1, ts, w), lambda b, i: (b, i, 0))
    lay = lambda rows, w, tiled: pl.BlockSpec(
        (None, 1, rows, w), (lambda b, i: (layer, b, i, 0)) if tiled else (lambda b, i: (layer, b, 0, 0)))
    out_shape = [
        jax.ShapeDtypeStruct((n, s, D_CONV), BF16),
        jax.ShapeDtypeStruct((n, s, D_HEADS_PAD), BF16),
        jax.ShapeDtypeStruct((depth, n, s, KV_LORA), F32),
        jax.ShapeDtypeStruct((depth, n, s, QK_ROPE), F32),
        jax.ShapeDtypeStruct((depth, n, CONV_W - 1, D_CONV), F32),
    ]
    in_specs = [
        tok(D_MODEL),
        pl.BlockSpec((4, ts, LANES), lambda b, i: (0, i, 0)),
        pl.BlockSpec((None, 1, CONV_W - 1, D_CONV), lambda b, i: (conv_layer, b, 0, 0)),
        _const_spec((1, D_MODEL)),
        _const_spec((D_MODEL, IN_EXT)),
        _const_spec((CONV_W, D_CONV)),
        _const_spec((1, Q_LORA)),
        _const_spec((Q_LORA, D_HEADS_PAD)),
        _const_spec((1, KV_LORA)),
        _const_spec((1, D_CONV)),
    ]
    out_specs = [
        tok(D_CONV), tok(D_HEADS_PAD), lay(ts, KV_LORA, True), lay(ts, QK_ROPE, True),
        lay(CONV_W - 1, D_CONV, False),
    ]
    args = [x, tables, conv_init, lw["g_mix_pre"], lw["w_in"], lw["w_conv"], lw["g_q"], lw["w_q"],
            lw["g_kv"], lw["g_conv_out"]]
    if with_kv:
        in_specs += [_const_spec((KV_LORA, D_HEADS_PAD)), _const_spec((D_ATTN, KV_LORA))]
        args += [lw["w_k"], lw["w_vt"]]
        out_shape += [jax.ShapeDtypeStruct((n, s, D_HEADS_PAD), BF16),
                      jax.ShapeDtypeStruct((n, s // tk, D_ATTN, tk), BF16)]
        out_specs += [tok(D_HEADS_PAD),
                      pl.BlockSpec((1, ts // tk, D_ATTN, tk), lambda b, i: (b, i, 0, 0))]
    aliases = {}
    if stacked is not None:
        for j, arr in enumerate(stacked):
            aliases[len(args)] = 2 + j
            in_specs.append(pl.BlockSpec(memory_space=pl.ANY))
            args.append(arr)
    return pl.pallas_call(
        functools.partial(_in_proj_kernel, with_kv=with_kv, n_alias=len(aliases)),
        grid=grid,
        in_specs=in_specs,
        out_specs=tuple(out_specs),
        out_shape=tuple(out_shape),
        input_output_aliases=aliases,
        scratch_shapes=[pltpu.VMEM((SUBLANES, D_CONV), F32)],
        compiler_params=pltpu.CompilerParams(
            dimension_semantics=("arbitrary", "arbitrary"),
            vmem_limit_bytes=VMEM_LIMIT),
        name="in_proj_kv" if with_kv else "in_proj",
    )(*args)


def _attn_prompt_kernel(qa_ref, qb_ref, k_ref, vt_ref, g_ref, o_ref, acc_ref, m_ref, ot_ref, ahead_ref,
                        *, tq, n_pairs):
    tk = vt_ref.shape[3]
    step = pl.program_id(1)
    pairs = (step, n_pairs - 1 - step)
    q_refs = (qa_ref, qb_ref)
    m_ref[...] = jnp.full(m_ref.shape, NEG_BIG, F32)
    acc_ref[...] = jnp.zeros(acc_ref.shape, F32)
    items = [(t, hd) for hd in range(N_HEADS) for t in range(Q_TILES)]

    def scores(slot, t, kb, hd, width):
        cols = slice(hd * HEAD_PAD, (hd + 1) * HEAD_PAD)
        k0 = pl.multiple_of(kb * tk, tk)
        return _dot_nt(k_ref[0, pl.ds(k0, width), cols],
                       q_refs[slot][0, t * tq:(t + 1) * tq, cols])

    def visible(t, width):
        ck = lax.broadcasted_iota(jnp.int32, (width, tq), 0) // CHUNK
        rq = (lax.broadcasted_iota(jnp.int32, (width, tq), 1) + t * tq) // CHUNK
        return ck <= rq

    def softmax(slot, t, hd, st, vis):
        if vis is not None:
            st = jnp.where(vis, st, NEG_BIG)
        m_prev = m_ref[slot, t, hd:hd + 1, :]
        m_new = jnp.maximum(m_prev, jnp.max(st, axis=0, keepdims=True))
        m_ref[slot, t, hd:hd + 1, :] = m_new
        return jnp.exp2(m_prev - m_new), jnp.exp2(st - m_new).astype(BF16)

    def weighted_values(slot, t, hd, kb, alpha, pt):
        width = pt.shape[0]
        lhs = jnp.concatenate([vt_ref[0, kb, hd * V_DIM:(hd + 1) * V_DIM, 0:width],
                               jnp.ones((SUM_ROWS, width), BF16)], axis=0)
        acc_ref[slot, t, hd] = alpha * acc_ref[slot, t, hd] + _dot(lhs, pt)

    def block(slot, kb, widths, masked, nxt):
        sts = {n: ahead_ref[slot, n, 0:widths[items[n][0]], :] for n in range(SCORE_LEAD)}
        vis = [visible(t, widths[t]) if masked else None for t in range(Q_TILES)]
        for n, (t, hd) in enumerate(items):
            lead = n + SCORE_LEAD
            if lead < len(items):
                sts[lead] = scores(slot, items[lead][0], kb, items[lead][1], widths[items[lead][0]])
            elif nxt is not None:
                tn, hn = items[lead - len(items)]
                ahead_ref[nxt[0], lead - len(items)] = scores(nxt[0], tn, nxt[1], hn, tk)
            weighted_values(slot, t, hd, kb, *softmax(slot, t, hd, sts.pop(n), vis[t]))

    def write_out(slot):
        for t in range(Q_TILES):
            ssq = jnp.zeros((1, tq), F32)
            for hd in range(N_HEADS):
                a = acc_ref[slot, t, hd]
                o = a[0:V_DIM, :] / a[V_DIM:V_DIM + 1, :]
                ssq = ssq + jnp.sum(o * o, axis=0, keepdims=True)
                ot_ref[slot, t, hd * V_DIM:(hd + 1) * V_DIM, :] = o
            rs = lax.rsqrt(ssq * (1.0 / D_ATTN) + EPS)
            o_ref[0, slot, 0, t * tq:(t + 1) * tq, :] = (
                (ot_ref[slot, t] * rs).T * g_ref[...]).astype(BF16)

    for n in range(SCORE_LEAD):
        ahead_ref[0, n] = scores(0, items[n][0], 0, items[n][1], tk)
    for slot in range(2):
        lax.fori_loop(
            0, pairs[slot],
            lambda kb, c, slot=slot: (block(slot, kb, (tk, tk), False, (slot, kb + 1)), c)[1], 0)
        if slot == 1:
            write_out(0)
        block(slot, pairs[slot], (tq, tk), True, (1, 0) if slot == 0 else None)
    write_out(1)


def _attn_prompt(q, k, vt, g_attn, tq):
    n, s, _ = q.shape
    tk = vt.shape[3]
    n_pairs = s // tk
    assert tk == Q_TILES * tq and s % tk == 0 and n_pairs % 2 == 0
    return pl.pallas_call(
        functools.partial(_attn_prompt_kernel, tq=tq, n_pairs=n_pairs),
        grid=(n, n_pairs // 2),
        in_specs=[
            pl.BlockSpec((1, tk, D_HEADS_PAD), lambda b, i: (b, i, 0)),
            pl.BlockSpec((1, tk, D_HEADS_PAD), lambda b, i: (b, n_pairs - 1 - i, 0)),
            pl.BlockSpec((1, s, D_HEADS_PAD), lambda b, i: (b, 0, 0)),
            pl.BlockSpec((1, n_pairs, D_ATTN, tk), lambda b, i: (b, 0, 0, 0)),
            _const_spec((1, D_ATTN)),
        ],
        out_specs=pl.BlockSpec((1, 2, 1, tk, D_ATTN), lambda b, i: (b, 0, i, 0, 0)),
        out_shape=jax.ShapeDtypeStruct((n, 2, n_pairs // 2, tk, D_ATTN), BF16),
        scratch_shapes=[pltpu.VMEM((2, Q_TILES, N_HEADS, V_DIM + SUM_ROWS, tq), F32),
                        pltpu.VMEM((2, Q_TILES, N_HEADS, tq), F32),
                        pltpu.VMEM((2, Q_TILES, D_ATTN, tq), F32),
                        pltpu.VMEM((2, SCORE_LEAD, tk, tq), F32)],
        compiler_params=pltpu.CompilerParams(
            dimension_semantics=("arbitrary", "arbitrary"),
            vmem_limit_bytes=VMEM_LIMIT),
        name="attn_prompt",
    )(q, q, k, vt, g_attn)


def _attn_sample_kernel(q_ref, cpast_ref, krpast_ref, cnew_ref, krnew_ref, wabs_ref, psel_ref,
                        wuv_ref, g_ref, o_ref):
    l_new = q_ref.shape[1]
    q = q_ref[0]
    ql, qr = [], []
    for hd in range(N_HEADS):
        qh = q[:, hd * HEAD_PAD:(hd + 1) * HEAD_PAD]
        ql.append(_dot(qh, wabs_ref[hd]))
        qr.append(_dot(qh, psel_ref[...]))
    ql = jnp.concatenate(ql, axis=0).astype(BF16)
    qr = jnp.concatenate(qr, axis=0).astype(BF16)
    cp = cpast_ref[0].astype(BF16)
    cn = cnew_ref[0].astype(BF16)
    s_p = _dot_nt(ql, cp) + _dot_nt(qr, krpast_ref[0].astype(BF16))
    s_n = _dot_nt(ql, cn) + _dot_nt(qr, krnew_ref[0].astype(BF16))
    m = jnp.maximum(jnp.max(s_p, axis=-1, keepdims=True), jnp.max(s_n, axis=-1, keepdims=True))
    p_p = jnp.exp2(s_p - m)
    p_n = jnp.exp2(s_n - m)
    l = jnp.sum(p_p, axis=-1, keepdims=True) + jnp.sum(p_n, axis=-1, keepdims=True)
    o_lat = ((_dot(p_p.astype(BF16), cp) + _dot(p_n.astype(BF16), cn)) / l).astype(BF16)
    o = _dot(o_lat[0:l_new, :], wuv_ref[0])
    for hd in range(1, N_HEADS):
        o = o + _dot(o_lat[hd * l_new:(hd + 1) * l_new, :], wuv_ref[hd])
    o_ref[0] = _rms(o, g_ref[...]).astype(BF16)


def _attn_sample(q, c_past, kr_past, c_new, kr_new, layer, lw):
    n, l_new, _ = q.shape
    p_len = c_past.shape[2]
    per_b = lambda rows, w: pl.BlockSpec((1, rows, w), lambda b: (b, 0, 0))
    lay = lambda rows, w: pl.BlockSpec((None, 1, rows, w), lambda b: (layer, b, 0, 0))
    return pl.pallas_call(
        _attn_sample_kernel,
        grid=(n,),
        in_specs=[
            per_b(l_new, D_HEADS_PAD),
            lay(p_len, KV_LORA),
            lay(p_len, QK_ROPE),
            lay(l_new, KV_LORA),
            lay(l_new, QK_ROPE),
            _const_spec((N_HEADS, HEAD_PAD, KV_LORA)),
            _const_spec((HEAD_PAD, QK_ROPE)),
            _const_spec((N_HEADS, KV_LORA, D_ATTN)),
            _const_spec((1, D_ATTN)),
        ],
        out_specs=per_b(l_new, D_ATTN),
        out_shape=jax.ShapeDtypeStruct((n, l_new, D_ATTN), BF16),
        compiler_params=pltpu.CompilerParams(
            dimension_semantics=("arbitrary",),
            vmem_limit_bytes=VMEM_LIMIT),
        name="attn_sample",
    )(q, c_past, kr_past, c_new, kr_new, lw["w_abs"], lw["p_sel"], lw["w_uv"], lw["g_attn"])


def _out_ffn_kernel(x_ref, convn_ref, attn_ref, p_ref, woc_ref, woa_ref, gpost_ref, gffn_ref,
                    wg_ref, wu_ref, wd_ref, gfpost_ref, wpg_ref, wpp_ref, y_ref, act_ref):
    tm = x_ref.shape[0]
    ng = max(1, tm // ROW_GROUP)
    groups = [slice(i * (tm // ng), (i + 1) * (tm // ng)) for i in range(ng)]
    mix = [_dot(convn_ref[r, :], woc_ref[...]) + _dot(attn_ref[r, :], woa_ref[...]) for r in groups]
    x1, h = [], []
    for r, mx in zip(groups, mix):
        x1.append(x_ref[r, :] + _rms(mx, gpost_ref[...]))
        h.append(_rms(x1[-1], gffn_ref[...]).astype(BF16))
    for r, hr in zip(groups, h):
        for c in range(D_FF // FF_CHUNK):
            cols = slice(c * FF_CHUNK, (c + 1) * FF_CHUNK)
            g = _dot(hr, wg_ref[:, cols])
            u = _dot(hr, wu_ref[:, cols])
            act_ref[r, cols] = (g * jax.nn.sigmoid(g) * u).astype(BF16)
    f = [_dot(act_ref[r, :], wd_ref[...]) for r in groups]
    for r, x1r, fr in zip(groups, x1, f):
        pp = _dot(p_ref[r, :].astype(BF16), wpp_ref[...])
        x2 = x1r + _rms(fr, gfpost_ref[...])
        gate = jax.nn.sigmoid(_dot(x2.astype(BF16), wpg_ref[...]))
        y_ref[r, :] = x2 + gate * pp


def _out_ffn(x, convn, attn, attn_index, p, layer, lw, tm):
    m = x.shape[0]
    tok = lambda w: pl.BlockSpec((tm, w), lambda i: (i, 0))
    return pl.pallas_call(
        _out_ffn_kernel,
        grid=(m // tm,),
        in_specs=[
            tok(D_MODEL), tok(D_CONV),
            pl.BlockSpec((None, None, None, tm, D_ATTN), lambda i: (*attn_index(i), 0)),
            pl.BlockSpec((None, tm, D_PLE), lambda i: (layer, i, 0)),
            _const_spec((D_CONV, D_MODEL)),
            _const_spec((D_ATTN, D_MODEL)),
            _const_spec((1, D_MODEL)),
            _const_spec((1, D_MODEL)),
            _const_spec((D_MODEL, D_FF)),
            _const_spec((D_MODEL, D_FF)),
            _const_spec((D_FF, D_MODEL)),
            _const_spec((1, D_MODEL)),
            _const_spec((D_MODEL, D_MODEL)),
            _const_spec((D_PLE, D_MODEL)),
        ],
        out_specs=tok(D_MODEL),
        out_shape=jax.ShapeDtypeStruct((m, D_MODEL), F32),
        scratch_shapes=[pltpu.VMEM((tm, D_FF), BF16)],
        compiler_params=pltpu.CompilerParams(
            dimension_semantics=("arbitrary",),
            vmem_limit_bytes=VMEM_LIMIT_FFN),
        name="out_ffn",
    )(x, convn, attn, p, lw["w_o_conv"], lw["w_o_attn"], lw["g_mix_post"], lw["g_ffn_pre"],
      lw["w_gate"], lw["w_up"], lw["w_down"], lw["g_ffn_post"], lw["w_ple_gate"], lw["w_ple_proj"])


def _swap_halves(a):
    half = a.shape[-1] // 2
    return jnp.concatenate([a[..., half:], a[..., :half]], axis=-1)


def _layer_weights(i, g_mix_pre, w_in, w_conv, g_q, w_uq, g_kv, w_ukv, g_conv_out, g_attn_out, w_o,
                   g_mix_post, g_ffn_pre, w_ffn_gate, w_ffn_up, w_ffn_down, g_ffn_post,
                   w_ple_proj, w_ple_gate):
    row = lambda g: g[i].reshape(1, -1).astype(F32)
    w_kr = w_in[i][:, OFF_KR:IN_COLS]
    w_in_ext = jnp.concatenate(
        [w_in[i][:, :OFF_KR], jnp.zeros((D_MODEL, QK_NOPE), F32), w_kr, _swap_halves(w_kr)], axis=1)
    wq3 = w_uq[i].reshape(Q_LORA, N_HEADS, QK_DIM)
    w_q = jnp.concatenate(
        [wq3, _swap_halves(wq3[..., QK_NOPE:])], axis=-1).reshape(Q_LORA, D_HEADS_PAD)
    wkv3 = w_ukv[i].reshape(KV_LORA, N_HEADS, QK_NOPE + V_DIM)
    w_uk = wkv3[..., :QK_NOPE]
    w_uv = wkv3[..., QK_NOPE:]
    zpad = jnp.zeros((KV_LORA, N_HEADS, HEAD_PAD - QK_NOPE), F32)
    w_k = jnp.concatenate([w_uk, zpad], axis=-1).reshape(KV_LORA, D_HEADS_PAD)
    w_vt = w_uv.reshape(KV_LORA, D_ATTN).T
    w_abs = jnp.concatenate([w_uk, zpad], axis=-1).transpose(1, 2, 0)
    head_of_col = jnp.arange(D_ATTN) // V_DIM
    w_uv_heads = jnp.where(head_of_col[None, None, :] == jnp.arange(N_HEADS)[:, None, None],
                           w_uv.reshape(1, KV_LORA, D_ATTN), 0.0)
    p_sel = jnp.zeros((HEAD_PAD, QK_ROPE), F32).at[QK_NOPE:QK_DIM, :].set(jnp.eye(QK_ROPE, dtype=F32))
    return dict(
        g_mix_pre=row(g_mix_pre), w_in=w_in_ext.astype(BF16), w_conv=w_conv[i].astype(F32),
        g_q=row(g_q), w_q=w_q.astype(BF16), g_kv=row(g_kv), w_k=w_k.astype(BF16), w_vt=w_vt.astype(BF16),
        g_conv_out=row(g_conv_out), g_attn=row(g_attn_out),
        w_abs=w_abs.astype(BF16), w_uv=w_uv_heads.astype(BF16), p_sel=p_sel.astype(BF16),
        w_o_conv=w_o[i][:D_CONV].astype(BF16), w_o_attn=w_o[i][D_CONV:].astype(BF16),
        g_mix_post=row(g_mix_post), g_ffn_pre=row(g_ffn_pre),
        w_gate=w_ffn_gate[i].astype(BF16), w_up=w_ffn_up[i].astype(BF16), w_down=w_ffn_down[i].astype(BF16),
        g_ffn_post=row(g_ffn_post), w_ple_gate=w_ple_gate[i].astype(BF16), w_ple_proj=w_ple_proj[i].astype(BF16),
    )


def _rope_tables(pos):
    half = QK_ROPE // 2
    inv = ROPE_THETA ** (-jnp.arange(half, dtype=F32) / half)
    ang = pos.astype(F32)[:, None] * inv[None, :]
    cos, sin = jnp.cos(ang), jnp.sin(ang)
    n = pos.shape[0]
    cc = jnp.concatenate([cos, cos], axis=1)
    ss = jnp.concatenate([-sin, sin], axis=1)
    z_lo = jnp.zeros((n, QK_NOPE), F32)
    z_hi = jnp.zeros((n, HEAD_PAD - QK_DIM), F32)
    c_q = jnp.concatenate([jnp.ones((n, QK_NOPE), F32), cc, z_hi], axis=1) * Q_PRESCALE
    s_q = jnp.concatenate([z_lo, ss, z_hi], axis=1)
    c_k = jnp.concatenate([z_lo, cc, z_hi], axis=1)
    return jnp.stack([c_q, s_q * Q_PRESCALE, c_k, s_q])


def _pick_tile(n, cap):
    t = min(n, cap)
    while n % t:
        t //= 2
    return t


def kernel(x_prompt, x_sample, cache_kv_latent, cache_k_rope, state_conv, p_prompt, p_sample, g_mix_pre, w_in, w_conv, g_q, w_uq, g_kv, w_ukv, g_conv_out, g_attn_out, w_o, g_mix_post, g_ffn_pre, w_ffn_gate, w_ffn_up, w_ffn_down, g_ffn_post, w_ple_proj, w_ple_gate):
    depth = w_in.shape[0]
    n_p, s_p, _ = x_prompt.shape
    n_d, s_d, _ = x_sample.shape
    past_len = cache_kv_latent.shape[2]
    tab_p = _rope_tables(jnp.arange(s_p))
    tab_d = _rope_tables(past_len + jnp.arange(s_d))
    conv_zero = jnp.zeros((1, n_p, CONV_W - 1, D_CONV), F32)
    ts_p = _pick_tile(s_p, SEQ_TILE)
    tq = _pick_tile(s_p, Q_TILE)
    tm_p = _pick_tile(n_p * s_p, TOKEN_TILE)
    tm_d = _pick_tile(n_d * s_d, TOKEN_TILE)
    n_pairs = s_p // (Q_TILES * tq)
    assert tm_p == Q_TILES * tq

    def attn_tile_index(t):
        b, p = t // n_pairs, t % n_pairs
        second = p // (n_pairs // 2)
        return b, second, jnp.where(second == 1, n_pairs - 1 - p, p), 0

    pp_flat = p_prompt.reshape(depth, n_p * s_p, D_PLE)
    pd_flat = p_sample.reshape(depth, n_d * s_d, D_PLE)

    xp, xd = x_prompt, x_sample
    stack_p = stack_d = None
    for i in range(depth):
        lw = _layer_weights(i, g_mix_pre, w_in, w_conv, g_q, w_uq, g_kv, w_ukv, g_conv_out,
                            g_attn_out, w_o, g_mix_post, g_ffn_pre, w_ffn_gate, w_ffn_up,
                            w_ffn_down, g_ffn_post, w_ple_proj, w_ple_gate)
        convn, q, *stack_p, k, vt = _in_proj(xp, tab_p, conv_zero, 0, lw, i, depth, ts_p,
                                             tk=Q_TILES * tq, stacked=stack_p)
        attn = _attn_prompt(q, k, vt, lw["g_attn"], tq)
        xp = _out_ffn(xp.reshape(n_p * s_p, D_MODEL), convn.reshape(n_p * s_p, D_CONV),
                      attn, attn_tile_index, pp_flat, i, lw, tm_p).reshape(n_p, s_p, D_MODEL)
        convn, q, *stack_d = _in_proj(xd, tab_d, state_conv, i, lw, i, depth, s_d, stacked=stack_d)
        attn = _attn_sample(q, cache_kv_latent, cache_k_rope, stack_d[0], stack_d[1], i, lw)
        xd = _out_ffn(xd.reshape(n_d * s_d, D_MODEL), convn.reshape(n_d * s_d, D_CONV),
                      attn.reshape(1, 1, 1, n_d * s_d, D_ATTN), lambda t: (0, 0, 0, t), pd_flat, i, lw,
                      tm_d).reshape(n_d, s_d, D_MODEL)

    return (xp, xd, *stack_p, *stack_d)
```

```python
import functools

import jax
import jax.numpy as jnp
from jax import lax
from jax.experimental import pallas as pl
from jax.experimental.pallas import tpu as pltpu

D_MODEL = 1024
CHUNK = 64
D_PLE = 256
D_CONV = 512
CONV_W = 3
V_DIM = 64
QK_NOPE = 64
QK_ROPE = 32
QK_DIM = QK_NOPE + QK_ROPE
N_HEADS = 8
D_ATTN = N_HEADS * V_DIM
Q_LORA = 768
KV_LORA = 256
D_FF = 2816
ROPE_THETA = 10000.0
EPS = 1e-6
SCALE = QK_DIM ** -0.5

OFF_C = D_CONV
OFF_X = 2 * D_CONV
OFF_Q = 3 * D_CONV
OFF_KV = OFF_Q + Q_LORA
OFF_KR = OFF_KV + KV_LORA
IN_COLS = OFF_KR + QK_ROPE

LANES = 128
SUBLANES = 8
HEAD_PAD = LANES
D_HEADS_PAD = N_HEADS * HEAD_PAD
IN_EXT = OFF_KR + LANES
FF_CHUNK = 256
NEG_BIG = -1e30
Q_PRESCALE = SCALE * 1.4426950408889634
Q_TILES = 2
SCORE_LEAD = 3
ROW_GROUP = 256
SUM_ROWS = 16
SEQ_TILE = 512
Q_TILE = 256
TOKEN_TILE = 512
VMEM_LIMIT = 48 * 1024 * 1024
VMEM_LIMIT_FFN = 56 * 1024 * 1024

BF16 = jnp.bfloat16
F32 = jnp.float32


def _rms(x, g):
    ms = jnp.mean(x * x, axis=-1, keepdims=True)
    return x * lax.rsqrt(ms + EPS) * g


def _dot(a, b):
    return jnp.dot(a, b, preferred_element_type=F32)


def _dot_nt(a, b):
    return lax.dot_general(a, b, (((1,), (1,)), ((), ())), preferred_element_type=F32)


def _const_spec(shape):
    nd = len(shape)
    return pl.BlockSpec(shape, lambda *_: (0,) * nd, pipeline_mode=pl.Buffered(1))


def _in_proj_kernel(x_ref, tab_ref, cinit_ref, gpre_ref, win_ref, wconv_ref, gq_ref, wq_ref,
                    gkv_ref, gconv_ref, *rest, with_kv, n_alias, layer):
    if with_kv:
        wk_ref, wvt_ref = rest[:2]
        rest = rest[2:]
    rest = rest[n_alias:]
    if with_kv:
        convn_ref, q_ref, lat_ref, kr_ref, newconv_ref, k_ref, vt_ref, carry_ref = rest
    else:
        convn_ref, q_ref, lat_ref, kr_ref, newconv_ref, carry_ref = rest
    ts = x_ref.shape[1]
    s_idx = pl.program_id(1)

    def store_layered(ref, rows, val):
        if n_alias:
            ref[0, rows, :] = val
        else:
            for d in range(ref.shape[0]):
                ref[d, 0, rows, :] = val if d == layer else jnp.zeros_like(val)

    @pl.when(s_idx == 0)
    def _():
        carry_ref[0:2, :] = cinit_ref[0]

    ng = max(1, ts // ROW_GROUP)
    gs = ts // ng
    groups = [slice(i * gs, (i + 1) * gs) for i in range(ng)]
    z = []
    for r in groups:
        h = _rms(x_ref[0, r, :], gpre_ref[...]).astype(BF16)
        z.append([_dot(h, win_ref[:, a:b]) for a, b in
                  ((0, OFF_C), (OFF_C, OFF_X), (OFF_X, OFF_Q), (OFF_Q, OFF_KV), (OFF_KV, OFF_KR),
                   (OFF_KR, IN_EXT))])
    cat = lambda j: z[0][j] if ng == 1 else jnp.concatenate([zg[j] for zg in z], axis=0)

    assert CONV_W == 3
    gb = cat(0)
    u = cat(1) * cat(2)
    row = lax.broadcasted_iota(jnp.int32, (ts, 1), 0)
    c0 = carry_ref[0:1, :]
    c1 = carry_ref[1:2, :]
    u_m1 = jnp.where(row == 0, c1, pltpu.roll(u, 1, axis=0))
    u_m2 = pltpu.roll(u, 2, axis=0)
    u_m2 = jnp.where(row == 0, c0, jnp.where(row == 1, c1, u_m2))
    conv = u_m2 * wconv_ref[0:1, :] + u_m1 * wconv_ref[1:2, :] + u * wconv_ref[2:3, :]
    convn_ref[0] = _rms(gb * conv, gconv_ref[...]).astype(BF16)
    tail = u[ts - 2:ts, :]
    carry_ref[0:2, :] = tail
    store_layered(newconv_ref, slice(None), tail)

    for gi, r in enumerate(groups):
        c_q = tab_ref[0, r, :]
        s_q = tab_ref[1, r, :]
        c_k = tab_ref[2, r, :]
        s_k = tab_ref[3, r, :]

        cqn = _rms(z[gi][3], gq_ref[...]).astype(BF16)
        qa = _dot(cqn, wq_ref[...])
        for hd in range(N_HEADS):
            blk = qa[:, hd * HEAD_PAD:(hd + 1) * HEAD_PAD]
            rot = blk * c_q + pltpu.roll(blk, HEAD_PAD - QK_ROPE, axis=1) * s_q
            q_ref[0, r, hd * HEAD_PAD:(hd + 1) * HEAD_PAD] = rot.astype(BF16)

        c_kv = _rms(z[gi][4], gkv_ref[...])
        store_layered(lat_ref, r, c_kv)
        ckvb = c_kv.astype(BF16)
        zkr = z[gi][5]
        kr_blk = zkr * c_k + pltpu.roll(zkr, HEAD_PAD - QK_ROPE, axis=1) * s_k
        store_layered(kr_ref, r, kr_blk[:, QK_NOPE:QK_DIM])
        if with_kv:
            kn = _dot(ckvb, wk_ref[...])
            for hd in range(N_HEADS):
                k_ref[0, r, hd * HEAD_PAD:(hd + 1) * HEAD_PAD] = (
                    kn[:, hd * HEAD_PAD:(hd + 1) * HEAD_PAD] + kr_blk).astype(BF16)
            vt = _dot_nt(wvt_ref[...], ckvb).astype(BF16)
            tk = vt_ref.shape[3]
            start = gi * gs
            vt_ref[0, start // tk, :, start % tk:start % tk + gs] = vt


def _in_proj(x, tables, conv_init, conv_layer, lw, layer, depth, ts, tk=None, stacked=None):
    n, s, _ = x.shape
    with_kv = tk is not None
    grid = (n, s // ts)
    tok = lambda w: pl.BlockSpec((1, ts, w), lambda b, i: (b, i, 0))

    def lay(rows, w, tiled):
        slab, first = (None, layer) if stacked is not None else (depth, 0)
        return pl.BlockSpec((slab, 1, rows, w),
                            (lambda b, i: (first, b, i, 0)) if tiled else (lambda b, i: (first, b, 0, 0)))
    out_shape = [
        jax.ShapeDtypeStruct((n, s, D_CONV), BF16),
        jax.ShapeDtypeStruct((n, s, D_HEADS_PAD), BF16),
        jax.ShapeDtypeStruct((depth, n, s, KV_LORA), F32),
        jax.ShapeDtypeStruct((depth, n, s, QK_ROPE), F32),
        jax.ShapeDtypeStruct((depth, n, CONV_W - 1, D_CONV), F32),
    ]
    in_specs = [
        tok(D_MODEL),
        pl.BlockSpec((4, ts, LANES), lambda b, i: (0, i, 0)),
        pl.BlockSpec((None, 1, CONV_W - 1, D_CONV), lambda b, i: (conv_layer, b, 0, 0)),
        _const_spec((1, D_MODEL)),
        _const_spec((D_MODEL, IN_EXT)),
        _const_spec((CONV_W, D_CONV)),
        _const_spec((1, Q_LORA)),
        _const_spec((Q_LORA, D_HEADS_PAD)),
        _const_spec((1, KV_LORA)),
        _const_spec((1, D_CONV)),
    ]
    out_specs = [
        tok(D_CONV), tok(D_HEADS_PAD), lay(ts, KV_LORA, True), lay(ts, QK_ROPE, True),
        lay(CONV_W - 1, D_CONV, False),
    ]
    args = [x, tables, conv_init, lw["g_mix_pre"], lw["w_in"], lw["w_conv"], lw["g_q"], lw["w_q"],
            lw["g_kv"], lw["g_conv_out"]]
    if with_kv:
        in_specs += [_const_spec((KV_LORA, D_HEADS_PAD)), _const_spec((D_ATTN, KV_LORA))]
        args += [lw["w_k"], lw["w_vt"]]
        out_shape += [jax.ShapeDtypeStruct((n, s, D_HEADS_PAD), BF16),
                      jax.ShapeDtypeStruct((n, s // tk, D_ATTN, tk), BF16)]
        out_specs += [tok(D_HEADS_PAD),
                      pl.BlockSpec((1, ts // tk, D_ATTN, tk), lambda b, i: (b, i, 0, 0))]
    aliases = {}
    if stacked is not None:
        for j, arr in enumerate(stacked):
            aliases[len(args)] = 2 + j
            in_specs.append(pl.BlockSpec(memory_space=pl.ANY))
            args.append(arr)
    return pl.pallas_call(
        functools.partial(_in_proj_kernel, with_kv=with_kv, n_alias=len(aliases), layer=layer),
        grid=grid,
        in_specs=in_specs,
        out_specs=tuple(out_specs),
        out_shape=tuple(out_shape),
        input_output_aliases=aliases,
        scratch_shapes=[pltpu.VMEM((SUBLANES, D_CONV), F32)],
        compiler_params=pltpu.CompilerParams(
            dimension_semantics=("arbitrary", "arbitrary"),
            vmem_limit_bytes=VMEM_LIMIT),
        name="in_proj_kv" if with_kv else "in_proj",
    )(*args)


def _attn_prompt_kernel(qa_ref, qb_ref, k_ref, vt_ref, g_ref, o_ref, acc_ref, m_ref, ot_ref, ahead_ref,
                        *, tq, n_pairs):
    tk = vt_ref.shape[3]
    step = pl.program_id(1)
    pairs = (step, n_pairs - 1 - step)
    q_refs = (qa_ref, qb_ref)
    m_ref[...] = jnp.full(m_ref.shape, NEG_BIG, F32)
    acc_ref[...] = jnp.zeros(acc_ref.shape, F32)
    items = [(t, hd) for hd in range(N_HEADS) for t in range(Q_TILES)]

    def scores(slot, t, kb, hd, width):
        cols = slice(hd * HEAD_PAD, (hd + 1) * HEAD_PAD)
        k0 = pl.multiple_of(kb * tk, tk)
        return _dot_nt(k_ref[0, pl.ds(k0, width), cols],
                       q_refs[slot][0, t * tq:(t + 1) * tq, cols])

    def visible(t, width):
        ck = lax.broadcasted_iota(jnp.int32, (width, tq), 0) // CHUNK
        rq = (lax.broadcasted_iota(jnp.int32, (width, tq), 1) + t * tq) // CHUNK
        return ck <= rq

    def softmax(slot, t, hd, st, vis):
        if vis is not None:
            st = jnp.where(vis, st, NEG_BIG)
        m_prev = m_ref[slot, t, hd:hd + 1, :]
        m_new = jnp.maximum(m_prev, jnp.max(st, axis=0, keepdims=True))
        m_ref[slot, t, hd:hd + 1, :] = m_new
        return jnp.exp2(m_prev - m_new), jnp.exp2(st - m_new).astype(BF16)

    def weighted_values(slot, t, hd, kb, alpha, pt):
        width = pt.shape[0]
        lhs = jnp.concatenate([vt_ref[0, kb, hd * V_DIM:(hd + 1) * V_DIM, 0:width],
                               jnp.ones((SUM_ROWS, width), BF16)], axis=0)
        acc_ref[slot, t, hd] = alpha * acc_ref[slot, t, hd] + _dot(lhs, pt)

    def block(slot, kb, widths, masked, nxt):
        sts = {n: ahead_ref[slot, n, 0:widths[items[n][0]], :] for n in range(SCORE_LEAD)}
        vis = [visible(t, widths[t]) if masked else None for t in range(Q_TILES)]
        for n, (t, hd) in enumerate(items):
            lead = n + SCORE_LEAD
            if lead < len(items):
                sts[lead] = scores(slot, items[lead][0], kb, items[lead][1], widths[items[lead][0]])
            elif nxt is not None:
                tn, hn = items[lead - len(items)]
                ahead_ref[nxt[0], lead - len(items)] = scores(nxt[0], tn, nxt[1], hn, tk)
            weighted_values(slot, t, hd, kb, *softmax(slot, t, hd, sts.pop(n), vis[t]))

    def write_out(slot):
        for t in range(Q_TILES):
            ssq = jnp.zeros((1, tq), F32)
            for hd in range(N_HEADS):
                a = acc_ref[slot, t, hd]
                o = a[0:V_DIM, :] / a[V_DIM:V_DIM + 1, :]
                ssq = ssq + jnp.sum(o * o, axis=0, keepdims=True)
                ot_ref[slot, t, hd * V_DIM:(hd + 1) * V_DIM, :] = o
            rs = lax.rsqrt(ssq * (1.0 / D_ATTN) + EPS)
            o_ref[0, slot, 0, t * tq:(t + 1) * tq, :] = (
                (ot_ref[slot, t] * rs).T * g_ref[...]).astype(BF16)

    for n in range(SCORE_LEAD):
        ahead_ref[0, n] = scores(0, items[n][0], 0, items[n][1], tk)
    for slot in range(2):
        lax.fori_loop(
            0, pairs[slot],
            lambda kb, c, slot=slot: (block(slot, kb, (tk, tk), False, (slot, kb + 1)), c)[1], 0)
        if slot == 1:
            write_out(0)
        block(slot, pairs[slot], (tq, tk), True, (1, 0) if slot == 0 else None)
    write_out(1)


def _attn_prompt(q, k, vt, g_attn, tq):
    n, s, _ = q.shape
    tk = vt.shape[3]
    n_pairs = s // tk
    assert tk == Q_TILES * tq and s % tk == 0 and n_pairs % 2 == 0
    return pl.pallas_call(
        functools.partial(_attn_prompt_kernel, tq=tq, n_pairs=n_pairs),
        grid=(n, n_pairs // 2),
        in_specs=[
            pl.BlockSpec((1, tk, D_HEADS_PAD), lambda b, i: (b, i, 0)),
            pl.BlockSpec((1, tk, D_HEADS_PAD), lambda b, i: (b, n_pairs - 1 - i, 0)),
            pl.BlockSpec((1, s, D_HEADS_PAD), lambda b, i: (b, 0, 0)),
            pl.BlockSpec((1, n_pairs, D_ATTN, tk), lambda b, i: (b, 0, 0, 0)),
            _const_spec((1, D_ATTN)),
        ],
        out_specs=pl.BlockSpec((1, 2, 1, tk, D_ATTN), lambda b, i: (b, 0, i, 0, 0)),
        out_shape=jax.ShapeDtypeStruct((n, 2, n_pairs // 2, tk, D_ATTN), BF16),
        scratch_shapes=[pltpu.VMEM((2, Q_TILES, N_HEADS, V_DIM + SUM_ROWS, tq), F32),
                        pltpu.VMEM((2, Q_TILES, N_HEADS, tq), F32),
                        pltpu.VMEM((2, Q_TILES, D_ATTN, tq), F32),
                        pltpu.VMEM((2, SCORE_LEAD, tk, tq), F32)],
        compiler_params=pltpu.CompilerParams(
            dimension_semantics=("arbitrary", "arbitrary"),
            vmem_limit_bytes=VMEM_LIMIT),
        name="attn_prompt",
    )(q, q, k, vt, g_attn)


def _attn_sample_kernel(q_ref, cpast_ref, krpast_ref, cnew_ref, krnew_ref, wabs_ref, psel_ref,
                        wuv_ref, g_ref, o_ref):
    l_new = q_ref.shape[1]
    q = q_ref[0]
    ql, qr = [], []
    for hd in range(N_HEADS):
        qh = q[:, hd * HEAD_PAD:(hd + 1) * HEAD_PAD]
        ql.append(_dot(qh, wabs_ref[hd]))
        qr.append(_dot(qh, psel_ref[...]))
    ql = jnp.concatenate(ql, axis=0).astype(BF16)
    qr = jnp.concatenate(qr, axis=0).astype(BF16)
    cp = cpast_ref[0].astype(BF16)
    cn = cnew_ref[0].astype(BF16)
    s_p = _dot_nt(ql, cp) + _dot_nt(qr, krpast_ref[0].astype(BF16))
    s_n = _dot_nt(ql, cn) + _dot_nt(qr, krnew_ref[0].astype(BF16))
    m = jnp.maximum(jnp.max(s_p, axis=-1, keepdims=True), jnp.max(s_n, axis=-1, keepdims=True))
    p_p = jnp.exp2(s_p - m)
    p_n = jnp.exp2(s_n - m)
    l = jnp.sum(p_p, axis=-1, keepdims=True) + jnp.sum(p_n, axis=-1, keepdims=True)
    o_lat = ((_dot(p_p.astype(BF16), cp) + _dot(p_n.astype(BF16), cn)) / l).astype(BF16)
    o = _dot(o_lat[0:l_new, :], wuv_ref[0])
    for hd in range(1, N_HEADS):
        o = o + _dot(o_lat[hd * l_new:(hd + 1) * l_new, :], wuv_ref[hd])
    o_ref[0] = _rms(o, g_ref[...]).astype(BF16)


def _attn_sample(q, c_past, kr_past, c_new, kr_new, layer, lw):
    n, l_new, _ = q.shape
    p_len = c_past.shape[2]
    per_b = lambda rows, w: pl.BlockSpec((1, rows, w), lambda b: (b, 0, 0))
    lay = lambda rows, w: pl.BlockSpec((None, 1, rows, w), lambda b: (layer, b, 0, 0))
    return pl.pallas_call(
        _attn_sample_kernel,
        grid=(n,),
        in_specs=[
            per_b(l_new, D_HEADS_PAD),
            lay(p_len, KV_LORA),
            lay(p_len, QK_ROPE),
            lay(l_new, KV_LORA),
            lay(l_new, QK_ROPE),
            _const_spec((N_HEADS, HEAD_PAD, KV_LORA)),
            _const_spec((HEAD_PAD, QK_ROPE)),
            _const_spec((N_HEADS, KV_LORA, D_ATTN)),
            _const_spec((1, D_ATTN)),
        ],
        out_specs=per_b(l_new, D_ATTN),
        out_shape=jax.ShapeDtypeStruct((n, l_new, D_ATTN), BF16),
        compiler_params=pltpu.CompilerParams(
            dimension_semantics=("arbitrary",),
            vmem_limit_bytes=VMEM_LIMIT),
        name="attn_sample",
    )(q, c_past, kr_past, c_new, kr_new, lw["w_abs"], lw["p_sel"], lw["w_uv"], lw["g_attn"])


def _out_ffn_kernel(x_ref, convn_ref, attn_ref, p_ref, woc_ref, woa_ref, gpost_ref, gffn_ref,
                    wg_ref, wu_ref, wd_ref, gfpost_ref, wpg_ref, wpp_ref, y_ref, act_ref):
    tm = x_ref.shape[0]
    ng = max(1, tm // ROW_GROUP)
    groups = [slice(i * (tm // ng), (i + 1) * (tm // ng)) for i in range(ng)]
    mix = [_dot(convn_ref[r, :], woc_ref[...]) + _dot(attn_ref[r, :], woa_ref[...]) for r in groups]
    x1, h = [], []
    for r, mx in zip(groups, mix):
        x1.append(x_ref[r, :] + _rms(mx, gpost_ref[...]))
        h.append(_rms(x1[-1], gffn_ref[...]).astype(BF16))
    for r, hr in zip(groups, h):
        for c in range(D_FF // FF_CHUNK):
            cols = slice(c * FF_CHUNK, (c + 1) * FF_CHUNK)
            g = _dot(hr, wg_ref[:, cols])
            u = _dot(hr, wu_ref[:, cols])
            act_ref[r, cols] = (g * jax.nn.sigmoid(g) * u).astype(BF16)
    f = [_dot(act_ref[r, :], wd_ref[...]) for r in groups]
    for r, x1r, fr in zip(groups, x1, f):
        pp = _dot(p_ref[r, :].astype(BF16), wpp_ref[...])
        x2 = x1r + _rms(fr, gfpost_ref[...])
        gate = jax.nn.sigmoid(_dot(x2.astype(BF16), wpg_ref[...]))
        y_ref[r, :] = x2 + gate * pp


def _out_ffn(x, convn, attn, attn_index, p, layer, lw, tm):
    m = x.shape[0]
    tok = lambda w: pl.BlockSpec((tm, w), lambda i: (i, 0))
    return pl.pallas_call(
        _out_ffn_kernel,
        grid=(m // tm,),
        in_specs=[
            tok(D_MODEL), tok(D_CONV),
            pl.BlockSpec((None, None, None, tm, D_ATTN), lambda i: (*attn_index(i), 0)),
            pl.BlockSpec((None, tm, D_PLE), lambda i: (layer, i, 0)),
            _const_spec((D_CONV, D_MODEL)),
            _const_spec((D_ATTN, D_MODEL)),
            _const_spec((1, D_MODEL)),
            _const_spec((1, D_MODEL)),
            _const_spec((D_MODEL, D_FF)),
            _const_spec((D_MODEL, D_FF)),
            _const_spec((D_FF, D_MODEL)),
            _const_spec((1, D_MODEL)),
            _const_spec((D_MODEL, D_MODEL)),
            _const_spec((D_PLE, D_MODEL)),
        ],
        out_specs=tok(D_MODEL),
        out_shape=jax.ShapeDtypeStruct((m, D_MODEL), F32),
        scratch_shapes=[pltpu.VMEM((tm, D_FF), BF16)],
        compiler_params=pltpu.CompilerParams(
            dimension_semantics=("arbitrary",),
            vmem_limit_bytes=VMEM_LIMIT_FFN),
        name="out_ffn",
    )(x, convn, attn, p, lw["w_o_conv"], lw["w_o_attn"], lw["g_mix_post"], lw["g_ffn_pre"],
      lw["w_gate"], lw["w_up"], lw["w_down"], lw["g_ffn_post"], lw["w_ple_gate"], lw["w_ple_proj"])


def _swap_halves(a):
    half = a.shape[-1] // 2
    return jnp.concatenate([a[..., half:], a[..., :half]], axis=-1)


def _layer_weights(i, g_mix_pre, w_in, w_conv, g_q, w_uq, g_kv, w_ukv, g_conv_out, g_attn_out, w_o,
                   g_mix_post, g_ffn_pre, w_ffn_gate, w_ffn_up, w_ffn_down, g_ffn_post,
                   w_ple_proj, w_ple_gate):
    row = lambda g: g[i].reshape(1, -1).astype(F32)
    w_kr = w_in[i][:, OFF_KR:IN_COLS]
    w_in_ext = jnp.concatenate(
        [w_in[i][:, :OFF_KR], jnp.zeros((D_MODEL, QK_NOPE), F32), w_kr, _swap_halves(w_kr)], axis=1)
    wq3 = w_uq[i].reshape(Q_LORA, N_HEADS, QK_DIM)
    w_q = jnp.concatenate(
        [wq3, _swap_halves(wq3[..., QK_NOPE:])], axis=-1).reshape(Q_LORA, D_HEADS_PAD)
    wkv3 = w_ukv[i].reshape(KV_LORA, N_HEADS, QK_NOPE + V_DIM)
    w_uk = wkv3[..., :QK_NOPE]
    w_uv = wkv3[..., QK_NOPE:]
    zpad = jnp.zeros((KV_LORA, N_HEADS, HEAD_PAD - QK_NOPE), F32)
    w_k = jnp.concatenate([w_uk, zpad], axis=-1).reshape(KV_LORA, D_HEADS_PAD)
    w_vt = w_uv.reshape(KV_LORA, D_ATTN).T
    w_abs = jnp.concatenate([w_uk, zpad], axis=-1).transpose(1, 2, 0)
    head_of_col = jnp.arange(D_ATTN) // V_DIM
    w_uv_heads = jnp.where(head_of_col[None, None, :] == jnp.arange(N_HEADS)[:, None, None],
                           w_uv.reshape(1, KV_LORA, D_ATTN), 0.0)
    p_sel = jnp.zeros((HEAD_PAD, QK_ROPE), F32).at[QK_NOPE:QK_DIM, :].set(jnp.eye(QK_ROPE, dtype=F32))
    return dict(
        g_mix_pre=row(g_mix_pre), w_in=w_in_ext.astype(BF16), w_conv=w_conv[i].astype(F32),
        g_q=row(g_q), w_q=w_q.astype(BF16), g_kv=row(g_kv), w_k=w_k.astype(BF16), w_vt=w_vt.astype(BF16),
        g_conv_out=row(g_conv_out), g_attn=row(g_attn_out),
        w_abs=w_abs.astype(BF16), w_uv=w_uv_heads.astype(BF16), p_sel=p_sel.astype(BF16),
        w_o_conv=w_o[i][:D_CONV].astype(BF16), w_o_attn=w_o[i][D_CONV:].astype(BF16),
        g_mix_post=row(g_mix_post), g_ffn_pre=row(g_ffn_pre),
        w_gate=w_ffn_gate[i].astype(BF16), w_up=w_ffn_up[i].astype(BF16), w_down=w_ffn_down[i].astype(BF16),
        g_ffn_post=row(g_ffn_post), w_ple_gate=w_ple_gate[i].astype(BF16), w_ple_proj=w_ple_proj[i].astype(BF16),
    )


def _rope_tables(pos):
    half = QK_ROPE // 2
    inv = ROPE_THETA ** (-jnp.arange(half, dtype=F32) / half)
    ang = pos.astype(F32)[:, None] * inv[None, :]
    cos, sin = jnp.cos(ang), jnp.sin(ang)
    n = pos.shape[0]
    cc = jnp.concatenate([cos, cos], axis=1)
    ss = jnp.concatenate([-sin, sin], axis=1)
    z_lo = jnp.zeros((n, QK_NOPE), F32)
    z_hi = jnp.zeros((n, HEAD_PAD - QK_DIM), F32)
    c_q = jnp.concatenate([jnp.ones((n, QK_NOPE), F32), cc, z_hi], axis=1) * Q_PRESCALE
    s_q = jnp.concatenate([z_lo, ss, z_hi], axis=1)
    c_k = jnp.concatenate([z_lo, cc, z_hi], axis=1)
    return jnp.stack([c_q, s_q * Q_PRESCALE, c_k, s_q])


def _pick_tile(n, cap):
    t = min(n, cap)
    while n % t:
        t //= 2
    return t


def kernel(x_prompt, x_sample, cache_kv_latent, cache_k_rope, state_conv, p_prompt, p_sample, g_mix_pre, w_in, w_conv, g_q, w_uq, g_kv, w_ukv, g_conv_out, g_attn_out, w_o, g_mix_post, g_ffn_pre, w_ffn_gate, w_ffn_up, w_ffn_down, g_ffn_post, w_ple_proj, w_ple_gate):
    depth = w_in.shape[0]
    n_p, s_p, _ = x_prompt.shape
    n_d, s_d, _ = x_sample.shape
    past_len = cache_kv_latent.shape[2]
    tab_p = _rope_tables(jnp.arange(s_p))
    tab_d = _rope_tables(past_len + jnp.arange(s_d))
    conv_zero = jnp.zeros((1, n_p, CONV_W - 1, D_CONV), F32)
    ts_p = _pick_tile(s_p, SEQ_TILE)
    tq = _pick_tile(s_p, Q_TILE)
    tm_p = _pick_tile(n_p * s_p, TOKEN_TILE)
    tm_d = _pick_tile(n_d * s_d, TOKEN_TILE)
    n_pairs = s_p // (Q_TILES * tq)
    assert tm_p == Q_TILES * tq

    def attn_tile_index(t):
        b, p = t // n_pairs, t % n_pairs
        second = p // (n_pairs // 2)
        return b, second, jnp.where(second == 1, n_pairs - 1 - p, p), 0

    pp_flat = p_prompt.reshape(depth, n_p * s_p, D_PLE)
    pd_flat = p_sample.reshape(depth, n_d * s_d, D_PLE)

    xp, xd = x_prompt, x_sample
    stack_p = stack_d = None
    for i in range(depth):
        lw = _layer_weights(i, g_mix_pre, w_in, w_conv, g_q, w_uq, g_kv, w_ukv, g_conv_out,
                            g_attn_out, w_o, g_mix_post, g_ffn_pre, w_ffn_gate, w_ffn_up,
                            w_ffn_down, g_ffn_post, w_ple_proj, w_ple_gate)
        convn, q, *stack_p, k, vt = _in_proj(xp, tab_p, conv_zero, 0, lw, i, depth, ts_p,
                                             tk=Q_TILES * tq, stacked=stack_p)
        attn = _attn_prompt(q, k, vt, lw["g_attn"], tq)
        xp = _out_ffn(xp.reshape(n_p * s_p, D_MODEL), convn.reshape(n_p * s_p, D_CONV),
                      attn, attn_tile_index, pp_flat, i, lw, tm_p).reshape(n_p, s_p, D_MODEL)
        convn, q, *stack_d = _in_proj(xd, tab_d, state_conv, i, lw, i, depth, s_d, stacked=stack_d)
        attn = _attn_sample(q, cache_kv_latent, cache_k_rope, stack_d[0], stack_d[1], i, lw)
        xd = _out_ffn(xd.reshape(n_d * s_d, D_MODEL), convn.reshape(n_d * s_d, D_CONV),
                      attn.reshape(1, 1, 1, n_d * s_d, D_ATTN), lambda t: (0, 0, 0, t), pd_flat, i, lw,
                      tm_d).reshape(n_d, s_d, D_MODEL)

    return (xp, xd, *stack_p, *stack_d)
```

```python
import functools

import jax
import jax.numpy as jnp
from jax import lax
from jax.experimental import pallas as pl
from jax.experimental.pallas import tpu as pltpu

D_MODEL = 1024
CHUNK = 64
D_PLE = 256
D_CONV = 512
CONV_W = 3
V_DIM = 64
QK_NOPE = 64
QK_ROPE = 32
QK_DIM = QK_NOPE + QK_ROPE
N_HEADS = 8
D_ATTN = N_HEADS * V_DIM
Q_LORA = 768
KV_LORA = 256
D_FF = 2816
ROPE_THETA = 10000.0
EPS = 1e-6
SCALE = QK_DIM ** -0.5

OFF_C = D_CONV
OFF_X = 2 * D_CONV
OFF_Q = 3 * D_CONV
OFF_KV = OFF_Q + Q_LORA
OFF_KR = OFF_KV + KV_LORA
IN_COLS = OFF_KR + QK_ROPE

LANES = 128
SUBLANES = 8
HEAD_PAD = LANES
D_HEADS_PAD = N_HEADS * HEAD_PAD
IN_EXT = OFF_KR + LANES
FF_CHUNK = 256
NEG_BIG = -1e30
Q_PRESCALE = SCALE * 1.4426950408889634
Q_TILES = 2
SCORE_LEAD = 3
ROW_GROUP = 256
SUM_ROWS = 16
SEQ_TILE = 512
Q_TILE = 256
TOKEN_TILE = 512
VMEM_LIMIT = 48 * 1024 * 1024
VMEM_LIMIT_FFN = 56 * 1024 * 1024

BF16 = jnp.bfloat16
F32 = jnp.float32


def _rms(x, g):
    ms = jnp.mean(x * x, axis=-1, keepdims=True)
    return x * lax.rsqrt(ms + EPS) * g


def _dot(a, b):
    return jnp.dot(a, b, preferred_element_type=F32)


def _dot_nt(a, b):
    return lax.dot_general(a, b, (((1,), (1,)), ((), ())), preferred_element_type=F32)


def _const_spec(shape):
    nd = len(shape)
    return pl.BlockSpec(shape, lambda *_: (0,) * nd, pipeline_mode=pl.Buffered(1))


def _in_proj_kernel(x_ref, tab_ref, cinit_ref, gpre_ref, win_ref, wconv_ref, gq_ref, wq_ref,
                    gkv_ref, gconv_ref, *rest, with_kv, n_alias, layer):
    if with_kv:
        wk_ref, wvt_ref = rest[:2]
        rest = rest[2:]
    rest = rest[n_alias:]
    if with_kv:
        convn_ref, q_ref, lat_ref, kr_ref, newconv_ref, k_ref, vt_ref, carry_ref = rest
    else:
        convn_ref, q_ref, lat_ref, kr_ref, newconv_ref, carry_ref = rest
    ts = x_ref.shape[1]
    s_idx = pl.program_id(1)

    def store_layered(ref, rows, val):
        if n_alias:
            ref[0, rows, :] = val
        else:
            for d in range(ref.shape[0]):
                ref[d, 0, rows, :] = val if d == layer else jnp.zeros_like(val)

    @pl.when(s_idx == 0)
    def _():
        carry_ref[0:2, :] = cinit_ref[0]

    ng = max(1, ts // ROW_GROUP)
    gs = ts // ng
    groups = [slice(i * gs, (i + 1) * gs) for i in range(ng)]
    z = []
    for r in groups:
        h = _rms(x_ref[0, r, :], gpre_ref[...]).astype(BF16)
        z.append([_dot(h, win_ref[:, a:b]) for a, b in
                  ((0, OFF_C), (OFF_C, OFF_X), (OFF_X, OFF_Q), (OFF_Q, OFF_KV), (OFF_KV, OFF_KR),
                   (OFF_KR, IN_EXT))])
    cat = lambda j: z[0][j] if ng == 1 else jnp.concatenate([zg[j] for zg in z], axis=0)

    assert CONV_W == 3
    gb = cat(0)
    u = cat(1) * cat(2)
    row = lax.broadcasted_iota(jnp.int32, (ts, 1), 0)
    c0 = carry_ref[0:1, :]
    c1 = carry_ref[1:2, :]
    u_m1 = jnp.where(row == 0, c1, pltpu.roll(u, 1, axis=0))
    u_m2 = pltpu.roll(u, 2, axis=0)
    u_m2 = jnp.where(row == 0, c0, jnp.where(row == 1, c1, u_m2))
    conv = u_m2 * wconv_ref[0:1, :] + u_m1 * wconv_ref[1:2, :] + u * wconv_ref[2:3, :]
    convn_ref[0] = _rms(gb * conv, gconv_ref[...]).astype(BF16)
    tail = u[ts - 2:ts, :]
    carry_ref[0:2, :] = tail
    store_layered(newconv_ref, slice(None), tail)

    for gi, r in enumerate(groups):
        c_q = tab_ref[0, r, :]
        s_q = tab_ref[1, r, :]
        c_k = tab_ref[2, r, :]
        s_k = tab_ref[3, r, :]

        cqn = _rms(z[gi][3], gq_ref[...]).astype(BF16)
        qa = _dot(cqn, wq_ref[...])
        for hd in range(N_HEADS):
            blk = qa[:, hd * HEAD_PAD:(hd + 1) * HEAD_PAD]
            rot = blk * c_q + pltpu.roll(blk, HEAD_PAD - QK_ROPE, axis=1) * s_q
            q_ref[0, r, hd * HEAD_PAD:(hd + 1) * HEAD_PAD] = rot.astype(BF16)

        c_kv = _rms(z[gi][4], gkv_ref[...])
        store_layered(lat_ref, r, c_kv)
        ckvb = c_kv.astype(BF16)
        zkr = z[gi][5]
        kr_blk = zkr * c_k + pltpu.roll(zkr, HEAD_PAD - QK_ROPE, axis=1) * s_k
        store_layered(kr_ref, r, kr_blk[:, QK_NOPE:QK_DIM])
        if with_kv:
            kn = _dot(ckvb, wk_ref[...])
            for hd in range(N_HEADS):
                k_ref[0, r, hd * HEAD_PAD:(hd + 1) * HEAD_PAD] = (
                    kn[:, hd * HEAD_PAD:(hd + 1) * HEAD_PAD] + kr_blk).astype(BF16)
            vt = _dot_nt(wvt_ref[...], ckvb).astype(BF16)
            tk = vt_ref.shape[3]
            start = gi * gs
            vt_ref[0, start // tk, :, start % tk:start % tk + gs] = vt


def _in_proj(x, tables, conv_init, conv_layer, lw, layer, depth, ts, tk=None, stacked=None):
    n, s, _ = x.shape
    with_kv = tk is not None
    grid = (n, s // ts)
    tok = lambda w: pl.BlockSpec((1, ts, w), lambda b, i: (b, i, 0))

    def lay(rows, w, tiled):
        slab, first = (None, layer) if stacked is not None else (depth, 0)
        return pl.BlockSpec((slab, 1, rows, w),
                            (lambda b, i: (first, b, i, 0)) if tiled else (lambda b, i: (first, b, 0, 0)))
    out_shape = [
        jax.ShapeDtypeStruct((n, s, D_CONV), BF16),
        jax.ShapeDtypeStruct((n, s, D_HEADS_PAD), BF16),
        jax.ShapeDtypeStruct((depth, n, s, KV_LORA), F32),
        jax.ShapeDtypeStruct((depth, n, s, QK_ROPE), F32),
        jax.ShapeDtypeStruct((depth, n, CONV_W - 1, D_CONV), F32),
    ]
    in_specs = [
        tok(D_MODEL),
        pl.BlockSpec((4, ts, LANES), lambda b, i: (0, i, 0)),
        pl.BlockSpec((None, 1, CONV_W - 1, D_CONV), lambda b, i: (conv_layer, b, 0, 0)),
        _const_spec((1, D_MODEL)),
        _const_spec((D_MODEL, IN_EXT)),
        _const_spec((CONV_W, D_CONV)),
        _const_spec((1, Q_LORA)),
        _const_spec((Q_LORA, D_HEADS_PAD)),
        _const_spec((1, KV_LORA)),
        _const_spec((1, D_CONV)),
    ]
    out_specs = [
        tok(D_CONV), tok(D_HEADS_PAD), lay(ts, KV_LORA, True), lay(ts, QK_ROPE, True),
        lay(CONV_W - 1, D_CONV, False),
    ]
    args = [x, tables, conv_init, lw["g_mix_pre"], lw["w_in"], lw["w_conv"], lw["g_q"], lw["w_q"],
            lw["g_kv"], lw["g_conv_out"]]
    if with_kv:
        in_specs += [_const_spec((KV_LORA, D_HEADS_PAD)), _const_spec((D_ATTN, KV_LORA))]
        args += [lw["w_k"], lw["w_vt"]]
        out_shape += [jax.ShapeDtypeStruct((n, s, D_HEADS_PAD), BF16),
                      jax.ShapeDtypeStruct((n, s // tk, D_ATTN, tk), BF16)]
        out_specs += [tok(D_HEADS_PAD),
                      pl.BlockSpec((1, ts // tk, D_ATTN, tk), lambda b, i: (b, i, 0, 0))]
    aliases = {}
    if stacked is not None:
        for j, arr in enumerate(stacked):
            aliases[len(args)] = 2 + j
            in_specs.append(pl.BlockSpec(memory_space=pl.ANY))
            args.append(arr)
    return pl.pallas_call(
        functools.partial(_in_proj_kernel, with_kv=with_kv, n_alias=len(aliases), layer=layer),
        grid=grid,
        in_specs=in_specs,
        out_specs=tuple(out_specs),
        out_shape=tuple(out_shape),
        input_output_aliases=aliases,
        scratch_shapes=[pltpu.VMEM((SUBLANES, D_CONV), F32)],
        compiler_params=pltpu.CompilerParams(
            dimension_semantics=("arbitrary", "arbitrary"),
            vmem_limit_bytes=VMEM_LIMIT),
        name="in_proj_kv" if with_kv else "in_proj",
    )(*args)


def _attn_prompt_kernel(qa_ref, qb_ref, k_ref, vt_ref, g_ref, o_ref, acc_ref, m_ref, ot_ref, ahead_ref,
                        *, tq, n_pairs):
    tk = vt_ref.shape[3]
    step = pl.program_id(1)
    pairs = (step, n_pairs - 1 - step)
    q_refs = (qa_ref, qb_ref)
    m_ref[...] = jnp.full(m_ref.shape, NEG_BIG, F32)
    acc_ref[...] = jnp.zeros(acc_ref.shape, F32)
    items = [(t, hd) for hd in range(N_HEADS) for t in range(Q_TILES)]

    def scores(slot, t, kb, hd, width):
        cols = slice(hd * HEAD_PAD, (hd + 1) * HEAD_PAD)
        k0 = pl.multiple_of(kb * tk, tk)
        return _dot_nt(k_ref[0, pl.ds(k0, width), cols],
                       q_refs[slot][0, t * tq:(t + 1) * tq, cols])

    def visible(t, width):
        ck = lax.broadcasted_iota(jnp.int32, (width, tq), 0) // CHUNK
        rq = (lax.broadcasted_iota(jnp.int32, (width, tq), 1) + t * tq) // CHUNK
        return ck <= rq

    def softmax(slot, t, hd, st, vis):
        if vis is not None:
            st = jnp.where(vis, st, NEG_BIG)
        m_prev = m_ref[slot, t, hd:hd + 1, :]
        m_new = jnp.maximum(m_prev, jnp.max(st, axis=0, keepdims=True))
        m_ref[slot, t, hd:hd + 1, :] = m_new
        return jnp.exp2(m_prev - m_new), jnp.exp2(st - m_new).astype(BF16)

    def weighted_values(slot, t, hd, kb, alpha, pt):
        width = pt.shape[0]
        lhs = jnp.concatenate([vt_ref[0, kb, hd * V_DIM:(hd + 1) * V_DIM, 0:width],
                               jnp.ones((SUM_ROWS, width), BF16)], axis=0)
        acc_ref[slot, t, hd] = alpha * acc_ref[slot, t, hd] + _dot(lhs, pt)

    def block(slot, kb, widths, masked, nxt):
        sts = {n: ahead_ref[slot, n, 0:widths[items[n][0]], :] for n in range(SCORE_LEAD)}
        vis = [visible(t, widths[t]) if masked else None for t in range(Q_TILES)]
        for n, (t, hd) in enumerate(items):
            lead = n + SCORE_LEAD
            if lead < len(items):
                sts[lead] = scores(slot, items[lead][0], kb, items[lead][1], widths[items[lead][0]])
            elif nxt is not None:
                tn, hn = items[lead - len(items)]
                ahead_ref[nxt[0], lead - len(items)] = scores(nxt[0], tn, nxt[1], hn, tk)
            weighted_values(slot, t, hd, kb, *softmax(slot, t, hd, sts.pop(n), vis[t]))

    def write_out(slot):
        for t in range(Q_TILES):
            ssq = jnp.zeros((1, tq), F32)
            for hd in range(N_HEADS):
                a = acc_ref[slot, t, hd]
                o = a[0:V_DIM, :] / a[V_DIM:V_DIM + 1, :]
                ssq = ssq + jnp.sum(o * o, axis=0, keepdims=True)
                ot_ref[slot, t, hd * V_DIM:(hd + 1) * V_DIM, :] = o
            rs = lax.rsqrt(ssq * (1.0 / D_ATTN) + EPS)
            o_ref[0, slot, 0, t * tq:(t + 1) * tq, :] = (
                (ot_ref[slot, t] * rs).T * g_ref[...]).astype(BF16)

    for n in range(SCORE_LEAD):
        ahead_ref[0, n] = scores(0, items[n][0], 0, items[n][1], tk)
    for slot in range(2):
        lax.fori_loop(
            0, pairs[slot],
            lambda kb, c, slot=slot: (block(slot, kb, (tk, tk), False, (slot, kb + 1)), c)[1], 0)
        if slot == 1:
            write_out(0)
        block(slot, pairs[slot], (tq, tk), True, (1, 0) if slot == 0 else None)
    write_out(1)


def _attn_prompt(q, k, vt, g_attn, tq):
    n, s, _ = q.shape
    tk = vt.shape[3]
    n_pairs = s // tk
    assert tk == Q_TILES * tq and s % tk == 0 and n_pairs % 2 == 0
    return pl.pallas_call(
        functools.partial(_attn_prompt_kernel, tq=tq, n_pairs=n_pairs),
        grid=(n, n_pairs // 2),
        in_specs=[
            pl.BlockSpec((1, tk, D_HEADS_PAD), lambda b, i: (b, i, 0)),
            pl.BlockSpec((1, tk, D_HEADS_PAD), lambda b, i: (b, n_pairs - 1 - i, 0)),
            pl.BlockSpec((1, s, D_HEADS_PAD), lambda b, i: (b, 0, 0)),
            pl.BlockSpec((1, n_pairs, D_ATTN, tk), lambda b, i: (b, 0, 0, 0)),
            _const_spec((1, D_ATTN)),
        ],
        out_specs=pl.BlockSpec((1, 2, 1, tk, D_ATTN), lambda b, i: (b, 0, i, 0, 0)),
        out_shape=jax.ShapeDtypeStruct((n, 2, n_pairs // 2, tk, D_ATTN), BF16),
        scratch_shapes=[pltpu.VMEM((2, Q_TILES, N_HEADS, V_DIM + SUM_ROWS, tq), F32),
                        pltpu.VMEM((2, Q_TILES, N_HEADS, tq), F32),
                        pltpu.VMEM((2, Q_TILES, D_ATTN, tq), F32),
                        pltpu.VMEM((2, SCORE_LEAD, tk, tq), F32)],
        compiler_params=pltpu.CompilerParams(
            dimension_semantics=("arbitrary", "arbitrary"),
            vmem_limit_bytes=VMEM_LIMIT),
        name="attn_prompt",
    )(q, q, k, vt, g_attn)


def _attn_sample_kernel(q_ref, cpast_ref, krpast_ref, cnew_ref, krnew_ref, wabs_ref, psel_ref,
                        wuv_ref, g_ref, o_ref):
    l_new = q_ref.shape[1]
    q = q_ref[0]
    ql, qr = [], []
    for hd in range(N_HEADS):
        qh = q[:, hd * HEAD_PAD:(hd + 1) * HEAD_PAD]
        ql.append(_dot(qh, wabs_ref[hd]))
        qr.append(_dot(qh, psel_ref[...]))
    ql = jnp.concatenate(ql, axis=0).astype(BF16)
    qr = jnp.concatenate(qr, axis=0).astype(BF16)
    cp = cpast_ref[0].astype(BF16)
    cn = cnew_ref[0].astype(BF16)
    s_p = _dot_nt(ql, cp) + _dot(qr, krpast_ref[0].astype(BF16))
    s_n = _dot_nt(ql, cn) + _dot_nt(qr, krnew_ref[0].astype(BF16))
    m = jnp.maximum(jnp.max(s_p, axis=-1, keepdims=True), jnp.max(s_n, axis=-1, keepdims=True))
    p_p = jnp.exp2(s_p - m)
    p_n = jnp.exp2(s_n - m)
    l = jnp.sum(p_p, axis=-1, keepdims=True) + jnp.sum(p_n, axis=-1, keepdims=True)
    o_lat = ((_dot(p_p.astype(BF16), cp) + _dot(p_n.astype(BF16), cn)) / l).astype(BF16)
    o = _dot(o_lat[0:l_new, :], wuv_ref[0])
    for hd in range(1, N_HEADS):
        o = o + _dot(o_lat[hd * l_new:(hd + 1) * l_new, :], wuv_ref[hd])
    o_ref[0] = _rms(o, g_ref[...]).astype(BF16)


def _attn_sample(q, c_past, kr_past, c_new, kr_new, layer, lw):
    n, l_new, _ = q.shape
    p_len = c_past.shape[2]
    per_b = lambda rows, w: pl.BlockSpec((1, rows, w), lambda b: (b, 0, 0))
    lay = lambda rows, w: pl.BlockSpec((None, 1, rows, w), lambda b: (layer, b, 0, 0))
    return pl.pallas_call(
        _attn_sample_kernel,
        grid=(n,),
        in_specs=[
            per_b(l_new, D_HEADS_PAD),
            lay(p_len, KV_LORA),
            lay(QK_ROPE, p_len),
            lay(l_new, KV_LORA),
            lay(l_new, QK_ROPE),
            _const_spec((N_HEADS, HEAD_PAD, KV_LORA)),
            _const_spec((HEAD_PAD, QK_ROPE)),
            _const_spec((N_HEADS, KV_LORA, D_ATTN)),
            _const_spec((1, D_ATTN)),
        ],
        out_specs=per_b(l_new, D_ATTN),
        out_shape=jax.ShapeDtypeStruct((n, l_new, D_ATTN), BF16),
        compiler_params=pltpu.CompilerParams(
            dimension_semantics=("arbitrary",),
            vmem_limit_bytes=VMEM_LIMIT),
        name="attn_sample",
    )(q, c_past, kr_past, c_new, kr_new, lw["w_abs"], lw["p_sel"], lw["w_uv"], lw["g_attn"])


def _out_ffn_kernel(x_ref, convn_ref, attn_ref, p_ref, woc_ref, woa_ref, gpost_ref, gffn_ref,
                    wg_ref, wu_ref, wd_ref, gfpost_ref, wpg_ref, wpp_ref, y_ref, act_ref):
    tm = x_ref.shape[0]
    ng = max(1, tm // ROW_GROUP)
    groups = [slice(i * (tm // ng), (i + 1) * (tm // ng)) for i in range(ng)]
    mix = [_dot(convn_ref[r, :], woc_ref[...]) + _dot(attn_ref[r, :], woa_ref[...]) for r in groups]
    x1, h = [], []
    for r, mx in zip(groups, mix):
        x1.append(x_ref[r, :] + _rms(mx, gpost_ref[...]))
        h.append(_rms(x1[-1], gffn_ref[...]).astype(BF16))
    for r, hr in zip(groups, h):
        for c in range(D_FF // FF_CHUNK):
            cols = slice(c * FF_CHUNK, (c + 1) * FF_CHUNK)
            g = _dot(hr, wg_ref[:, cols])
            u = _dot(hr, wu_ref[:, cols])
            act_ref[r, cols] = (g * jax.nn.sigmoid(g) * u).astype(BF16)
    f = [_dot(act_ref[r, :], wd_ref[...]) for r in groups]
    for r, x1r, fr in zip(groups, x1, f):
        pp = _dot(p_ref[r, :].astype(BF16), wpp_ref[...])
        x2 = x1r + _rms(fr, gfpost_ref[...])
        gate = jax.nn.sigmoid(_dot(x2.astype(BF16), wpg_ref[...]))
        y_ref[r, :] = x2 + gate * pp


def _out_ffn(x, convn, attn, attn_index, p, layer, lw, tm):
    m = x.shape[0]
    tok = lambda w: pl.BlockSpec((tm, w), lambda i: (i, 0))
    return pl.pallas_call(
        _out_ffn_kernel,
        grid=(m // tm,),
        in_specs=[
            tok(D_MODEL), tok(D_CONV),
            pl.BlockSpec((None, None, None, tm, D_ATTN), lambda i: (*attn_index(i), 0)),
            pl.BlockSpec((None, tm, D_PLE), lambda i: (layer, i, 0)),
            _const_spec((D_CONV, D_MODEL)),
            _const_spec((D_ATTN, D_MODEL)),
            _const_spec((1, D_MODEL)),
            _const_spec((1, D_MODEL)),
            _const_spec((D_MODEL, D_FF)),
            _const_spec((D_MODEL, D_FF)),
            _const_spec((D_FF, D_MODEL)),
            _const_spec((1, D_MODEL)),
            _const_spec((D_MODEL, D_MODEL)),
            _const_spec((D_PLE, D_MODEL)),
        ],
        out_specs=tok(D_MODEL),
        out_shape=jax.ShapeDtypeStruct((m, D_MODEL), F32),
        scratch_shapes=[pltpu.VMEM((tm, D_FF), BF16)],
        compiler_params=pltpu.CompilerParams(
            dimension_semantics=("arbitrary",),
            vmem_limit_bytes=VMEM_LIMIT_FFN),
        name="out_ffn",
    )(x, convn, attn, p, lw["w_o_conv"], lw["w_o_attn"], lw["g_mix_post"], lw["g_ffn_pre"],
      lw["w_gate"], lw["w_up"], lw["w_down"], lw["g_ffn_post"], lw["w_ple_gate"], lw["w_ple_proj"])


def _swap_halves(a):
    half = a.shape[-1] // 2
    return jnp.concatenate([a[..., half:], a[..., :half]], axis=-1)


def _layer_weights(i, g_mix_pre, w_in, w_conv, g_q, w_uq, g_kv, w_ukv, g_conv_out, g_attn_out, w_o,
                   g_mix_post, g_ffn_pre, w_ffn_gate, w_ffn_up, w_ffn_down, g_ffn_post,
                   w_ple_proj, w_ple_gate):
    row = lambda g: g[i].reshape(1, -1).astype(F32)
    w_kr = w_in[i][:, OFF_KR:IN_COLS]
    w_in_ext = jnp.concatenate(
        [w_in[i][:, :OFF_KR], jnp.zeros((D_MODEL, QK_NOPE), F32), w_kr, _swap_halves(w_kr)], axis=1)
    wq3 = w_uq[i].reshape(Q_LORA, N_HEADS, QK_DIM)
    w_q = jnp.concatenate(
        [wq3, _swap_halves(wq3[..., QK_NOPE:])], axis=-1).reshape(Q_LORA, D_HEADS_PAD)
    wkv3 = w_ukv[i].reshape(KV_LORA, N_HEADS, QK_NOPE + V_DIM)
    w_uk = wkv3[..., :QK_NOPE]
    w_uv = wkv3[..., QK_NOPE:]
    zpad = jnp.zeros((KV_LORA, N_HEADS, HEAD_PAD - QK_NOPE), F32)
    w_k = jnp.concatenate([w_uk, zpad], axis=-1).reshape(KV_LORA, D_HEADS_PAD)
    w_vt = w_uv.reshape(KV_LORA, D_ATTN).T
    w_abs = jnp.concatenate([w_uk, zpad], axis=-1).transpose(1, 2, 0)
    head_of_col = jnp.arange(D_ATTN) // V_DIM
    w_uv_heads = jnp.where(head_of_col[None, None, :] == jnp.arange(N_HEADS)[:, None, None],
                           w_uv.reshape(1, KV_LORA, D_ATTN), 0.0)
    p_sel = jnp.zeros((HEAD_PAD, QK_ROPE), F32).at[QK_NOPE:QK_DIM, :].set(jnp.eye(QK_ROPE, dtype=F32))
    return dict(
        g_mix_pre=row(g_mix_pre), w_in=w_in_ext.astype(BF16), w_conv=w_conv[i].astype(F32),
        g_q=row(g_q), w_q=w_q.astype(BF16), g_kv=row(g_kv), w_k=w_k.astype(BF16), w_vt=w_vt.astype(BF16),
        g_conv_out=row(g_conv_out), g_attn=row(g_attn_out),
        w_abs=w_abs.astype(BF16), w_uv=w_uv_heads.astype(BF16), p_sel=p_sel.astype(BF16),
        w_o_conv=w_o[i][:D_CONV].astype(BF16), w_o_attn=w_o[i][D_CONV:].astype(BF16),
        g_mix_post=row(g_mix_post), g_ffn_pre=row(g_ffn_pre),
        w_gate=w_ffn_gate[i].astype(BF16), w_up=w_ffn_up[i].astype(BF16), w_down=w_ffn_down[i].astype(BF16),
        g_ffn_post=row(g_ffn_post), w_ple_gate=w_ple_gate[i].astype(BF16), w_ple_proj=w_ple_proj[i].astype(BF16),
    )


def _rope_tables(pos):
    half = QK_ROPE // 2
    inv = ROPE_THETA ** (-jnp.arange(half, dtype=F32) / half)
    ang = pos.astype(F32)[:, None] * inv[None, :]
    cos, sin = jnp.cos(ang), jnp.sin(ang)
    n = pos.shape[0]
    cc = jnp.concatenate([cos, cos], axis=1)
    ss = jnp.concatenate([-sin, sin], axis=1)
    z_lo = jnp.zeros((n, QK_NOPE), F32)
    z_hi = jnp.zeros((n, HEAD_PAD - QK_DIM), F32)
    c_q = jnp.concatenate([jnp.ones((n, QK_NOPE), F32), cc, z_hi], axis=1) * Q_PRESCALE
    s_q = jnp.concatenate([z_lo, ss, z_hi], axis=1)
    c_k = jnp.concatenate([z_lo, cc, z_hi], axis=1)
    return jnp.stack([c_q, s_q * Q_PRESCALE, c_k, s_q])


def _pick_tile(n, cap):
    t = min(n, cap)
    while n % t:
        t //= 2
    return t


def kernel(x_prompt, x_sample, cache_kv_latent, cache_k_rope, state_conv, p_prompt, p_sample, g_mix_pre, w_in, w_conv, g_q, w_uq, g_kv, w_ukv, g_conv_out, g_attn_out, w_o, g_mix_post, g_ffn_pre, w_ffn_gate, w_ffn_up, w_ffn_down, g_ffn_post, w_ple_proj, w_ple_gate):
    depth = w_in.shape[0]
    n_p, s_p, _ = x_prompt.shape
    n_d, s_d, _ = x_sample.shape
    past_len = cache_kv_latent.shape[2]
    tab_p = _rope_tables(jnp.arange(s_p))
    tab_d = _rope_tables(past_len + jnp.arange(s_d))
    conv_zero = jnp.zeros((1, n_p, CONV_W - 1, D_CONV), F32)
    ts_p = _pick_tile(s_p, SEQ_TILE)
    tq = _pick_tile(s_p, Q_TILE)
    tm_p = _pick_tile(n_p * s_p, TOKEN_TILE)
    tm_d = _pick_tile(n_d * s_d, TOKEN_TILE)
    n_pairs = s_p // (Q_TILES * tq)
    assert tm_p == Q_TILES * tq

    def attn_tile_index(t):
        b, p = t // n_pairs, t % n_pairs
        second = p // (n_pairs // 2)
        return b, second, jnp.where(second == 1, n_pairs - 1 - p, p), 0

    pp_flat = p_prompt.reshape(depth, n_p * s_p, D_PLE)
    pd_flat = p_sample.reshape(depth, n_d * s_d, D_PLE)

    cache_kr_t = jnp.swapaxes(cache_k_rope, 2, 3)
    xp, xd = x_prompt, x_sample
    stack_p = stack_d = None
    for i in range(depth):
        lw = _layer_weights(i, g_mix_pre, w_in, w_conv, g_q, w_uq, g_kv, w_ukv, g_conv_out,
                            g_attn_out, w_o, g_mix_post, g_ffn_pre, w_ffn_gate, w_ffn_up,
                            w_ffn_down, g_ffn_post, w_ple_proj, w_ple_gate)
        convn, q, *stack_p, k, vt = _in_proj(xp, tab_p, conv_zero, 0, lw, i, depth, ts_p,
                                             tk=Q_TILES * tq, stacked=stack_p)
        attn = _attn_prompt(q, k, vt, lw["g_attn"], tq)
        xp = _out_ffn(xp.reshape(n_p * s_p, D_MODEL), convn.reshape(n_p * s_p, D_CONV),
                      attn, attn_tile_index, pp_flat, i, lw, tm_p).reshape(n_p, s_p, D_MODEL)
        convn, q, *stack_d = _in_proj(xd, tab_d, state_conv, i, lw, i, depth, s_d, stacked=stack_d)
        attn = _attn_sample(q, cache_kv_latent, cache_kr_t, stack_d[0], stack_d[1], i, lw)
        xd = _out_ffn(xd.reshape(n_d * s_d, D_MODEL), convn.reshape(n_d * s_d, D_CONV),
                      attn.reshape(1, 1, 1, n_d * s_d, D_ATTN), lambda t: (0, 0, 0, t), pd_flat, i, lw,
                      tm_d).reshape(n_d, s_d, D_MODEL)

    return (xp, xd, *stack_p, *stack_d)
```

```python
import functools

import jax
import jax.numpy as jnp
from jax import lax
from jax.experimental import pallas as pl
from jax.experimental.pallas import tpu as pltpu

D_MODEL = 1024
CHUNK = 64
D_PLE = 256
D_CONV = 512
CONV_W = 3
V_DIM = 64
QK_NOPE = 64
QK_ROPE = 32
QK_DIM = QK_NOPE + QK_ROPE
N_HEADS = 8
D_ATTN = N_HEADS * V_DIM
Q_LORA = 768
KV_LORA = 256
D_FF = 2816
ROPE_THETA = 10000.0
EPS = 1e-6
SCALE = QK_DIM ** -0.5

OFF_C = D_CONV
OFF_X = 2 * D_CONV
OFF_Q = 3 * D_CONV
OFF_KV = OFF_Q + Q_LORA
OFF_KR = OFF_KV + KV_LORA
IN_COLS = OFF_KR + QK_ROPE

LANES = 128
SUBLANES = 8
HEAD_PAD = LANES
D_HEADS_PAD = N_HEADS * HEAD_PAD
IN_EXT = OFF_KR + LANES
FF_CHUNK = 256
NEG_BIG = -1e30
Q_PRESCALE = SCALE * 1.4426950408889634
Q_TILES = 2
SCORE_LEAD = 3
ROW_GROUP = 256
SUM_ROWS = 16
SEQ_TILE = 512
Q_TILE = 256
TOKEN_TILE = 512
VMEM_LIMIT = 48 * 1024 * 1024
VMEM_LIMIT_FFN = 56 * 1024 * 1024

BF16 = jnp.bfloat16
F32 = jnp.float32


def _rms(x, g):
    ms = jnp.mean(x * x, axis=-1, keepdims=True)
    return x * lax.rsqrt(ms + EPS) * g


def _dot(a, b):
    return jnp.dot(a, b, preferred_element_type=F32)


def _dot_nt(a, b):
    return lax.dot_general(a, b, (((1,), (1,)), ((), ())), preferred_element_type=F32)


def _const_spec(shape):
    nd = len(shape)
    return pl.BlockSpec(shape, lambda *_: (0,) * nd, pipeline_mode=pl.Buffered(1))


def _in_proj_kernel(x_ref, tab_ref, cinit_ref, gpre_ref, win_ref, wconv_ref, gq_ref, wq_ref,
                    gkv_ref, gconv_ref, *rest, with_kv, n_alias, layer):
    if with_kv:
        wk_ref, wvt_ref = rest[:2]
        rest = rest[2:]
    rest = rest[n_alias:]
    if with_kv:
        convn_ref, q_ref, lat_ref, kr_ref, newconv_ref, k_ref, vt_ref, carry_ref = rest
    else:
        convn_ref, q_ref, lat_ref, kr_ref, newconv_ref, carry_ref = rest
    ts = x_ref.shape[1]
    s_idx = pl.program_id(1)

    def store_layered(ref, rows, val):
        if n_alias:
            ref[0, rows, :] = val
        else:
            for d in range(ref.shape[0]):
                ref[d, 0, rows, :] = val if d == layer else jnp.zeros_like(val)

    @pl.when(s_idx == 0)
    def _():
        carry_ref[0:2, :] = cinit_ref[0]

    ng = max(1, ts // ROW_GROUP)
    gs = ts // ng
    groups = [slice(i * gs, (i + 1) * gs) for i in range(ng)]
    z = []
    for r in groups:
        h = _rms(x_ref[0, r, :], gpre_ref[...]).astype(BF16)
        z.append([_dot(h, win_ref[:, a:b]) for a, b in
                  ((0, OFF_C), (OFF_C, OFF_X), (OFF_X, OFF_Q), (OFF_Q, OFF_KV), (OFF_KV, OFF_KR),
                   (OFF_KR, IN_EXT))])
    cat = lambda j: z[0][j] if ng == 1 else jnp.concatenate([zg[j] for zg in z], axis=0)

    assert CONV_W == 3
    gb = cat(0)
    u = cat(1) * cat(2)
    row = lax.broadcasted_iota(jnp.int32, (ts, 1), 0)
    c0 = carry_ref[0:1, :]
    c1 = carry_ref[1:2, :]
    u_m1 = jnp.where(row == 0, c1, pltpu.roll(u, 1, axis=0))
    u_m2 = pltpu.roll(u, 2, axis=0)
    u_m2 = jnp.where(row == 0, c0, jnp.where(row == 1, c1, u_m2))
    conv = u_m2 * wconv_ref[0:1, :] + u_m1 * wconv_ref[1:2, :] + u * wconv_ref[2:3, :]
    convn_ref[0] = _rms(gb * conv, gconv_ref[...]).astype(BF16)
    tail = u[ts - 2:ts, :]
    carry_ref[0:2, :] = tail
    store_layered(newconv_ref, slice(None), tail)

    for gi, r in enumerate(groups):
        c_q = tab_ref[0, r, :]
        s_q = tab_ref[1, r, :]
        c_k = tab_ref[2, r, :]
        s_k = tab_ref[3, r, :]

        cqn = _rms(z[gi][3], gq_ref[...]).astype(BF16)
        qa = _dot(cqn, wq_ref[...])
        for hd in range(N_HEADS):
            blk = qa[:, hd * HEAD_PAD:(hd + 1) * HEAD_PAD]
            rot = blk * c_q + pltpu.roll(blk, HEAD_PAD - QK_ROPE, axis=1) * s_q
            q_ref[0, r, hd * HEAD_PAD:(hd + 1) * HEAD_PAD] = rot.astype(BF16)

        c_kv = _rms(z[gi][4], gkv_ref[...])
        store_layered(lat_ref, r, c_kv)
        ckvb = c_kv.astype(BF16)
        zkr = z[gi][5]
        kr_blk = zkr * c_k + pltpu.roll(zkr, HEAD_PAD - QK_ROPE, axis=1) * s_k
        store_layered(kr_ref, r, kr_blk[:, QK_NOPE:QK_DIM])
        if with_kv:
            kn = _dot(ckvb, wk_ref[...])
            for hd in range(N_HEADS):
                k_ref[0, r, hd * HEAD_PAD:(hd + 1) * HEAD_PAD] = (
                    kn[:, hd * HEAD_PAD:(hd + 1) * HEAD_PAD] + kr_blk).astype(BF16)
            vt = _dot_nt(wvt_ref[...], ckvb).astype(BF16)
            tk = vt_ref.shape[3]
            start = gi * gs
            vt_ref[0, start // tk, :, start % tk:start % tk + gs] = vt


def _in_proj(x, tables, conv_init, conv_layer, lw, layer, depth, ts, tk=None, stacked=None):
    n, s, _ = x.shape
    with_kv = tk is not None
    grid = (n, s // ts)
    tok = lambda w: pl.BlockSpec((1, ts, w), lambda b, i: (b, i, 0))

    def lay(rows, w, tiled):
        slab, first = (None, layer) if stacked is not None else (depth, 0)
        return pl.BlockSpec((slab, 1, rows, w),
                            (lambda b, i: (first, b, i, 0)) if tiled else (lambda b, i: (first, b, 0, 0)))
    out_shape = [
        jax.ShapeDtypeStruct((n, s, D_CONV), BF16),
        jax.ShapeDtypeStruct((n, s, D_HEADS_PAD), BF16),
        jax.ShapeDtypeStruct((depth, n, s, KV_LORA), F32),
        jax.ShapeDtypeStruct((depth, n, s, QK_ROPE), F32),
        jax.ShapeDtypeStruct((depth, n, CONV_W - 1, D_CONV), F32),
    ]
    in_specs = [
        tok(D_MODEL),
        pl.BlockSpec((4, ts, LANES), lambda b, i: (0, i, 0)),
        pl.BlockSpec((None, 1, CONV_W - 1, D_CONV), lambda b, i: (conv_layer, b, 0, 0)),
        _const_spec((1, D_MODEL)),
        _const_spec((D_MODEL, IN_EXT)),
        _const_spec((CONV_W, D_CONV)),
        _const_spec((1, Q_LORA)),
        _const_spec((Q_LORA, D_HEADS_PAD)),
        _const_spec((1, KV_LORA)),
        _const_spec((1, D_CONV)),
    ]
    out_specs = [
        tok(D_CONV), tok(D_HEADS_PAD), lay(ts, KV_LORA, True), lay(ts, QK_ROPE, True),
        lay(CONV_W - 1, D_CONV, False),
    ]
    args = [x, tables, conv_init, lw["g_mix_pre"], lw["w_in"], lw["w_conv"], lw["g_q"], lw["w_q"],
            lw["g_kv"], lw["g_conv_out"]]
    if with_kv:
        in_specs += [_const_spec((KV_LORA, D_HEADS_PAD)), _const_spec((D_ATTN, KV_LORA))]
        args += [lw["w_k"], lw["w_vt"]]
        out_shape += [jax.ShapeDtypeStruct((n, s, D_HEADS_PAD), BF16),
                      jax.ShapeDtypeStruct((n, s // tk, D_ATTN, tk), BF16)]
        out_specs += [tok(D_HEADS_PAD),
                      pl.BlockSpec((1, ts // tk, D_ATTN, tk), lambda b, i: (b, i, 0, 0))]
    aliases = {}
    if stacked is not None:
        for j, arr in enumerate(stacked):
            aliases[len(args)] = 2 + j
            in_specs.append(pl.BlockSpec(memory_space=pl.ANY))
            args.append(arr)
    return pl.pallas_call(
        functools.partial(_in_proj_kernel, with_kv=with_kv, n_alias=len(aliases), layer=layer),
        grid=grid,
        in_specs=in_specs,
        out_specs=tuple(out_specs),
        out_shape=tuple(out_shape),
        input_output_aliases=aliases,
        scratch_shapes=[pltpu.VMEM((SUBLANES, D_CONV), F32)],
        compiler_params=pltpu.CompilerParams(
            dimension_semantics=("arbitrary", "arbitrary"),
            vmem_limit_bytes=VMEM_LIMIT),
        name="in_proj_kv" if with_kv else "in_proj",
    )(*args)


def _attn_prompt_kernel(q_ref, k_ref, vt_ref, g_ref, o_ref, acc_ref, m_ref, ot_ref, ahead_ref, *, tq):
    tk = vt_ref.shape[3]
    n_pairs = vt_ref.shape[1]
    m_ref[...] = jnp.full(m_ref.shape, NEG_BIG, F32)
    acc_ref[...] = jnp.zeros(acc_ref.shape, F32)
    items = [(t, hd) for hd in range(N_HEADS) for t in range(Q_TILES)]

    def scores(slot, t, kb, hd, width):
        cols = slice(hd * HEAD_PAD, (hd + 1) * HEAD_PAD)
        k0 = kb * tk if isinstance(kb, int) else pl.multiple_of(kb * tk, tk)
        q0 = slot * tk + t * tq
        return _dot_nt(k_ref[0, pl.ds(k0, width), cols], q_ref[0, q0:q0 + tq, cols])

    def visible(t, width):
        ck = lax.broadcasted_iota(jnp.int32, (width, tq), 0) // CHUNK
        rq = (lax.broadcasted_iota(jnp.int32, (width, tq), 1) + t * tq) // CHUNK
        return ck <= rq

    def softmax(slot, t, hd, st, vis):
        if vis is not None:
            st = jnp.where(vis, st, NEG_BIG)
        m_prev = m_ref[slot, t, hd:hd + 1, :]
        m_new = jnp.maximum(m_prev, jnp.max(st, axis=0, keepdims=True))
        m_ref[slot, t, hd:hd + 1, :] = m_new
        return jnp.exp2(m_prev - m_new), jnp.exp2(st - m_new).astype(BF16)

    def weighted_values(slot, t, hd, kb, alpha, pt):
        width = pt.shape[0]
        lhs = jnp.concatenate([vt_ref[0, kb, hd * V_DIM:(hd + 1) * V_DIM, 0:width],
                               jnp.ones((SUM_ROWS, width), BF16)], axis=0)
        acc_ref[slot, t, hd] = alpha * acc_ref[slot, t, hd] + _dot(lhs, pt)

    def block(slot, kb, widths, masked, nxt):
        sts = {n: ahead_ref[slot % 2, n, 0:widths[items[n][0]], :] for n in range(SCORE_LEAD)}
        vis = [visible(t, widths[t]) if masked else None for t in range(Q_TILES)]
        for n, (t, hd) in enumerate(items):
            lead = n + SCORE_LEAD
            if lead < len(items):
                sts[lead] = scores(slot, items[lead][0], kb, items[lead][1], widths[items[lead][0]])
            elif nxt is not None:
                tn, hn = items[lead - len(items)]
                ahead_ref[nxt[0] % 2, lead - len(items)] = scores(nxt[0], tn, nxt[1], hn, tk)
            weighted_values(slot, t, hd, kb, *softmax(slot, t, hd, sts.pop(n), vis[t]))

    def write_out(slot):
        for t in range(Q_TILES):
            ssq = jnp.zeros((1, tq), F32)
            for hd in range(N_HEADS):
                a = acc_ref[slot, t, hd]
                o = a[0:V_DIM, :] / a[V_DIM:V_DIM + 1, :]
                ssq = ssq + jnp.sum(o * o, axis=0, keepdims=True)
                ot_ref[slot % 2, t, hd * V_DIM:(hd + 1) * V_DIM, :] = o
            rs = lax.rsqrt(ssq * (1.0 / D_ATTN) + EPS)
            q0 = slot * tk + t * tq
            o_ref[0, q0:q0 + tq, :] = ((ot_ref[slot % 2, t] * rs).T * g_ref[...]).astype(BF16)

    for n in range(SCORE_LEAD):
        ahead_ref[0, n] = scores(0, items[n][0], 0, items[n][1], tk)
    for slot in range(n_pairs):
        lax.fori_loop(
            0, slot,
            lambda kb, c, slot=slot: (block(slot, kb, (tk, tk), False, (slot, kb + 1)), c)[1], 0)
        if slot > 0:
            write_out(slot - 1)
        block(slot, slot, (tq, tk), True, (slot + 1, 0) if slot + 1 < n_pairs else None)
    write_out(n_pairs - 1)


def _attn_prompt(q, k, vt, g_attn, tq):
    n, s, _ = q.shape
    tk = vt.shape[3]
    n_pairs = s // tk
    assert tk == Q_TILES * tq and s % tk == 0
    row = lambda w: pl.BlockSpec((1, s, w), lambda b: (b, 0, 0))
    return pl.pallas_call(
        functools.partial(_attn_prompt_kernel, tq=tq),
        grid=(n,),
        in_specs=[
            row(D_HEADS_PAD),
            row(D_HEADS_PAD),
            pl.BlockSpec((1, n_pairs, D_ATTN, tk), lambda b: (b, 0, 0, 0)),
            _const_spec((1, D_ATTN)),
        ],
        out_specs=row(D_ATTN),
        out_shape=jax.ShapeDtypeStruct((n, s, D_ATTN), BF16),
        scratch_shapes=[pltpu.VMEM((n_pairs, Q_TILES, N_HEADS, V_DIM + SUM_ROWS, tq), F32),
                        pltpu.VMEM((n_pairs, Q_TILES, N_HEADS, tq), F32),
                        pltpu.VMEM((2, Q_TILES, D_ATTN, tq), F32),
                        pltpu.VMEM((2, SCORE_LEAD, tk, tq), F32)],
        compiler_params=pltpu.CompilerParams(
            dimension_semantics=("arbitrary",),
            vmem_limit_bytes=VMEM_LIMIT),
        name="attn_prompt",
    )(q, k, vt, g_attn)


def _attn_sample_kernel(q_ref, cpast_ref, krpast_ref, cnew_ref, krnew_ref, wabs_ref, psel_ref,
                        wuv_ref, g_ref, o_ref):
    l_new = q_ref.shape[1]
    q = q_ref[0]
    ql, qr = [], []
    for hd in range(N_HEADS):
        qh = q[:, hd * HEAD_PAD:(hd + 1) * HEAD_PAD]
        ql.append(_dot(qh, wabs_ref[hd]))
        qr.append(_dot(qh, psel_ref[...]))
    ql = jnp.concatenate(ql, axis=0).astype(BF16)
    qr = jnp.concatenate(qr, axis=0).astype(BF16)
    cp = cpast_ref[0].astype(BF16)
    cn = cnew_ref[0].astype(BF16)
    s_p = _dot_nt(ql, cp) + _dot(qr, krpast_ref[0].astype(BF16))
    s_n = _dot_nt(ql, cn) + _dot_nt(qr, krnew_ref[0].astype(BF16))
    m = jnp.maximum(jnp.max(s_p, axis=-1, keepdims=True), jnp.max(s_n, axis=-1, keepdims=True))
    p_p = jnp.exp2(s_p - m)
    p_n = jnp.exp2(s_n - m)
    l = jnp.sum(p_p, axis=-1, keepdims=True) + jnp.sum(p_n, axis=-1, keepdims=True)
    o_lat = ((_dot(p_p.astype(BF16), cp) + _dot(p_n.astype(BF16), cn)) / l).astype(BF16)
    o = _dot(o_lat[0:l_new, :], wuv_ref[0])
    for hd in range(1, N_HEADS):
        o = o + _dot(o_lat[hd * l_new:(hd + 1) * l_new, :], wuv_ref[hd])
    o_ref[0] = _rms(o, g_ref[...]).astype(BF16)


def _attn_sample(q, c_past, kr_past, c_new, kr_new, layer, lw):
    n, l_new, _ = q.shape
    p_len = c_past.shape[2]
    per_b = lambda rows, w: pl.BlockSpec((1, rows, w), lambda b: (b, 0, 0))
    lay = lambda rows, w: pl.BlockSpec((None, 1, rows, w), lambda b: (layer, b, 0, 0))
    return pl.pallas_call(
        _attn_sample_kernel,
        grid=(n,),
        in_specs=[
            per_b(l_new, D_HEADS_PAD),
            lay(p_len, KV_LORA),
            lay(QK_ROPE, p_len),
            lay(l_new, KV_LORA),
            lay(l_new, QK_ROPE),
            _const_spec((N_HEADS, HEAD_PAD, KV_LORA)),
            _const_spec((HEAD_PAD, QK_ROPE)),
            _const_spec((N_HEADS, KV_LORA, D_ATTN)),
            _const_spec((1, D_ATTN)),
        ],
        out_specs=per_b(l_new, D_ATTN),
        out_shape=jax.ShapeDtypeStruct((n, l_new, D_ATTN), BF16),
        compiler_params=pltpu.CompilerParams(
            dimension_semantics=("arbitrary",),
            vmem_limit_bytes=VMEM_LIMIT),
        name="attn_sample",
    )(q, c_past, kr_past, c_new, kr_new, lw["w_abs"], lw["p_sel"], lw["w_uv"], lw["g_attn"])


def _out_ffn_kernel(x_ref, convn_ref, attn_ref, p_ref, woc_ref, woa_ref, gpost_ref, gffn_ref,
                    wg_ref, wu_ref, wd_ref, gfpost_ref, wpg_ref, wpp_ref, y_ref, act_ref):
    tm = x_ref.shape[0]
    ng = max(1, tm // ROW_GROUP)
    groups = [slice(i * (tm // ng), (i + 1) * (tm // ng)) for i in range(ng)]
    mix = [_dot(convn_ref[r, :], woc_ref[...]) + _dot(attn_ref[r, :], woa_ref[...]) for r in groups]
    x1, h = [], []
    for r, mx in zip(groups, mix):
        x1.append(x_ref[r, :] + _rms(mx, gpost_ref[...]))
        h.append(_rms(x1[-1], gffn_ref[...]).astype(BF16))
    for r, hr in zip(groups, h):
        for c in range(D_FF // FF_CHUNK):
            cols = slice(c * FF_CHUNK, (c + 1) * FF_CHUNK)
            g = _dot(hr, wg_ref[:, cols])
            u = _dot(hr, wu_ref[:, cols])
            act_ref[r, cols] = (g * jax.nn.sigmoid(g) * u).astype(BF16)
    f = [_dot(act_ref[r, :], wd_ref[...]) for r in groups]
    for r, x1r, fr in zip(groups, x1, f):
        pp = _dot(p_ref[r, :].astype(BF16), wpp_ref[...])
        x2 = x1r + _rms(fr, gfpost_ref[...])
        gate = jax.nn.sigmoid(_dot(x2.astype(BF16), wpg_ref[...]))
        y_ref[r, :] = x2 + gate * pp


def _out_ffn(x, convn, attn, p, layer, lw, tm):
    m = x.shape[0]
    tok = lambda w: pl.BlockSpec((tm, w), lambda i: (i, 0))
    return pl.pallas_call(
        _out_ffn_kernel,
        grid=(m // tm,),
        in_specs=[
            tok(D_MODEL), tok(D_CONV), tok(D_ATTN),
            pl.BlockSpec((None, tm, D_PLE), lambda i: (layer, i, 0)),
            _const_spec((D_CONV, D_MODEL)),
            _const_spec((D_ATTN, D_MODEL)),
            _const_spec((1, D_MODEL)),
            _const_spec((1, D_MODEL)),
            _const_spec((D_MODEL, D_FF)),
            _const_spec((D_MODEL, D_FF)),
            _const_spec((D_FF, D_MODEL)),
            _const_spec((1, D_MODEL)),
            _const_spec((D_MODEL, D_MODEL)),
            _const_spec((D_PLE, D_MODEL)),
        ],
        out_specs=tok(D_MODEL),
        out_shape=jax.ShapeDtypeStruct((m, D_MODEL), F32),
        scratch_shapes=[pltpu.VMEM((tm, D_FF), BF16)],
        compiler_params=pltpu.CompilerParams(
            dimension_semantics=("arbitrary",),
            vmem_limit_bytes=VMEM_LIMIT_FFN),
        name="out_ffn",
    )(x, convn, attn, p, lw["w_o_conv"], lw["w_o_attn"], lw["g_mix_post"], lw["g_ffn_pre"],
      lw["w_gate"], lw["w_up"], lw["w_down"], lw["g_ffn_post"], lw["w_ple_gate"], lw["w_ple_proj"])


def _swap_halves(a):
    half = a.shape[-1] // 2
    return jnp.concatenate([a[..., half:], a[..., :half]], axis=-1)


def _layer_weights(i, g_mix_pre, w_in, w_conv, g_q, w_uq, g_kv, w_ukv, g_conv_out, g_attn_out, w_o,
                   g_mix_post, g_ffn_pre, w_ffn_gate, w_ffn_up, w_ffn_down, g_ffn_post,
                   w_ple_proj, w_ple_gate):
    row = lambda g: g[i].reshape(1, -1).astype(F32)
    w_kr = w_in[i][:, OFF_KR:IN_COLS]
    w_in_ext = jnp.concatenate(
        [w_in[i][:, :OFF_KR], jnp.zeros((D_MODEL, QK_NOPE), F32), w_kr, _swap_halves(w_kr)], axis=1)
    wq3 = w_uq[i].reshape(Q_LORA, N_HEADS, QK_DIM)
    w_q = jnp.concatenate(
        [wq3, _swap_halves(wq3[..., QK_NOPE:])], axis=-1).reshape(Q_LORA, D_HEADS_PAD)
    wkv3 = w_ukv[i].reshape(KV_LORA, N_HEADS, QK_NOPE + V_DIM)
    w_uk = wkv3[..., :QK_NOPE]
    w_uv = wkv3[..., QK_NOPE:]
    zpad = jnp.zeros((KV_LORA, N_HEADS, HEAD_PAD - QK_NOPE), F32)
    w_k = jnp.concatenate([w_uk, zpad], axis=-1).reshape(KV_LORA, D_HEADS_PAD)
    w_vt = w_uv.reshape(KV_LORA, D_ATTN).T
    w_abs = jnp.concatenate([w_uk, zpad], axis=-1).transpose(1, 2, 0)
    head_of_col = jnp.arange(D_ATTN) // V_DIM
    w_uv_heads = jnp.where(head_of_col[None, None, :] == jnp.arange(N_HEADS)[:, None, None],
                           w_uv.reshape(1, KV_LORA, D_ATTN), 0.0)
    p_sel = jnp.zeros((HEAD_PAD, QK_ROPE), F32).at[QK_NOPE:QK_DIM, :].set(jnp.eye(QK_ROPE, dtype=F32))
    return dict(
        g_mix_pre=row(g_mix_pre), w_in=w_in_ext.astype(BF16), w_conv=w_conv[i].astype(F32),
        g_q=row(g_q), w_q=w_q.astype(BF16), g_kv=row(g_kv), w_k=w_k.astype(BF16), w_vt=w_vt.astype(BF16),
        g_conv_out=row(g_conv_out), g_attn=row(g_attn_out),
        w_abs=w_abs.astype(BF16), w_uv=w_uv_heads.astype(BF16), p_sel=p_sel.astype(BF16),
        w_o_conv=w_o[i][:D_CONV].astype(BF16), w_o_attn=w_o[i][D_CONV:].astype(BF16),
        g_mix_post=row(g_mix_post), g_ffn_pre=row(g_ffn_pre),
        w_gate=w_ffn_gate[i].astype(BF16), w_up=w_ffn_up[i].astype(BF16), w_down=w_ffn_down[i].astype(BF16),
        g_ffn_post=row(g_ffn_post), w_ple_gate=w_ple_gate[i].astype(BF16), w_ple_proj=w_ple_proj[i].astype(BF16),
    )


def _rope_tables(pos):
    half = QK_ROPE // 2
    inv = ROPE_THETA ** (-jnp.arange(half, dtype=F32) / half)
    ang = pos.astype(F32)[:, None] * inv[None, :]
    cos, sin = jnp.cos(ang), jnp.sin(ang)
    n = pos.shape[0]
    cc = jnp.concatenate([cos, cos], axis=1)
    ss = jnp.concatenate([-sin, sin], axis=1)
    z_lo = jnp.zeros((n, QK_NOPE), F32)
    z_hi = jnp.zeros((n, HEAD_PAD - QK_DIM), F32)
    c_q = jnp.concatenate([jnp.ones((n, QK_NOPE), F32), cc, z_hi], axis=1) * Q_PRESCALE
    s_q = jnp.concatenate([z_lo, ss, z_hi], axis=1)
    c_k = jnp.concatenate([z_lo, cc, z_hi], axis=1)
    return jnp.stack([c_q, s_q * Q_PRESCALE, c_k, s_q])


def _pick_tile(n, cap):
    t = min(n, cap)
    while n % t:
        t //= 2
    return t


def kernel(x_prompt, x_sample, cache_kv_latent, cache_k_rope, state_conv, p_prompt, p_sample, g_mix_pre, w_in, w_conv, g_q, w_uq, g_kv, w_ukv, g_conv_out, g_attn_out, w_o, g_mix_post, g_ffn_pre, w_ffn_gate, w_ffn_up, w_ffn_down, g_ffn_post, w_ple_proj, w_ple_gate):
    depth = w_in.shape[0]
    n_p, s_p, _ = x_prompt.shape
    n_d, s_d, _ = x_sample.shape
    past_len = cache_kv_latent.shape[2]
    tab_p = _rope_tables(jnp.arange(s_p))
    tab_d = _rope_tables(past_len + jnp.arange(s_d))
    conv_zero = jnp.zeros((1, n_p, CONV_W - 1, D_CONV), F32)
    ts_p = _pick_tile(s_p, SEQ_TILE)
    tq = _pick_tile(s_p, Q_TILE)
    tm_p = _pick_tile(n_p * s_p, TOKEN_TILE)
    tm_d = _pick_tile(n_d * s_d, TOKEN_TILE)
    pp_flat = p_prompt.reshape(depth, n_p * s_p, D_PLE)
    pd_flat = p_sample.reshape(depth, n_d * s_d, D_PLE)

    cache_kr_t = jnp.swapaxes(cache_k_rope, 2, 3)
    xp, xd = x_prompt, x_sample
    stack_p = stack_d = None
    for i in range(depth):
        lw = _layer_weights(i, g_mix_pre, w_in, w_conv, g_q, w_uq, g_kv, w_ukv, g_conv_out,
                            g_attn_out, w_o, g_mix_post, g_ffn_pre, w_ffn_gate, w_ffn_up,
                            w_ffn_down, g_ffn_post, w_ple_proj, w_ple_gate)
        convn, q, *stack_p, k, vt = _in_proj(xp, tab_p, conv_zero, 0, lw, i, depth, ts_p,
                                             tk=Q_TILES * tq, stacked=stack_p)
        attn = _attn_prompt(q, k, vt, lw["g_attn"], tq)
        xp = _out_ffn(xp.reshape(n_p * s_p, D_MODEL), convn.reshape(n_p * s_p, D_CONV),
                      attn.reshape(n_p * s_p, D_ATTN), pp_flat, i, lw, tm_p).reshape(n_p, s_p, D_MODEL)
        convn, q, *stack_d = _in_proj(xd, tab_d, state_conv, i, lw, i, depth, s_d, stacked=stack_d)
        attn = _attn_sample(q, cache_kv_latent, cache_kr_t, stack_d[0], stack_d[1], i, lw)
        xd = _out_ffn(xd.reshape(n_d * s_d, D_MODEL), convn.reshape(n_d * s_d, D_CONV),
                      attn.reshape(n_d * s_d, D_ATTN), pd_flat, i, lw, tm_d).reshape(n_d, s_d, D_MODEL)

    return (xp, xd, *stack_p, *stack_d)
```

```python
import functools

import jax
import jax.numpy as jnp
from jax import lax
from jax.experimental import pallas as pl
from jax.experimental.pallas import tpu as pltpu

D_MODEL = 1024
CHUNK = 64
D_PLE = 256
D_CONV = 512
CONV_W = 3
V_DIM = 64
QK_NOPE = 64
QK_ROPE = 32
QK_DIM = QK_NOPE + QK_ROPE
N_HEADS = 8
D_ATTN = N_HEADS * V_DIM
Q_LORA = 768
KV_LORA = 256
D_FF = 2816
ROPE_THETA = 10000.0
EPS = 1e-6
SCALE = QK_DIM ** -0.5

OFF_C = D_CONV
OFF_X = 2 * D_CONV
OFF_Q = 3 * D_CONV
OFF_KV = OFF_Q + Q_LORA
OFF_KR = OFF_KV + KV_LORA
IN_COLS = OFF_KR + QK_ROPE

LANES = 128
SUBLANES = 8
HEAD_PAD = LANES
D_HEADS_PAD = N_HEADS * HEAD_PAD
IN_EXT = OFF_KR + LANES
FF_CHUNK = 256
NEG_BIG = -1e30
Q_PRESCALE = SCALE * 1.4426950408889634
Q_TILES = 2
SCORE_LEAD = 3
ROW_GROUP = 256
SUM_ROWS = 16
SEQ_TILE = 512
Q_TILE = 256
TOKEN_TILE = 512
VMEM_LIMIT = 48 * 1024 * 1024
VMEM_LIMIT_FFN = 56 * 1024 * 1024

BF16 = jnp.bfloat16
F32 = jnp.float32


def _rms(x, g):
    ms = jnp.mean(x * x, axis=-1, keepdims=True)
    return x * lax.rsqrt(ms + EPS) * g


def _dot(a, b):
    return jnp.dot(a, b, preferred_element_type=F32)


def _dot_nt(a, b):
    return lax.dot_general(a, b, (((1,), (1,)), ((), ())), preferred_element_type=F32)


def _const_spec(shape):
    nd = len(shape)
    return pl.BlockSpec(shape, lambda *_: (0,) * nd, pipeline_mode=pl.Buffered(1))


def _in_proj_kernel(x_ref, tab_ref, cinit_ref, gpre_ref, win_ref, wconv_ref, gq_ref, wq_ref,
                    gkv_ref, gconv_ref, *rest, with_kv, n_alias, layer):
    if with_kv:
        wk_ref, wvt_ref = rest[:2]
        rest = rest[2:]
    rest = rest[n_alias:]
    if with_kv:
        convn_ref, q_ref, lat_ref, kr_ref, newconv_ref, k_ref, vt_ref, carry_ref = rest
    else:
        convn_ref, q_ref, lat_ref, kr_ref, newconv_ref, carry_ref = rest
    ts = x_ref.shape[1]
    s_idx = pl.program_id(1)

    def store_layered(ref, rows, val):
        if n_alias:
            ref[0, rows, :] = val
        else:
            for d in range(ref.shape[0]):
                ref[d, 0, rows, :] = val if d == layer else jnp.zeros_like(val)

    @pl.when(s_idx == 0)
    def _():
        carry_ref[0:2, :] = cinit_ref[0]

    ng = max(1, ts // ROW_GROUP)
    gs = ts // ng
    groups = [slice(i * gs, (i + 1) * gs) for i in range(ng)]
    z = []
    for r in groups:
        h = _rms(x_ref[0, r, :], gpre_ref[...]).astype(BF16)
        z.append([_dot(h, win_ref[:, a:b]) for a, b in
                  ((0, OFF_C), (OFF_C, OFF_X), (OFF_X, OFF_Q), (OFF_Q, OFF_KV), (OFF_KV, OFF_KR),
                   (OFF_KR, IN_EXT))])
    cat = lambda j: z[0][j] if ng == 1 else jnp.concatenate([zg[j] for zg in z], axis=0)

    assert CONV_W == 3
    gb = cat(0)
    u = cat(1) * cat(2)
    row = lax.broadcasted_iota(jnp.int32, (ts, 1), 0)
    c0 = carry_ref[0:1, :]
    c1 = carry_ref[1:2, :]
    u_m1 = jnp.where(row == 0, c1, pltpu.roll(u, 1, axis=0))
    u_m2 = pltpu.roll(u, 2, axis=0)
    u_m2 = jnp.where(row == 0, c0, jnp.where(row == 1, c1, u_m2))
    conv = u_m2 * wconv_ref[0:1, :] + u_m1 * wconv_ref[1:2, :] + u * wconv_ref[2:3, :]
    convn_ref[0] = _rms(gb * conv, gconv_ref[...]).astype(BF16)
    tail = u[ts - 2:ts, :]
    carry_ref[0:2, :] = tail
    store_layered(newconv_ref, slice(None), tail)

    for gi, r in enumerate(groups):
        c_q = tab_ref[0, r, :]
        s_q = tab_ref[1, r, :]
        c_k = tab_ref[2, r, :]
        s_k = tab_ref[3, r, :]

        cqn = _rms(z[gi][3], gq_ref[...]).astype(BF16)
        qa = _dot(cqn, wq_ref[...])
        for hd in range(N_HEADS):
            blk = qa[:, hd * HEAD_PAD:(hd + 1) * HEAD_PAD]
            rot = blk * c_q + pltpu.roll(blk, HEAD_PAD - QK_ROPE, axis=1) * s_q
            q_ref[0, r, hd * HEAD_PAD:(hd + 1) * HEAD_PAD] = rot.astype(BF16)

        c_kv = _rms(z[gi][4], gkv_ref[...])
        store_layered(lat_ref, r, c_kv)
        ckvb = c_kv.astype(BF16)
        zkr = z[gi][5]
        kr_blk = zkr * c_k + pltpu.roll(zkr, HEAD_PAD - QK_ROPE, axis=1) * s_k
        store_layered(kr_ref, r, kr_blk[:, QK_NOPE:QK_DIM])
        if with_kv:
            kn = _dot(ckvb, wk_ref[...])
            for hd in range(N_HEADS):
                k_ref[0, r, hd * HEAD_PAD:(hd + 1) * HEAD_PAD] = (
                    kn[:, hd * HEAD_PAD:(hd + 1) * HEAD_PAD] + kr_blk).astype(BF16)
            vt = _dot_nt(wvt_ref[...], ckvb).astype(BF16)
            tk = vt_ref.shape[3]
            start = gi * gs
            vt_ref[0, start // tk, :, start % tk:start % tk + gs] = vt


def _in_proj(x, tables, conv_init, conv_layer, lw, layer, depth, ts, tk=None, stacked=None):
    n, s, _ = x.shape
    with_kv = tk is not None
    grid = (n, s // ts)
    tok = lambda w: pl.BlockSpec((1, ts, w), lambda b, i: (b, i, 0))

    def lay(rows, w, tiled):
        slab, first = (None, layer) if stacked is not None else (depth, 0)
        return pl.BlockSpec((slab, 1, rows, w),
                            (lambda b, i: (first, b, i, 0)) if tiled else (lambda b, i: (first, b, 0, 0)))
    out_shape = [
        jax.ShapeDtypeStruct((n, s, D_CONV), BF16),
        jax.ShapeDtypeStruct((n, s, D_HEADS_PAD), BF16),
        jax.ShapeDtypeStruct((depth, n, s, KV_LORA), F32),
        jax.ShapeDtypeStruct((depth, n, s, QK_ROPE), F32),
        jax.ShapeDtypeStruct((depth, n, CONV_W - 1, D_CONV), F32),
    ]
    in_specs = [
        tok(D_MODEL),
        pl.BlockSpec((4, ts, LANES), lambda b, i: (0, i, 0)),
        pl.BlockSpec((None, 1, CONV_W - 1, D_CONV), lambda b, i: (conv_layer, b, 0, 0)),
        _const_spec((1, D_MODEL)),
        _const_spec((D_MODEL, IN_EXT)),
        _const_spec((CONV_W, D_CONV)),
        _const_spec((1, Q_LORA)),
        _const_spec((Q_LORA, D_HEADS_PAD)),
        _const_spec((1, KV_LORA)),
        _const_spec((1, D_CONV)),
    ]
    out_specs = [
        tok(D_CONV), tok(D_HEADS_PAD), lay(ts, KV_LORA, True), lay(ts, QK_ROPE, True),
        lay(CONV_W - 1, D_CONV, False),
    ]
    args = [x, tables, conv_init, lw["g_mix_pre"], lw["w_in"], lw["w_conv"], lw["g_q"], lw["w_q"],
            lw["g_kv"], lw["g_conv_out"]]
    if with_kv:
        in_specs += [_const_spec((KV_LORA, D_HEADS_PAD)), _const_spec((D_ATTN, KV_LORA))]
        args += [lw["w_k"], lw["w_vt"]]
        out_shape += [jax.ShapeDtypeStruct((n, s, D_HEADS_PAD), BF16),
                      jax.ShapeDtypeStruct((n, s // tk, D_ATTN, tk), BF16)]
        out_specs += [tok(D_HEADS_PAD),
                      pl.BlockSpec((1, ts // tk, D_ATTN, tk), lambda b, i: (b, i, 0, 0))]
    aliases = {}
    if stacked is not None:
        for j, arr in enumerate(stacked):
            aliases[len(args)] = 2 + j
            in_specs.append(pl.BlockSpec(memory_space=pl.ANY))
            args.append(arr)
    return pl.pallas_call(
        functools.partial(_in_proj_kernel, with_kv=with_kv, n_alias=len(aliases), layer=layer),
        grid=grid,
        in_specs=in_specs,
        out_specs=tuple(out_specs),
        out_shape=tuple(out_shape),
        input_output_aliases=aliases,
        scratch_shapes=[pltpu.VMEM((SUBLANES, D_CONV), F32)],
        compiler_params=pltpu.CompilerParams(
            dimension_semantics=("arbitrary", "arbitrary"),
            vmem_limit_bytes=VMEM_LIMIT),
        name="in_proj_kv" if with_kv else "in_proj",
    )(*args)


def _attn_prompt_kernel(q_ref, k_ref, vt_ref, g_ref, o_ref, acc_ref, m_ref, ot_ref, ahead_ref, *, tq):
    tk = vt_ref.shape[3]
    n_pairs = vt_ref.shape[1]
    m_ref[...] = jnp.full(m_ref.shape, NEG_BIG, F32)
    acc_ref[...] = jnp.zeros(acc_ref.shape, F32)
    items = [(t, hd) for hd in range(N_HEADS) for t in range(Q_TILES)]

    def scores(slot, t, kb, hd, width):
        cols = slice(hd * HEAD_PAD, (hd + 1) * HEAD_PAD)
        k0 = kb * tk if isinstance(kb, int) else pl.multiple_of(kb * tk, tk)
        q0 = slot * tk + t * tq
        return _dot_nt(k_ref[0, pl.ds(k0, width), cols], q_ref[0, q0:q0 + tq, cols])

    def visible(t, width):
        ck = lax.broadcasted_iota(jnp.int32, (width, tq), 0) // CHUNK
        rq = (lax.broadcasted_iota(jnp.int32, (width, tq), 1) + t * tq) // CHUNK
        return ck <= rq

    def softmax(slot, t, hd, st, vis):
        if vis is not None:
            st = jnp.where(vis, st, NEG_BIG)
        m_prev = m_ref[slot, t, hd:hd + 1, :]
        m_new = jnp.maximum(m_prev, jnp.max(st, axis=0, keepdims=True))
        m_ref[slot, t, hd:hd + 1, :] = m_new
        return jnp.exp2(m_prev - m_new), jnp.exp2(st - m_new).astype(BF16)

    def weighted_values(slot, t, hd, kb, alpha, pt):
        width = pt.shape[0]
        lhs = jnp.concatenate([vt_ref[0, kb, hd * V_DIM:(hd + 1) * V_DIM, 0:width],
                               jnp.ones((SUM_ROWS, width), BF16)], axis=0)
        acc_ref[slot, t, hd] = alpha * acc_ref[slot, t, hd] + _dot(lhs, pt)

    def block(slot, kb, widths, masked, nxt):
        sts = {n: ahead_ref[slot % 2, n, 0:widths[items[n][0]], :] for n in range(SCORE_LEAD)}
        vis = [visible(t, widths[t]) if masked else None for t in range(Q_TILES)]
        for n, (t, hd) in enumerate(items):
            lead = n + SCORE_LEAD
            if lead < len(items):
                sts[lead] = scores(slot, items[lead][0], kb, items[lead][1], widths[items[lead][0]])
            elif nxt is not None:
                tn, hn = items[lead - len(items)]
                ahead_ref[nxt[0] % 2, lead - len(items)] = scores(nxt[0], tn, nxt[1], hn, tk)
            weighted_values(slot, t, hd, kb, *softmax(slot, t, hd, sts.pop(n), vis[t]))

    def write_out(slot):
        for t in range(Q_TILES):
            ssq = jnp.zeros((1, tq), F32)
            for hd in range(N_HEADS):
                a = acc_ref[slot, t, hd]
                o = a[0:V_DIM, :] / a[V_DIM:V_DIM + 1, :]
                ssq = ssq + jnp.sum(o * o, axis=0, keepdims=True)
                ot_ref[slot % 2, t, hd * V_DIM:(hd + 1) * V_DIM, :] = o
            rs = lax.rsqrt(ssq * (1.0 / D_ATTN) + EPS)
            q0 = slot * tk + t * tq
            o_ref[0, q0:q0 + tq, :] = ((ot_ref[slot % 2, t] * rs).T * g_ref[...]).astype(BF16)

    for n in range(SCORE_LEAD):
        ahead_ref[0, n] = scores(0, items[n][0], 0, items[n][1], tk)
    for slot in range(n_pairs):
        for kb in range(slot):
            block(slot, kb, (tk, tk), False, (slot, kb + 1))
        if slot > 0:
            write_out(slot - 1)
        block(slot, slot, (tq, tk), True, (slot + 1, 0) if slot + 1 < n_pairs else None)
    write_out(n_pairs - 1)


def _attn_prompt(q, k, vt, g_attn, tq):
    n, s, _ = q.shape
    tk = vt.shape[3]
    n_pairs = s // tk
    assert tk == Q_TILES * tq and s % tk == 0
    row = lambda w: pl.BlockSpec((1, s, w), lambda b: (b, 0, 0))
    return pl.pallas_call(
        functools.partial(_attn_prompt_kernel, tq=tq),
        grid=(n,),
        in_specs=[
            row(D_HEADS_PAD),
            row(D_HEADS_PAD),
            pl.BlockSpec((1, n_pairs, D_ATTN, tk), lambda b: (b, 0, 0, 0)),
            _const_spec((1, D_ATTN)),
        ],
        out_specs=row(D_ATTN),
        out_shape=jax.ShapeDtypeStruct((n, s, D_ATTN), BF16),
        scratch_shapes=[pltpu.VMEM((n_pairs, Q_TILES, N_HEADS, V_DIM + SUM_ROWS, tq), F32),
                        pltpu.VMEM((n_pairs, Q_TILES, N_HEADS, tq), F32),
                        pltpu.VMEM((2, Q_TILES, D_ATTN, tq), F32),
                        pltpu.VMEM((2, SCORE_LEAD, tk, tq), F32)],
        compiler_params=pltpu.CompilerParams(
            dimension_semantics=("arbitrary",),
            vmem_limit_bytes=VMEM_LIMIT),
        name="attn_prompt",
    )(q, k, vt, g_attn)


def _attn_sample_kernel(q_ref, cpast_ref, krpast_ref, cnew_ref, krnew_ref, wabs_ref, psel_ref,
                        wuv_ref, g_ref, o_ref):
    l_new = q_ref.shape[1]
    q = q_ref[0]
    ql, qr = [], []
    for hd in range(N_HEADS):
        qh = q[:, hd * HEAD_PAD:(hd + 1) * HEAD_PAD]
        ql.append(_dot(qh, wabs_ref[hd]))
        qr.append(_dot(qh, psel_ref[...]))
    ql = jnp.concatenate(ql, axis=0).astype(BF16)
    qr = jnp.concatenate(qr, axis=0).astype(BF16)
    cp = cpast_ref[0].astype(BF16)
    cn = cnew_ref[0].astype(BF16)
    s_p = _dot_nt(ql, cp) + _dot(qr, krpast_ref[0].astype(BF16))
    s_n = _dot_nt(ql, cn) + _dot_nt(qr, krnew_ref[0].astype(BF16))
    m = jnp.maximum(jnp.max(s_p, axis=-1, keepdims=True), jnp.max(s_n, axis=-1, keepdims=True))
    p_p = jnp.exp2(s_p - m)
    p_n = jnp.exp2(s_n - m)
    l = jnp.sum(p_p, axis=-1, keepdims=True) + jnp.sum(p_n, axis=-1, keepdims=True)
    o_lat = ((_dot(p_p.astype(BF16), cp) + _dot(p_n.astype(BF16), cn)) / l).astype(BF16)
    o = _dot(o_lat[0:l_new, :], wuv_ref[0])
    for hd in range(1, N_HEADS):
        o = o + _dot(o_lat[hd * l_new:(hd + 1) * l_new, :], wuv_ref[hd])
    o_ref[0] = _rms(o, g_ref[...]).astype(BF16)


def _attn_sample(q, c_past, kr_past, c_new, kr_new, layer, lw):
    n, l_new, _ = q.shape
    p_len = c_past.shape[2]
    per_b = lambda rows, w: pl.BlockSpec((1, rows, w), lambda b: (b, 0, 0))
    lay = lambda rows, w: pl.BlockSpec((None, 1, rows, w), lambda b: (layer, b, 0, 0))
    return pl.pallas_call(
        _attn_sample_kernel,
        grid=(n,),
        in_specs=[
            per_b(l_new, D_HEADS_PAD),
            lay(p_len, KV_LORA),
            lay(QK_ROPE, p_len),
            lay(l_new, KV_LORA),
            lay(l_new, QK_ROPE),
            _const_spec((N_HEADS, HEAD_PAD, KV_LORA)),
            _const_spec((HEAD_PAD, QK_ROPE)),
            _const_spec((N_HEADS, KV_LORA, D_ATTN)),
            _const_spec((1, D_ATTN)),
        ],
        out_specs=per_b(l_new, D_ATTN),
        out_shape=jax.ShapeDtypeStruct((n, l_new, D_ATTN), BF16),
        compiler_params=pltpu.CompilerParams(
            dimension_semantics=("arbitrary",),
            vmem_limit_bytes=VMEM_LIMIT),
        name="attn_sample",
    )(q, c_past, kr_past, c_new, kr_new, lw["w_abs"], lw["p_sel"], lw["w_uv"], lw["g_attn"])


def _out_ffn_kernel(x_ref, convn_ref, attn_ref, p_ref, woc_ref, woa_ref, gpost_ref, gffn_ref,
                    wg_ref, wu_ref, wd_ref, gfpost_ref, wpg_ref, wpp_ref, y_ref, act_ref):
    tm = x_ref.shape[0]
    ng = max(1, tm // ROW_GROUP)
    groups = [slice(i * (tm // ng), (i + 1) * (tm // ng)) for i in range(ng)]
    mix = [_dot(convn_ref[r, :], woc_ref[...]) + _dot(attn_ref[r, :], woa_ref[...]) for r in groups]
    x1, h = [], []
    for r, mx in zip(groups, mix):
        x1.append(x_ref[r, :] + _rms(mx, gpost_ref[...]))
        h.append(_rms(x1[-1], gffn_ref[...]).astype(BF16))
    for r, hr in zip(groups, h):
        for c in range(D_FF // FF_CHUNK):
            cols = slice(c * FF_CHUNK, (c + 1) * FF_CHUNK)
            g = _dot(hr, wg_ref[:, cols])
            u = _dot(hr, wu_ref[:, cols])
            act_ref[r, cols] = (g * jax.nn.sigmoid(g) * u).astype(BF16)
    f = [_dot(act_ref[r, :], wd_ref[...]) for r in groups]
    for r, x1r, fr in zip(groups, x1, f):
        pp = _dot(p_ref[r, :].astype(BF16), wpp_ref[...])
        x2 = x1r + _rms(fr, gfpost_ref[...])
        gate = jax.nn.sigmoid(_dot(x2.astype(BF16), wpg_ref[...]))
        y_ref[r, :] = x2 + gate * pp


def _out_ffn(x, convn, attn, p, layer, lw, tm):
    m = x.shape[0]
    tok = lambda w: pl.BlockSpec((tm, w), lambda i: (i, 0))
    return pl.pallas_call(
        _out_ffn_kernel,
        grid=(m // tm,),
        in_specs=[
            tok(D_MODEL), tok(D_CONV), tok(D_ATTN),
            pl.BlockSpec((None, tm, D_PLE), lambda i: (layer, i, 0)),
            _const_spec((D_CONV, D_MODEL)),
            _const_spec((D_ATTN, D_MODEL)),
            _const_spec((1, D_MODEL)),
            _const_spec((1, D_MODEL)),
            _const_spec((D_MODEL, D_FF)),
            _const_spec((D_MODEL, D_FF)),
            _const_spec((D_FF, D_MODEL)),
            _const_spec((1, D_MODEL)),
            _const_spec((D_MODEL, D_MODEL)),
            _const_spec((D_PLE, D_MODEL)),
        ],
        out_specs=tok(D_MODEL),
        out_shape=jax.ShapeDtypeStruct((m, D_MODEL), F32),
        scratch_shapes=[pltpu.VMEM((tm, D_FF), BF16)],
        compiler_params=pltpu.CompilerParams(
            dimension_semantics=("arbitrary",),
            vmem_limit_bytes=VMEM_LIMIT_FFN),
        name="out_ffn",
    )(x, convn, attn, p, lw["w_o_conv"], lw["w_o_attn"], lw["g_mix_post"], lw["g_ffn_pre"],
      lw["w_gate"], lw["w_up"], lw["w_down"], lw["g_ffn_post"], lw["w_ple_gate"], lw["w_ple_proj"])


def _swap_halves(a):
    half = a.shape[-1] // 2
    return jnp.concatenate([a[..., half:], a[..., :half]], axis=-1)


def _layer_weights(i, g_mix_pre, w_in, w_conv, g_q, w_uq, g_kv, w_ukv, g_conv_out, g_attn_out, w_o,
                   g_mix_post, g_ffn_pre, w_ffn_gate, w_ffn_up, w_ffn_down, g_ffn_post,
                   w_ple_proj, w_ple_gate):
    row = lambda g: g[i].reshape(1, -1).astype(F32)
    w_kr = w_in[i][:, OFF_KR:IN_COLS]
    w_in_ext = jnp.concatenate(
        [w_in[i][:, :OFF_KR], jnp.zeros((D_MODEL, QK_NOPE), F32), w_kr, _swap_halves(w_kr)], axis=1)
    wq3 = w_uq[i].reshape(Q_LORA, N_HEADS, QK_DIM)
    w_q = jnp.concatenate(
        [wq3, _swap_halves(wq3[..., QK_NOPE:])], axis=-1).reshape(Q_LORA, D_HEADS_PAD)
    wkv3 = w_ukv[i].reshape(KV_LORA, N_HEADS, QK_NOPE + V_DIM)
    w_uk = wkv3[..., :QK_NOPE]
    w_uv = wkv3[..., QK_NOPE:]
    zpad = jnp.zeros((KV_LORA, N_HEADS, HEAD_PAD - QK_NOPE), F32)
    w_k = jnp.concatenate([w_uk, zpad], axis=-1).reshape(KV_LORA, D_HEADS_PAD)
    w_vt = w_uv.reshape(KV_LORA, D_ATTN).T
    w_abs = jnp.concatenate([w_uk, zpad], axis=-1).transpose(1, 2, 0)
    head_of_col = jnp.arange(D_ATTN) // V_DIM
    w_uv_heads = jnp.where(head_of_col[None, None, :] == jnp.arange(N_HEADS)[:, None, None],
                           w_uv.reshape(1, KV_LORA, D_ATTN), 0.0)
    p_sel = jnp.zeros((HEAD_PAD, QK_ROPE), F32).at[QK_NOPE:QK_DIM, :].set(jnp.eye(QK_ROPE, dtype=F32))
    return dict(
        g_mix_pre=row(g_mix_pre), w_in=w_in_ext.astype(BF16), w_conv=w_conv[i].astype(F32),
        g_q=row(g_q), w_q=w_q.astype(BF16), g_kv=row(g_kv), w_k=w_k.astype(BF16), w_vt=w_vt.astype(BF16),
        g_conv_out=row(g_conv_out), g_attn=row(g_attn_out),
        w_abs=w_abs.astype(BF16), w_uv=w_uv_heads.astype(BF16), p_sel=p_sel.astype(BF16),
        w_o_conv=w_o[i][:D_CONV].astype(BF16), w_o_attn=w_o[i][D_CONV:].astype(BF16),
        g_mix_post=row(g_mix_post), g_ffn_pre=row(g_ffn_pre),
        w_gate=w_ffn_gate[i].astype(BF16), w_up=w_ffn_up[i].astype(BF16), w_down=w_ffn_down[i].astype(BF16),
        g_ffn_post=row(g_ffn_post), w_ple_gate=w_ple_gate[i].astype(BF16), w_ple_proj=w_ple_proj[i].astype(BF16),
    )


def _rope_tables(pos):
    half = QK_ROPE // 2
    inv = ROPE_THETA ** (-jnp.arange(half, dtype=F32) / half)
    ang = pos.astype(F32)[:, None] * inv[None, :]
    cos, sin = jnp.cos(ang), jnp.sin(ang)
    n = pos.shape[0]
    cc = jnp.concatenate([cos, cos], axis=1)
    ss = jnp.concatenate([-sin, sin], axis=1)
    z_lo = jnp.zeros((n, QK_NOPE), F32)
    z_hi = jnp.zeros((n, HEAD_PAD - QK_DIM), F32)
    c_q = jnp.concatenate([jnp.ones((n, QK_NOPE), F32), cc, z_hi], axis=1) * Q_PRESCALE
    s_q = jnp.concatenate([z_lo, ss, z_hi], axis=1)
    c_k = jnp.concatenate([z_lo, cc, z_hi], axis=1)
    return jnp.stack([c_q, s_q * Q_PRESCALE, c_k, s_q])


def _pick_tile(n, cap):
    t = min(n, cap)
    while n % t:
        t //= 2
    return t


def kernel(x_prompt, x_sample, cache_kv_latent, cache_k_rope, state_conv, p_prompt, p_sample, g_mix_pre, w_in, w_conv, g_q, w_uq, g_kv, w_ukv, g_conv_out, g_attn_out, w_o, g_mix_post, g_ffn_pre, w_ffn_gate, w_ffn_up, w_ffn_down, g_ffn_post, w_ple_proj, w_ple_gate):
    depth = w_in.shape[0]
    n_p, s_p, _ = x_prompt.shape
    n_d, s_d, _ = x_sample.shape
    past_len = cache_kv_latent.shape[2]
    tab_p = _rope_tables(jnp.arange(s_p))
    tab_d = _rope_tables(past_len + jnp.arange(s_d))
    conv_zero = jnp.zeros((1, n_p, CONV_W - 1, D_CONV), F32)
    ts_p = _pick_tile(s_p, SEQ_TILE)
    tq = _pick_tile(s_p, Q_TILE)
    tm_p = _pick_tile(n_p * s_p, TOKEN_TILE)
    tm_d = _pick_tile(n_d * s_d, TOKEN_TILE)
    pp_flat = p_prompt.reshape(depth, n_p * s_p, D_PLE)
    pd_flat = p_sample.reshape(depth, n_d * s_d, D_PLE)

    cache_kr_t = jnp.swapaxes(cache_k_rope, 2, 3)
    xp, xd = x_prompt, x_sample
    stack_p = stack_d = None
    for i in range(depth):
        lw = _layer_weights(i, g_mix_pre, w_in, w_conv, g_q, w_uq, g_kv, w_ukv, g_conv_out,
                            g_attn_out, w_o, g_mix_post, g_ffn_pre, w_ffn_gate, w_ffn_up,
                            w_ffn_down, g_ffn_post, w_ple_proj, w_ple_gate)
        convn, q, *stack_p, k, vt = _in_proj(xp, tab_p, conv_zero, 0, lw, i, depth, ts_p,
                                             tk=Q_TILES * tq, stacked=stack_p)
        attn = _attn_prompt(q, k, vt, lw["g_attn"], tq)
        xp = _out_ffn(xp.reshape(n_p * s_p, D_MODEL), convn.reshape(n_p * s_p, D_CONV),
                      attn.reshape(n_p * s_p, D_ATTN), pp_flat, i, lw, tm_p).reshape(n_p, s_p, D_MODEL)
        convn, q, *stack_d = _in_proj(xd, tab_d, state_conv, i, lw, i, depth, s_d, stacked=stack_d)
        attn = _attn_sample(q, cache_kv_latent, cache_kr_t, stack_d[0], stack_d[1], i, lw)
        xd = _out_ffn(xd.reshape(n_d * s_d, D_MODEL), convn.reshape(n_d * s_d, D_CONV),
                      attn.reshape(n_d * s_d, D_ATTN), pd_flat, i, lw, tm_d).reshape(n_d, s_d, D_MODEL)

    return (xp, xd, *stack_p, *stack_d)
```

```python
import functools

import jax
import jax.numpy as jnp
from jax import lax
from jax.experimental import pallas as pl
from jax.experimental.pallas import tpu as pltpu

D_MODEL = 1024
CHUNK = 64
D_PLE = 256
D_CONV = 512
CONV_W = 3
V_DIM = 64
QK_NOPE = 64
QK_ROPE = 32
QK_DIM = QK_NOPE + QK_ROPE
N_HEADS = 8
D_ATTN = N_HEADS * V_DIM
Q_LORA = 768
KV_LORA = 256
D_FF = 2816
ROPE_THETA = 10000.0
EPS = 1e-6
SCALE = QK_DIM ** -0.5

OFF_C = D_CONV
OFF_X = 2 * D_CONV
OFF_Q = 3 * D_CONV
OFF_KV = OFF_Q + Q_LORA
OFF_KR = OFF_KV + KV_LORA
IN_COLS = OFF_KR + QK_ROPE

LANES = 128
SUBLANES = 8
HEAD_PAD = LANES
D_HEADS_PAD = N_HEADS * HEAD_PAD
IN_EXT = OFF_KR + LANES
FF_CHUNK = 256
NEG_BIG = -1e30
Q_PRESCALE = SCALE * 1.4426950408889634
Q_TILES = 2
SCORE_LEAD = 3
ROW_GROUP = 256
SUM_ROWS = 16
SEQ_TILE = 512
Q_TILE = 256
TOKEN_TILE = 512
VMEM_LIMIT = 48 * 1024 * 1024
VMEM_LIMIT_FFN = 56 * 1024 * 1024

BF16 = jnp.bfloat16
F32 = jnp.float32


def _rms(x, g):
    ms = jnp.mean(x * x, axis=-1, keepdims=True)
    return x * lax.rsqrt(ms + EPS) * g


def _dot(a, b):
    return jnp.dot(a, b, preferred_element_type=F32)


def _dot_nt(a, b):
    return lax.dot_general(a, b, (((1,), (1,)), ((), ())), preferred_element_type=F32)


def _const_spec(shape):
    nd = len(shape)
    return pl.BlockSpec(shape, lambda *_: (0,) * nd, pipeline_mode=pl.Buffered(1))


def _in_proj_kernel(x_ref, tab_ref, cinit_ref, gpre_ref, win_ref, wconv_ref, gq_ref, wq_ref,
                    gkv_ref, gconv_ref, *rest, with_kv, n_alias, layer):
    if with_kv:
        wk_ref, wvt_ref = rest[:2]
        rest = rest[2:]
    rest = rest[n_alias:]
    if with_kv:
        convn_ref, q_ref, lat_ref, kr_ref, newconv_ref, k_ref, vt_ref, carry_ref = rest
    else:
        convn_ref, q_ref, lat_ref, kr_ref, newconv_ref, carry_ref = rest
    ts = x_ref.shape[1]
    s_idx = pl.program_id(1)

    def store_layered(ref, rows, val):
        if n_alias:
            ref[0, rows, :] = val
        else:
            for d in range(ref.shape[0]):
                ref[d, 0, rows, :] = val if d == layer else jnp.zeros_like(val)

    @pl.when(s_idx == 0)
    def _():
        carry_ref[0:2, :] = cinit_ref[0]

    ng = max(1, ts // ROW_GROUP)
    gs = ts // ng
    groups = [slice(i * gs, (i + 1) * gs) for i in range(ng)]
    z = []
    for r in groups:
        h = _rms(x_ref[0, r, :], gpre_ref[...]).astype(BF16)
        z.append([_dot(h, win_ref[:, a:b]) for a, b in
                  ((0, OFF_C), (OFF_C, OFF_X), (OFF_X, OFF_Q), (OFF_Q, OFF_KV), (OFF_KV, OFF_KR),
                   (OFF_KR, IN_EXT))])
    cat = lambda j: z[0][j] if ng == 1 else jnp.concatenate([zg[j] for zg in z], axis=0)

    assert CONV_W == 3
    gb = cat(0)
    u = cat(1) * cat(2)
    row = lax.broadcasted_iota(jnp.int32, (ts, 1), 0)
    c0 = carry_ref[0:1, :]
    c1 = carry_ref[1:2, :]
    u_m1 = jnp.where(row == 0, c1, pltpu.roll(u, 1, axis=0))
    u_m2 = pltpu.roll(u, 2, axis=0)
    u_m2 = jnp.where(row == 0, c0, jnp.where(row == 1, c1, u_m2))
    conv = u_m2 * wconv_ref[0:1, :] + u_m1 * wconv_ref[1:2, :] + u * wconv_ref[2:3, :]
    convn_ref[0] = _rms(gb * conv, gconv_ref[...]).astype(BF16)
    tail = u[ts - 2:ts, :]
    carry_ref[0:2, :] = tail
    store_layered(newconv_ref, slice(None), tail)

    for gi, r in enumerate(groups):
        c_q = tab_ref[0, r, :]
        s_q = tab_ref[1, r, :]
        c_k = tab_ref[2, r, :]
        s_k = tab_ref[3, r, :]

        cqn = _rms(z[gi][3], gq_ref[...]).astype(BF16)
        qa = _dot(cqn, wq_ref[...])
        for hd in range(N_HEADS):
            blk = qa[:, hd * HEAD_PAD:(hd + 1) * HEAD_PAD]
            rot = blk * c_q + pltpu.roll(blk, HEAD_PAD - QK_ROPE, axis=1) * s_q
            q_ref[0, r, hd * HEAD_PAD:(hd + 1) * HEAD_PAD] = rot.astype(BF16)

        c_kv = _rms(z[gi][4], gkv_ref[...])
        store_layered(lat_ref, r, c_kv)
        ckvb = c_kv.astype(BF16)
        zkr = z[gi][5]
        kr_blk = zkr * c_k + pltpu.roll(zkr, HEAD_PAD - QK_ROPE, axis=1) * s_k
        store_layered(kr_ref, r, kr_blk[:, QK_NOPE:QK_DIM])
        if with_kv:
            kn = _dot(ckvb, wk_ref[...])
            for hd in range(N_HEADS):
                k_ref[0, r, hd * HEAD_PAD:(hd + 1) * HEAD_PAD] = (
                    kn[:, hd * HEAD_PAD:(hd + 1) * HEAD_PAD] + kr_blk).astype(BF16)
            vt = _dot_nt(wvt_ref[...], ckvb).astype(BF16)
            tk = vt_ref.shape[3]
            start = gi * gs
            vt_ref[0, start // tk, :, start % tk:start % tk + gs] = vt


def _in_proj(x, tables, conv_init, conv_layer, lw, layer, depth, ts, tk=None, stacked=None):
    n, s, _ = x.shape
    with_kv = tk is not None
    grid = (n, s // ts)
    tok = lambda w: pl.BlockSpec((1, ts, w), lambda b, i: (b, i, 0))

    def lay(rows, w, tiled):
        slab, first = (None, layer) if stacked is not None else (depth, 0)
        return pl.BlockSpec((slab, 1, rows, w),
                            (lambda b, i: (first, b, i, 0)) if tiled else (lambda b, i: (first, b, 0, 0)))
    out_shape = [
        jax.ShapeDtypeStruct((n, s, D_CONV), BF16),
        jax.ShapeDtypeStruct((n, s, D_HEADS_PAD), BF16),
        jax.ShapeDtypeStruct((depth, n, s, KV_LORA), F32),
        jax.ShapeDtypeStruct((depth, n, s, QK_ROPE), F32),
        jax.ShapeDtypeStruct((depth, n, CONV_W - 1, D_CONV), F32),
    ]
    in_specs = [
        tok(D_MODEL),
        pl.BlockSpec((4, ts, LANES), lambda b, i: (0, i, 0)),
        pl.BlockSpec((None, 1, CONV_W - 1, D_CONV), lambda b, i: (conv_layer, b, 0, 0)),
        _const_spec((1, D_MODEL)),
        _const_spec((D_MODEL, IN_EXT)),
        _const_spec((CONV_W, D_CONV)),
        _const_spec((1, Q_LORA)),
        _const_spec((Q_LORA, D_HEADS_PAD)),
        _const_spec((1, KV_LORA)),
        _const_spec((1, D_CONV)),
    ]
    out_specs = [
        tok(D_CONV), tok(D_HEADS_PAD), lay(ts, KV_LORA, True), lay(ts, QK_ROPE, True),
        lay(CONV_W - 1, D_CONV, False),
    ]
    args = [x, tables, conv_init, lw["g_mix_pre"], lw["w_in"], lw["w_conv"], lw["g_q"], lw["w_q"],
            lw["g_kv"], lw["g_conv_out"]]
    if with_kv:
        in_specs += [_const_spec((KV_LORA, D_HEADS_PAD)), _const_spec((D_ATTN, KV_LORA))]
        args += [lw["w_k"], lw["w_vt"]]
        out_shape += [jax.ShapeDtypeStruct((n, s, D_HEADS_PAD), BF16),
                      jax.ShapeDtypeStruct((n, s // tk, D_ATTN, tk), BF16)]
        out_specs += [tok(D_HEADS_PAD),
                      pl.BlockSpec((1, ts // tk, D_ATTN, tk), lambda b, i: (b, i, 0, 0))]
    aliases = {}
    if stacked is not None:
        for j, arr in enumerate(stacked):
            aliases[len(args)] = 2 + j
            in_specs.append(pl.BlockSpec(memory_space=pl.ANY))
            args.append(arr)
    return pl.pallas_call(
        functools.partial(_in_proj_kernel, with_kv=with_kv, n_alias=len(aliases), layer=layer),
        grid=grid,
        in_specs=in_specs,
        out_specs=tuple(out_specs),
        out_shape=tuple(out_shape),
        input_output_aliases=aliases,
        scratch_shapes=[pltpu.VMEM((SUBLANES, D_CONV), F32)],
        compiler_params=pltpu.CompilerParams(
            dimension_semantics=("arbitrary", "arbitrary"),
            vmem_limit_bytes=VMEM_LIMIT),
        name="in_proj_kv" if with_kv else "in_proj",
    )(*args)


def _attn_prompt_kernel(q_ref, k_ref, vt_ref, g_ref, o_ref, acc_ref, m_ref, ot_ref, *, tq):
    tk = vt_ref.shape[3]
    n_pairs = vt_ref.shape[1]
    m_ref[...] = jnp.full(m_ref.shape, NEG_BIG, F32)
    acc_ref[...] = jnp.zeros(acc_ref.shape, F32)
    items = [(t, hd) for hd in range(N_HEADS) for t in range(Q_TILES)]

    def scores(slot, t, kb, hd, width):
        cols = slice(hd * HEAD_PAD, (hd + 1) * HEAD_PAD)
        k0 = kb * tk
        q0 = slot * tk + t * tq
        return _dot_nt(k_ref[0, k0:k0 + width, cols], q_ref[0, q0:q0 + tq, cols])

    def visible(t, width):
        ck = lax.broadcasted_iota(jnp.int32, (width, tq), 0) // CHUNK
        rq = (lax.broadcasted_iota(jnp.int32, (width, tq), 1) + t * tq) // CHUNK
        return ck <= rq

    def softmax(slot, t, hd, st, vis):
        if vis is not None:
            st = jnp.where(vis, st, NEG_BIG)
        m_prev = m_ref[slot, t, hd:hd + 1, :]
        m_new = jnp.maximum(m_prev, jnp.max(st, axis=0, keepdims=True))
        m_ref[slot, t, hd:hd + 1, :] = m_new
        return jnp.exp2(m_prev - m_new), jnp.exp2(st - m_new).astype(BF16)

    def weighted_values(slot, t, hd, kb, alpha, pt):
        width = pt.shape[0]
        lhs = jnp.concatenate([vt_ref[0, kb, hd * V_DIM:(hd + 1) * V_DIM, 0:width],
                               jnp.ones((SUM_ROWS, width), BF16)], axis=0)
        acc_ref[slot, t, hd] = alpha * acc_ref[slot, t, hd] + _dot(lhs, pt)

    def write_out(slot):
        for t in range(Q_TILES):
            ssq = jnp.zeros((1, tq), F32)
            for hd in range(N_HEADS):
                a = acc_ref[slot, t, hd]
                o = a[0:V_DIM, :] / a[V_DIM:V_DIM + 1, :]
                ssq = ssq + jnp.sum(o * o, axis=0, keepdims=True)
                ot_ref[slot % 2, t, hd * V_DIM:(hd + 1) * V_DIM, :] = o
            rs = lax.rsqrt(ssq * (1.0 / D_ATTN) + EPS)
            q0 = slot * tk + t * tq
            o_ref[0, q0:q0 + tq, :] = ((ot_ref[slot % 2, t] * rs).T * g_ref[...]).astype(BF16)

    work = []
    for slot in range(n_pairs):
        for kb in range(slot + 1):
            widths = (tk, tk) if kb < slot else (tq, tk)
            work += [(slot, kb, t, hd, widths[t], kb == slot) for t, hd in items]

    sts = {}
    for n in range(len(work) + SCORE_LEAD):
        if n < len(work):
            slot, kb, t, hd, width, _ = work[n]
            sts[n] = scores(slot, t, kb, hd, width)
        if n >= SCORE_LEAD:
            slot, kb, t, hd, width, masked = work[n - SCORE_LEAD]
            if masked and (t, hd) == items[0]:
                vis = [visible(0, tq), visible(1, tk)]
                if slot > 0:
                    write_out(slot - 1)
            weighted_values(slot, t, hd, kb,
                            *softmax(slot, t, hd, sts.pop(n - SCORE_LEAD), vis[t] if masked else None))
    write_out(n_pairs - 1)


def _attn_prompt(q, k, vt, g_attn, tq):
    n, s, _ = q.shape
    tk = vt.shape[3]
    n_pairs = s // tk
    assert tk == Q_TILES * tq and s % tk == 0
    row = lambda w: pl.BlockSpec((1, s, w), lambda b: (b, 0, 0))
    return pl.pallas_call(
        functools.partial(_attn_prompt_kernel, tq=tq),
        grid=(n,),
        in_specs=[
            row(D_HEADS_PAD),
            row(D_HEADS_PAD),
            pl.BlockSpec((1, n_pairs, D_ATTN, tk), lambda b: (b, 0, 0, 0)),
            _const_spec((1, D_ATTN)),
        ],
        out_specs=row(D_ATTN),
        out_shape=jax.ShapeDtypeStruct((n, s, D_ATTN), BF16),
        scratch_shapes=[pltpu.VMEM((n_pairs, Q_TILES, N_HEADS, V_DIM + SUM_ROWS, tq), F32),
                        pltpu.VMEM((n_pairs, Q_TILES, N_HEADS, tq), F32),
                        pltpu.VMEM((2, Q_TILES, D_ATTN, tq), F32)],
        compiler_params=pltpu.CompilerParams(
            dimension_semantics=("arbitrary",),
            vmem_limit_bytes=VMEM_LIMIT),
        name="attn_prompt",
    )(q, k, vt, g_attn)


def _attn_sample_kernel(q_ref, cpast_ref, krpast_ref, cnew_ref, krnew_ref, wabs_ref, psel_ref,
                        wuv_ref, g_ref, o_ref):
    l_new = q_ref.shape[1]
    q = q_ref[0]
    ql, qr = [], []
    for hd in range(N_HEADS):
        qh = q[:, hd * HEAD_PAD:(hd + 1) * HEAD_PAD]
        ql.append(_dot(qh, wabs_ref[hd]))
        qr.append(_dot(qh, psel_ref[...]))
    ql = jnp.concatenate(ql, axis=0).astype(BF16)
    qr = jnp.concatenate(qr, axis=0).astype(BF16)
    cp = cpast_ref[0].astype(BF16)
    cn = cnew_ref[0].astype(BF16)
    s_p = _dot_nt(ql, cp) + _dot(qr, krpast_ref[0].astype(BF16))
    s_n = _dot_nt(ql, cn) + _dot_nt(qr, krnew_ref[0].astype(BF16))
    m = jnp.maximum(jnp.max(s_p, axis=-1, keepdims=True), jnp.max(s_n, axis=-1, keepdims=True))
    p_p = jnp.exp2(s_p - m)
    p_n = jnp.exp2(s_n - m)
    l = jnp.sum(p_p, axis=-1, keepdims=True) + jnp.sum(p_n, axis=-1, keepdims=True)
    o_lat = ((_dot(p_p.astype(BF16), cp) + _dot(p_n.astype(BF16), cn)) / l).astype(BF16)
    o = _dot(o_lat[0:l_new, :], wuv_ref[0])
    for hd in range(1, N_HEADS):
        o = o + _dot(o_lat[hd * l_new:(hd + 1) * l_new, :], wuv_ref[hd])
    o_ref[0] = _rms(o, g_ref[...]).astype(BF16)


def _attn_sample(q, c_past, kr_past, c_new, kr_new, layer, lw):
    n, l_new, _ = q.shape
    p_len = c_past.shape[2]
    per_b = lambda rows, w: pl.BlockSpec((1, rows, w), lambda b: (b, 0, 0))
    lay = lambda rows, w: pl.BlockSpec((None, 1, rows, w), lambda b: (layer, b, 0, 0))
    return pl.pallas_call(
        _attn_sample_kernel,
        grid=(n,),
        in_specs=[
            per_b(l_new, D_HEADS_PAD),
            lay(p_len, KV_LORA),
            lay(QK_ROPE, p_len),
            lay(l_new, KV_LORA),
            lay(l_new, QK_ROPE),
            _const_spec((N_HEADS, HEAD_PAD, KV_LORA)),
            _const_spec((HEAD_PAD, QK_ROPE)),
            _const_spec((N_HEADS, KV_LORA, D_ATTN)),
            _const_spec((1, D_ATTN)),
        ],
        out_specs=per_b(l_new, D_ATTN),
        out_shape=jax.ShapeDtypeStruct((n, l_new, D_ATTN), BF16),
        compiler_params=pltpu.CompilerParams(
            dimension_semantics=("arbitrary",),
            vmem_limit_bytes=VMEM_LIMIT),
        name="attn_sample",
    )(q, c_past, kr_past, c_new, kr_new, lw["w_abs"], lw["p_sel"], lw["w_uv"], lw["g_attn"])


def _out_ffn_kernel(x_ref, convn_ref, attn_ref, p_ref, woc_ref, woa_ref, gpost_ref, gffn_ref,
                    wg_ref, wu_ref, wd_ref, gfpost_ref, wpg_ref, wpp_ref, y_ref, act_ref):
    tm = x_ref.shape[0]
    ng = max(1, tm // ROW_GROUP)
    groups = [slice(i * (tm // ng), (i + 1) * (tm // ng)) for i in range(ng)]
    mix = [_dot(convn_ref[r, :], woc_ref[...]) + _dot(attn_ref[r, :], woa_ref[...]) for r in groups]
    x1, h = [], []
    for r, mx in zip(groups, mix):
        x1.append(x_ref[r, :] + _rms(mx, gpost_ref[...]))
        h.append(_rms(x1[-1], gffn_ref[...]).astype(BF16))
    for r, hr in zip(groups, h):
        for c in range(D_FF // FF_CHUNK):
            cols = slice(c * FF_CHUNK, (c + 1) * FF_CHUNK)
            g = _dot(hr, wg_ref[:, cols])
            u = _dot(hr, wu_ref[:, cols])
            act_ref[r, cols] = (g * jax.nn.sigmoid(g) * u).astype(BF16)
    f = [_dot(act_ref[r, :], wd_ref[...]) for r in groups]
    for r, x1r, fr in zip(groups, x1, f):
        pp = _dot(p_ref[r, :].astype(BF16), wpp_ref[...])
        x2 = x1r + _rms(fr, gfpost_ref[...])
        gate = jax.nn.sigmoid(_dot(x2.astype(BF16), wpg_ref[...]))
        y_ref[r, :] = x2 + gate * pp


def _out_ffn(x, convn, attn, p, layer, lw, tm):
    m = x.shape[0]
    tok = lambda w: pl.BlockSpec((tm, w), lambda i: (i, 0))
    return pl.pallas_call(
        _out_ffn_kernel,
        grid=(m // tm,),
        in_specs=[
            tok(D_MODEL), tok(D_CONV), tok(D_ATTN),
            pl.BlockSpec((None, tm, D_PLE), lambda i: (layer, i, 0)),
            _const_spec((D_CONV, D_MODEL)),
            _const_spec((D_ATTN, D_MODEL)),
            _const_spec((1, D_MODEL)),
            _const_spec((1, D_MODEL)),
            _const_spec((D_MODEL, D_FF)),
            _const_spec((D_MODEL, D_FF)),
            _const_spec((D_FF, D_MODEL)),
            _const_spec((1, D_MODEL)),
            _const_spec((D_MODEL, D_MODEL)),
            _const_spec((D_PLE, D_MODEL)),
        ],
        out_specs=tok(D_MODEL),
        out_shape=jax.ShapeDtypeStruct((m, D_MODEL), F32),
        scratch_shapes=[pltpu.VMEM((tm, D_FF), BF16)],
        compiler_params=pltpu.CompilerParams(
            dimension_semantics=("arbitrary",),
            vmem_limit_bytes=VMEM_LIMIT_FFN),
        name="out_ffn",
    )(x, convn, attn, p, lw["w_o_conv"], lw["w_o_attn"], lw["g_mix_post"], lw["g_ffn_pre"],
      lw["w_gate"], lw["w_up"], lw["w_down"], lw["g_ffn_post"], lw["w_ple_gate"], lw["w_ple_proj"])


def _swap_halves(a):
    half = a.shape[-1] // 2
    return jnp.concatenate([a[..., half:], a[..., :half]], axis=-1)


def _layer_weights(i, g_mix_pre, w_in, w_conv, g_q, w_uq, g_kv, w_ukv, g_conv_out, g_attn_out, w_o,
                   g_mix_post, g_ffn_pre, w_ffn_gate, w_ffn_up, w_ffn_down, g_ffn_post,
                   w_ple_proj, w_ple_gate):
    row = lambda g: g[i].reshape(1, -1).astype(F32)
    w_kr = w_in[i][:, OFF_KR:IN_COLS]
    w_in_ext = jnp.concatenate(
        [w_in[i][:, :OFF_KR], jnp.zeros((D_MODEL, QK_NOPE), F32), w_kr, _swap_halves(w_kr)], axis=1)
    wq3 = w_uq[i].reshape(Q_LORA, N_HEADS, QK_DIM)
    w_q = jnp.concatenate(
        [wq3, _swap_halves(wq3[..., QK_NOPE:])], axis=-1).reshape(Q_LORA, D_HEADS_PAD)
    wkv3 = w_ukv[i].reshape(KV_LORA, N_HEADS, QK_NOPE + V_DIM)
    w_uk = wkv3[..., :QK_NOPE]
    w_uv = wkv3[..., QK_NOPE:]
    zpad = jnp.zeros((KV_LORA, N_HEADS, HEAD_PAD - QK_NOPE), F32)
    w_k = jnp.concatenate([w_uk, zpad], axis=-1).reshape(KV_LORA, D_HEADS_PAD)
    w_vt = w_uv.reshape(KV_LORA, D_ATTN).T
    w_abs = jnp.concatenate([w_uk, zpad], axis=-1).transpose(1, 2, 0)
    head_of_col = jnp.arange(D_ATTN) // V_DIM
    w_uv_heads = jnp.where(head_of_col[None, None, :] == jnp.arange(N_HEADS)[:, None, None],
                           w_uv.reshape(1, KV_LORA, D_ATTN), 0.0)
    p_sel = jnp.zeros((HEAD_PAD, QK_ROPE), F32).at[QK_NOPE:QK_DIM, :].set(jnp.eye(QK_ROPE, dtype=F32))
    return dict(
        g_mix_pre=row(g_mix_pre), w_in=w_in_ext.astype(BF16), w_conv=w_conv[i].astype(F32),
        g_q=row(g_q), w_q=w_q.astype(BF16), g_kv=row(g_kv), w_k=w_k.astype(BF16), w_vt=w_vt.astype(BF16),
        g_conv_out=row(g_conv_out), g_attn=row(g_attn_out),
        w_abs=w_abs.astype(BF16), w_uv=w_uv_heads.astype(BF16), p_sel=p_sel.astype(BF16),
        w_o_conv=w_o[i][:D_CONV].astype(BF16), w_o_attn=w_o[i][D_CONV:].astype(BF16),
        g_mix_post=row(g_mix_post), g_ffn_pre=row(g_ffn_pre),
        w_gate=w_ffn_gate[i].astype(BF16), w_up=w_ffn_up[i].astype(BF16), w_down=w_ffn_down[i].astype(BF16),
        g_ffn_post=row(g_ffn_post), w_ple_gate=w_ple_gate[i].astype(BF16), w_ple_proj=w_ple_proj[i].astype(BF16),
    )


def _rope_tables(pos):
    half = QK_ROPE // 2
    inv = ROPE_THETA ** (-jnp.arange(half, dtype=F32) / half)
    ang = pos.astype(F32)[:, None] * inv[None, :]
    cos, sin = jnp.cos(ang), jnp.sin(ang)
    n = pos.shape[0]
    cc = jnp.concatenate([cos, cos], axis=1)
    ss = jnp.concatenate([-sin, sin], axis=1)
    z_lo = jnp.zeros((n, QK_NOPE), F32)
    z_hi = jnp.zeros((n, HEAD_PAD - QK_DIM), F32)
    c_q = jnp.concatenate([jnp.ones((n, QK_NOPE), F32), cc, z_hi], axis=1) * Q_PRESCALE
    s_q = jnp.concatenate([z_lo, ss, z_hi], axis=1)
    c_k = jnp.concatenate([z_lo, cc, z_hi], axis=1)
    return jnp.stack([c_q, s_q * Q_PRESCALE, c_k, s_q])


def _pick_tile(n, cap):
    t = min(n, cap)
    while n % t:
        t //= 2
    return t


def kernel(x_prompt, x_sample, cache_kv_latent, cache_k_rope, state_conv, p_prompt, p_sample, g_mix_pre, w_in, w_conv, g_q, w_uq, g_kv, w_ukv, g_conv_out, g_attn_out, w_o, g_mix_post, g_ffn_pre, w_ffn_gate, w_ffn_up, w_ffn_down, g_ffn_post, w_ple_proj, w_ple_gate):
    depth = w_in.shape[0]
    n_p, s_p, _ = x_prompt.shape
    n_d, s_d, _ = x_sample.shape
    past_len = cache_kv_latent.shape[2]
    tab_p = _rope_tables(jnp.arange(s_p))
    tab_d = _rope_tables(past_len + jnp.arange(s_d))
    conv_zero = jnp.zeros((1, n_p, CONV_W - 1, D_CONV), F32)
    ts_p = _pick_tile(s_p, SEQ_TILE)
    tq = _pick_tile(s_p, Q_TILE)
    tm_p = _pick_tile(n_p * s_p, TOKEN_TILE)
    tm_d = _pick_tile(n_d * s_d, TOKEN_TILE)
    pp_flat = p_prompt.reshape(depth, n_p * s_p, D_PLE)
    pd_flat = p_sample.reshape(depth, n_d * s_d, D_PLE)

    cache_kr_t = jnp.swapaxes(cache_k_rope, 2, 3)
    xp, xd = x_prompt, x_sample
    stack_p = stack_d = None
    for i in range(depth):
        lw = _layer_weights(i, g_mix_pre, w_in, w_conv, g_q, w_uq, g_kv, w_ukv, g_conv_out,
                            g_attn_out, w_o, g_mix_post, g_ffn_pre, w_ffn_gate, w_ffn_up,
                            w_ffn_down, g_ffn_post, w_ple_proj, w_ple_gate)
        convn, q, *stack_p, k, vt = _in_proj(xp, tab_p, conv_zero, 0, lw, i, depth, ts_p,
                                             tk=Q_TILES * tq, stacked=stack_p)
        attn = _attn_prompt(q, k, vt, lw["g_attn"], tq)
        xp = _out_ffn(xp.reshape(n_p * s_p, D_MODEL), convn.reshape(n_p * s_p, D_CONV),
                      attn.reshape(n_p * s_p, D_ATTN), pp_flat, i, lw, tm_p).reshape(n_p, s_p, D_MODEL)
        convn, q, *stack_d = _in_proj(xd, tab_d, state_conv, i, lw, i, depth, s_d, stacked=stack_d)
        attn = _attn_sample(q, cache_kv_latent, cache_kr_t, stack_d[0], stack_d[1], i, lw)
        xd = _out_ffn(xd.reshape(n_d * s_d, D_MODEL), convn.reshape(n_d * s_d, D_CONV),
                      attn.reshape(n_d * s_d, D_ATTN), pd_flat, i, lw, tm_d).reshape(n_d, s_d, D_MODEL)

    return (xp, xd, *stack_p, *stack_d)
```

```python
import functools

import jax
import jax.numpy as jnp
from jax import lax
from jax.experimental import pallas as pl
from jax.experimental.pallas import tpu as pltpu

D_MODEL = 1024
CHUNK = 64
D_PLE = 256
D_CONV = 512
CONV_W = 3
V_DIM = 64
QK_NOPE = 64
QK_ROPE = 32
QK_DIM = QK_NOPE + QK_ROPE
N_HEADS = 8
D_ATTN = N_HEADS * V_DIM
Q_LORA = 768
KV_LORA = 256
D_FF = 2816
ROPE_THETA = 10000.0
EPS = 1e-6
SCALE = QK_DIM ** -0.5

OFF_C = D_CONV
OFF_X = 2 * D_CONV
OFF_Q = 3 * D_CONV
OFF_KV = OFF_Q + Q_LORA
OFF_KR = OFF_KV + KV_LORA
IN_COLS = OFF_KR + QK_ROPE

LANES = 128
SUBLANES = 8
HEAD_PAD = LANES
D_HEADS_PAD = N_HEADS * HEAD_PAD
IN_EXT = OFF_KR + LANES
FF_CHUNK = 256
NEG_BIG = -1e30
Q_PRESCALE = SCALE * 1.4426950408889634
Q_TILES = 2
SCORE_LEAD = 3
ROW_GROUP = 256
SUM_ROWS = 16
SEQ_TILE = 512
Q_TILE = 256
TOKEN_TILE = 512
VMEM_LIMIT = 48 * 1024 * 1024
VMEM_LIMIT_FFN = 56 * 1024 * 1024

BF16 = jnp.bfloat16
F32 = jnp.float32


def _rms(x, g):
    ms = jnp.mean(x * x, axis=-1, keepdims=True)
    return x * lax.rsqrt(ms + EPS) * g


def _dot(a, b):
    return jnp.dot(a, b, preferred_element_type=F32)


def _dot_nt(a, b):
    return lax.dot_general(a, b, (((1,), (1,)), ((), ())), preferred_element_type=F32)


def _const_spec(shape):
    nd = len(shape)
    return pl.BlockSpec(shape, lambda *_: (0,) * nd, pipeline_mode=pl.Buffered(1))


def _in_proj_kernel(x_ref, tab_ref, cinit_ref, gpre_ref, win_ref, wconv_ref, gq_ref, wq_ref,
                    gkv_ref, gconv_ref, *rest, with_kv, n_alias, layer):
    if with_kv:
        wk_ref, wvt_ref = rest[:2]
        rest = rest[2:]
    rest = rest[n_alias:]
    if with_kv:
        convn_ref, q_ref, lat_ref, kr_ref, newconv_ref, k_ref, vt_ref, carry_ref = rest
    else:
        convn_ref, q_ref, lat_ref, kr_ref, newconv_ref, carry_ref = rest
    ts = x_ref.shape[1]
    s_idx = pl.program_id(1)

    def store_layered(ref, rows, val):
        if n_alias:
            ref[0, rows, :] = val
        else:
            for d in range(ref.shape[0]):
                ref[d, 0, rows, :] = val if d == layer else jnp.zeros_like(val)

    @pl.when(s_idx == 0)
    def _():
        carry_ref[0:2, :] = cinit_ref[0]

    ng = max(1, ts // ROW_GROUP)
    gs = ts // ng
    groups = [slice(i * gs, (i + 1) * gs) for i in range(ng)]
    z = []
    for r in groups:
        h = _rms(x_ref[0, r, :], gpre_ref[...]).astype(BF16)
        z.append([_dot(h, win_ref[:, a:b]) for a, b in
                  ((0, OFF_C), (OFF_C, OFF_X), (OFF_X, OFF_Q), (OFF_Q, OFF_KV), (OFF_KV, OFF_KR),
                   (OFF_KR, IN_EXT))])
    cat = lambda j: z[0][j] if ng == 1 else jnp.concatenate([zg[j] for zg in z], axis=0)

    assert CONV_W == 3
    gb = cat(0)
    u = cat(1) * cat(2)
    row = lax.broadcasted_iota(jnp.int32, (ts, 1), 0)
    c0 = carry_ref[0:1, :]
    c1 = carry_ref[1:2, :]
    u_m1 = jnp.where(row == 0, c1, pltpu.roll(u, 1, axis=0))
    u_m2 = pltpu.roll(u, 2, axis=0)
    u_m2 = jnp.where(row == 0, c0, jnp.where(row == 1, c1, u_m2))
    conv = u_m2 * wconv_ref[0:1, :] + u_m1 * wconv_ref[1:2, :] + u * wconv_ref[2:3, :]
    convn_ref[0] = _rms(gb * conv, gconv_ref[...]).astype(BF16)
    tail = u[ts - 2:ts, :]
    carry_ref[0:2, :] = tail
    store_layered(newconv_ref, slice(None), tail)

    for gi, r in enumerate(groups):
        c_q = tab_ref[0, r, :]
        s_q = tab_ref[1, r, :]
        c_k = tab_ref[2, r, :]
        s_k = tab_ref[3, r, :]

        cqn = _rms(z[gi][3], gq_ref[...]).astype(BF16)
        qa = _dot(cqn, wq_ref[...])
        for hd in range(N_HEADS):
            blk = qa[:, hd * HEAD_PAD:(hd + 1) * HEAD_PAD]
            rot = blk * c_q + pltpu.roll(blk, HEAD_PAD - QK_ROPE, axis=1) * s_q
            q_ref[0, r, hd * HEAD_PAD:(hd + 1) * HEAD_PAD] = rot.astype(BF16)

        c_kv = _rms(z[gi][4], gkv_ref[...])
        store_layered(lat_ref, r, c_kv)
        ckvb = c_kv.astype(BF16)
        zkr = z[gi][5]
        kr_blk = zkr * c_k + pltpu.roll(zkr, HEAD_PAD - QK_ROPE, axis=1) * s_k
        store_layered(kr_ref, r, kr_blk[:, QK_NOPE:QK_DIM])
        if with_kv:
            kn = _dot(ckvb, wk_ref[...])
            for hd in range(N_HEADS):
                k_ref[0, r, hd * HEAD_PAD:(hd + 1) * HEAD_PAD] = (
                    kn[:, hd * HEAD_PAD:(hd + 1) * HEAD_PAD] + kr_blk).astype(BF16)
            vt = _dot_nt(wvt_ref[...], ckvb).astype(BF16)
            tk = vt_ref.shape[3]
            start = gi * gs
            vt_ref[0, start // tk, :, start % tk:start % tk + gs] = vt


def _in_proj(x, tables, conv_init, conv_layer, lw, layer, depth, ts, tk=None, stacked=None):
    n, s, _ = x.shape
    with_kv = tk is not None
    grid = (n, s // ts)
    tok = lambda w: pl.BlockSpec((1, ts, w), lambda b, i: (b, i, 0))

    def lay(rows, w, tiled):
        slab, first = (None, layer) if stacked is not None else (depth, 0)
        return pl.BlockSpec((slab, 1, rows, w),
                            (lambda b, i: (first, b, i, 0)) if tiled else (lambda b, i: (first, b, 0, 0)))
    out_shape = [
        jax.ShapeDtypeStruct((n, s, D_CONV), BF16),
        jax.ShapeDtypeStruct((n, s, D_HEADS_PAD), BF16),
        jax.ShapeDtypeStruct((depth, n, s, KV_LORA), F32),
        jax.ShapeDtypeStruct((depth, n, s, QK_ROPE), F32),
        jax.ShapeDtypeStruct((depth, n, CONV_W - 1, D_CONV), F32),
    ]
    in_specs = [
        tok(D_MODEL),
        pl.BlockSpec((4, ts, LANES), lambda b, i: (0, i, 0)),
        pl.BlockSpec((None, 1, CONV_W - 1, D_CONV), lambda b, i: (conv_layer, b, 0, 0)),
        _const_spec((1, D_MODEL)),
        _const_spec((D_MODEL, IN_EXT)),
        _const_spec((CONV_W, D_CONV)),
        _const_spec((1, Q_LORA)),
        _const_spec((Q_LORA, D_HEADS_PAD)),
        _const_spec((1, KV_LORA)),
        _const_spec((1, D_CONV)),
    ]
    out_specs = [
        tok(D_CONV), tok(D_HEADS_PAD), lay(ts, KV_LORA, True), lay(ts, QK_ROPE, True),
        lay(CONV_W - 1, D_CONV, False),
    ]
    args = [x, tables, conv_init, lw["g_mix_pre"], lw["w_in"], lw["w_conv"], lw["g_q"], lw["w_q"],
            lw["g_kv"], lw["g_conv_out"]]
    if with_kv:
        in_specs += [_const_spec((KV_LORA, D_HEADS_PAD)), _const_spec((D_ATTN, KV_LORA))]
        args += [lw["w_k"], lw["w_vt"]]
        out_shape += [jax.ShapeDtypeStruct((n, s, D_HEADS_PAD), BF16),
                      jax.ShapeDtypeStruct((n, s // tk, D_ATTN, tk), BF16)]
        out_specs += [tok(D_HEADS_PAD),
                      pl.BlockSpec((1, ts // tk, D_ATTN, tk), lambda b, i: (b, i, 0, 0))]
    aliases = {}
    if stacked is not None:
        for j, arr in enumerate(stacked):
            aliases[len(args)] = 2 + j
            in_specs.append(pl.BlockSpec(memory_space=pl.ANY))
            args.append(arr)
    return pl.pallas_call(
        functools.partial(_in_proj_kernel, with_kv=with_kv, n_alias=len(aliases), layer=layer),
        grid=grid,
        in_specs=in_specs,
        out_specs=tuple(out_specs),
        out_shape=tuple(out_shape),
        input_output_aliases=aliases,
        scratch_shapes=[pltpu.VMEM((SUBLANES, D_CONV), F32)],
        compiler_params=pltpu.CompilerParams(
            dimension_semantics=("arbitrary", "arbitrary"),
            vmem_limit_bytes=VMEM_LIMIT),
        name="in_proj_kv" if with_kv else "in_proj",
    )(*args)


def _attn_prompt_kernel(q_ref, k_ref, vt_ref, g_ref, o_ref, acc_ref, m_ref, ot_ref, *, tq):
    tk = vt_ref.shape[3]
    n_pairs = vt_ref.shape[1]
    items = [(t, hd) for hd in range(N_HEADS) for t in range(Q_TILES)]

    def scores(slot, t, kb, hd, width):
        cols = slice(hd * HEAD_PAD, (hd + 1) * HEAD_PAD)
        k0 = kb * tk
        q0 = slot * tk + t * tq
        return _dot_nt(k_ref[0, k0:k0 + width, cols], q_ref[0, q0:q0 + tq, cols])

    def visible(t, width):
        ck = lax.broadcasted_iota(jnp.int32, (width, tq), 0) // CHUNK
        rq = (lax.broadcasted_iota(jnp.int32, (width, tq), 1) + t * tq) // CHUNK
        return ck <= rq

    def softmax(slot, t, hd, st, vis, first):
        if vis is not None:
            st = jnp.where(vis, st, NEG_BIG)
        m_new = jnp.max(st, axis=0, keepdims=True)
        alpha = None
        if not first:
            m_prev = m_ref[slot, t, hd:hd + 1, :]
            m_new = jnp.maximum(m_prev, m_new)
            alpha = jnp.exp2(m_prev - m_new)
        m_ref[slot, t, hd:hd + 1, :] = m_new
        return alpha, jnp.exp2(st - m_new).astype(BF16)

    def weighted_values(slot, t, hd, kb, alpha, pt):
        width = pt.shape[0]
        lhs = jnp.concatenate([vt_ref[0, kb, hd * V_DIM:(hd + 1) * V_DIM, 0:width],
                               jnp.ones((SUM_ROWS, width), BF16)], axis=0)
        pv = _dot(lhs, pt)
        acc_ref[slot, t, hd] = pv if alpha is None else alpha * acc_ref[slot, t, hd] + pv

    def write_out(slot):
        for t in range(Q_TILES):
            ssq = jnp.zeros((1, tq), F32)
            for hd in range(N_HEADS):
                a = acc_ref[slot, t, hd]
                o = a[0:V_DIM, :] / a[V_DIM:V_DIM + 1, :]
                ssq = ssq + jnp.sum(o * o, axis=0, keepdims=True)
                ot_ref[slot % 2, t, hd * V_DIM:(hd + 1) * V_DIM, :] = o
            rs = lax.rsqrt(ssq * (1.0 / D_ATTN) + EPS)
            q0 = slot * tk + t * tq
            o_ref[0, q0:q0 + tq, :] = ((ot_ref[slot % 2, t] * rs).T * g_ref[...]).astype(BF16)

    work = []
    for slot in range(n_pairs):
        for kb in range(slot + 1):
            widths = (tk, tk) if kb < slot else (tq, tk)
            work += [(slot, kb, t, hd, widths[t], kb == slot) for t, hd in items]

    sts = {}
    for n in range(len(work) + SCORE_LEAD):
        if n < len(work):
            slot, kb, t, hd, width, _ = work[n]
            sts[n] = scores(slot, t, kb, hd, width)
        if n >= SCORE_LEAD:
            slot, kb, t, hd, width, masked = work[n - SCORE_LEAD]
            if masked and (t, hd) == items[0]:
                vis = [visible(0, tq), visible(1, tk)]
                if slot > 0:
                    write_out(slot - 1)
            weighted_values(slot, t, hd, kb,
                            *softmax(slot, t, hd, sts.pop(n - SCORE_LEAD), vis[t] if masked else None,
                                     kb == 0))
    write_out(n_pairs - 1)


def _attn_prompt(q, k, vt, g_attn, tq):
    n, s, _ = q.shape
    tk = vt.shape[3]
    n_pairs = s // tk
    assert tk == Q_TILES * tq and s % tk == 0
    row = lambda w: pl.BlockSpec((1, s, w), lambda b: (b, 0, 0))
    return pl.pallas_call(
        functools.partial(_attn_prompt_kernel, tq=tq),
        grid=(n,),
        in_specs=[
            row(D_HEADS_PAD),
            row(D_HEADS_PAD),
            pl.BlockSpec((1, n_pairs, D_ATTN, tk), lambda b: (b, 0, 0, 0)),
            _const_spec((1, D_ATTN)),
        ],
        out_specs=row(D_ATTN),
        out_shape=jax.ShapeDtypeStruct((n, s, D_ATTN), BF16),
        scratch_shapes=[pltpu.VMEM((n_pairs, Q_TILES, N_HEADS, V_DIM + SUM_ROWS, tq), F32),
                        pltpu.VMEM((n_pairs, Q_TILES, N_HEADS, tq), F32),
                        pltpu.VMEM((2, Q_TILES, D_ATTN, tq), F32)],
        compiler_params=pltpu.CompilerParams(
            dimension_semantics=("arbitrary",),
            vmem_limit_bytes=VMEM_LIMIT),
        name="attn_prompt",
    )(q, k, vt, g_attn)


def _attn_sample_kernel(q_ref, cpast_ref, krpast_ref, cnew_ref, krnew_ref, wabs_ref, psel_ref,
                        wuv_ref, g_ref, o_ref):
    l_new = q_ref.shape[1]
    q = q_ref[0]
    ql, qr = [], []
    for hd in range(N_HEADS):
        qh = q[:, hd * HEAD_PAD:(hd + 1) * HEAD_PAD]
        ql.append(_dot(qh, wabs_ref[hd]))
        qr.append(_dot(qh, psel_ref[...]))
    ql = jnp.concatenate(ql, axis=0).astype(BF16)
    qr = jnp.concatenate(qr, axis=0).astype(BF16)
    cp = cpast_ref[0].astype(BF16)
    cn = cnew_ref[0].astype(BF16)
    s_p = _dot_nt(ql, cp) + _dot(qr, krpast_ref[0].astype(BF16))
    s_n = _dot_nt(ql, cn) + _dot_nt(qr, krnew_ref[0].astype(BF16))
    m = jnp.maximum(jnp.max(s_p, axis=-1, keepdims=True), jnp.max(s_n, axis=-1, keepdims=True))
    p_p = jnp.exp2(s_p - m)
    p_n = jnp.exp2(s_n - m)
    l = jnp.sum(p_p, axis=-1, keepdims=True) + jnp.sum(p_n, axis=-1, keepdims=True)
    o_lat = ((_dot(p_p.astype(BF16), cp) + _dot(p_n.astype(BF16), cn)) / l).astype(BF16)
    o = _dot(o_lat[0:l_new, :], wuv_ref[0])
    for hd in range(1, N_HEADS):
        o = o + _dot(o_lat[hd * l_new:(hd + 1) * l_new, :], wuv_ref[hd])
    o_ref[0] = _rms(o, g_ref[...]).astype(BF16)


def _attn_sample(q, c_past, kr_past, c_new, kr_new, layer, lw):
    n, l_new, _ = q.shape
    p_len = c_past.shape[2]
    per_b = lambda rows, w: pl.BlockSpec((1, rows, w), lambda b: (b, 0, 0))
    lay = lambda rows, w: pl.BlockSpec((None, 1, rows, w), lambda b: (layer, b, 0, 0))
    return pl.pallas_call(
        _attn_sample_kernel,
        grid=(n,),
        in_specs=[
            per_b(l_new, D_HEADS_PAD),
            lay(p_len, KV_LORA),
            lay(QK_ROPE, p_len),
            lay(l_new, KV_LORA),
            lay(l_new, QK_ROPE),
            _const_spec((N_HEADS, HEAD_PAD, KV_LORA)),
            _const_spec((HEAD_PAD, QK_ROPE)),
            _const_spec((N_HEADS, KV_LORA, D_ATTN)),
            _const_spec((1, D_ATTN)),
        ],
        out_specs=per_b(l_new, D_ATTN),
        out_shape=jax.ShapeDtypeStruct((n, l_new, D_ATTN), BF16),
        compiler_params=pltpu.CompilerParams(
            dimension_semantics=("arbitrary",),
            vmem_limit_bytes=VMEM_LIMIT),
        name="attn_sample",
    )(q, c_past, kr_past, c_new, kr_new, lw["w_abs"], lw["p_sel"], lw["w_uv"], lw["g_attn"])


def _out_ffn_kernel(x_ref, convn_ref, attn_ref, p_ref, woc_ref, woa_ref, gpost_ref, gffn_ref,
                    wg_ref, wu_ref, wd_ref, gfpost_ref, wpg_ref, wpp_ref, y_ref, act_ref):
    tm = x_ref.shape[0]
    ng = max(1, tm // ROW_GROUP)
    groups = [slice(i * (tm // ng), (i + 1) * (tm // ng)) for i in range(ng)]
    mix = [_dot(convn_ref[r, :], woc_ref[...]) + _dot(attn_ref[r, :], woa_ref[...]) for r in groups]
    x1, h = [], []
    for r, mx in zip(groups, mix):
        x1.append(x_ref[r, :] + _rms(mx, gpost_ref[...]))
        h.append(_rms(x1[-1], gffn_ref[...]).astype(BF16))
    for r, hr in zip(groups, h):
        for c in range(D_FF // FF_CHUNK):
            cols = slice(c * FF_CHUNK, (c + 1) * FF_CHUNK)
            g = _dot(hr, wg_ref[:, cols])
            u = _dot(hr, wu_ref[:, cols])
            act_ref[r, cols] = (g * jax.nn.sigmoid(g) * u).astype(BF16)
    f = [_dot(act_ref[r, :], wd_ref[...]) for r in groups]
    for r, x1r, fr in zip(groups, x1, f):
        pp = _dot(p_ref[r, :].astype(BF16), wpp_ref[...])
        x2 = x1r + _rms(fr, gfpost_ref[...])
        gate = jax.nn.sigmoid(_dot(x2.astype(BF16), wpg_ref[...]))
        y_ref[r, :] = x2 + gate * pp


def _out_ffn(x, convn, attn, p, layer, lw, tm):
    m = x.shape[0]
    tok = lambda w: pl.BlockSpec((tm, w), lambda i: (i, 0))
    return pl.pallas_call(
        _out_ffn_kernel,
        grid=(m // tm,),
        in_specs=[
            tok(D_MODEL), tok(D_CONV), tok(D_ATTN),
            pl.BlockSpec((None, tm, D_PLE), lambda i: (layer, i, 0)),
            _const_spec((D_CONV, D_MODEL)),
            _const_spec((D_ATTN, D_MODEL)),
            _const_spec((1, D_MODEL)),
            _const_spec((1, D_MODEL)),
            _const_spec((D_MODEL, D_FF)),
            _const_spec((D_MODEL, D_FF)),
            _const_spec((D_FF, D_MODEL)),
            _const_spec((1, D_MODEL)),
            _const_spec((D_MODEL, D_MODEL)),
            _const_spec((D_PLE, D_MODEL)),
        ],
        out_specs=tok(D_MODEL),
        out_shape=jax.ShapeDtypeStruct((m, D_MODEL), F32),
        scratch_shapes=[pltpu.VMEM((tm, D_FF), BF16)],
        compiler_params=pltpu.CompilerParams(
            dimension_semantics=("arbitrary",),
            vmem_limit_bytes=VMEM_LIMIT_FFN),
        name="out_ffn",
    )(x, convn, attn, p, lw["w_o_conv"], lw["w_o_attn"], lw["g_mix_post"], lw["g_ffn_pre"],
      lw["w_gate"], lw["w_up"], lw["w_down"], lw["g_ffn_post"], lw["w_ple_gate"], lw["w_ple_proj"])


def _swap_halves(a):
    half = a.shape[-1] // 2
    return jnp.concatenate([a[..., half:], a[..., :half]], axis=-1)


def _layer_weights(i, g_mix_pre, w_in, w_conv, g_q, w_uq, g_kv, w_ukv, g_conv_out, g_attn_out, w_o,
                   g_mix_post, g_ffn_pre, w_ffn_gate, w_ffn_up, w_ffn_down, g_ffn_post,
                   w_ple_proj, w_ple_gate):
    row = lambda g: g[i].reshape(1, -1).astype(F32)
    w_kr = w_in[i][:, OFF_KR:IN_COLS]
    w_in_ext = jnp.concatenate(
        [w_in[i][:, :OFF_KR], jnp.zeros((D_MODEL, QK_NOPE), F32), w_kr, _swap_halves(w_kr)], axis=1)
    wq3 = w_uq[i].reshape(Q_LORA, N_HEADS, QK_DIM)
    w_q = jnp.concatenate(
        [wq3, _swap_halves(wq3[..., QK_NOPE:])], axis=-1).reshape(Q_LORA, D_HEADS_PAD)
    wkv3 = w_ukv[i].reshape(KV_LORA, N_HEADS, QK_NOPE + V_DIM)
    w_uk = wkv3[..., :QK_NOPE]
    w_uv = wkv3[..., QK_NOPE:]
    zpad = jnp.zeros((KV_LORA, N_HEADS, HEAD_PAD - QK_NOPE), F32)
    w_k = jnp.concatenate([w_uk, zpad], axis=-1).reshape(KV_LORA, D_HEADS_PAD)
    w_vt = w_uv.reshape(KV_LORA, D_ATTN).T
    w_abs = jnp.concatenate([w_uk, zpad], axis=-1).transpose(1, 2, 0)
    head_of_col = jnp.arange(D_ATTN) // V_DIM
    w_uv_heads = jnp.where(head_of_col[None, None, :] == jnp.arange(N_HEADS)[:, None, None],
                           w_uv.reshape(1, KV_LORA, D_ATTN), 0.0)
    p_sel = jnp.zeros((HEAD_PAD, QK_ROPE), F32).at[QK_NOPE:QK_DIM, :].set(jnp.eye(QK_ROPE, dtype=F32))
    return dict(
        g_mix_pre=row(g_mix_pre), w_in=w_in_ext.astype(BF16), w_conv=w_conv[i].astype(F32),
        g_q=row(g_q), w_q=w_q.astype(BF16), g_kv=row(g_kv), w_k=w_k.astype(BF16), w_vt=w_vt.astype(BF16),
        g_conv_out=row(g_conv_out), g_attn=row(g_attn_out),
        w_abs=w_abs.astype(BF16), w_uv=w_uv_heads.astype(BF16), p_sel=p_sel.astype(BF16),
        w_o_conv=w_o[i][:D_CONV].astype(BF16), w_o_attn=w_o[i][D_CONV:].astype(BF16),
        g_mix_post=row(g_mix_post), g_ffn_pre=row(g_ffn_pre),
        w_gate=w_ffn_gate[i].astype(BF16), w_up=w_ffn_up[i].astype(BF16), w_down=w_ffn_down[i].astype(BF16),
        g_ffn_post=row(g_ffn_post), w_ple_gate=w_ple_gate[i].astype(BF16), w_ple_proj=w_ple_proj[i].astype(BF16),
    )


def _rope_tables(pos):
    half = QK_ROPE // 2
    inv = ROPE_THETA ** (-jnp.arange(half, dtype=F32) / half)
    ang = pos.astype(F32)[:, None] * inv[None, :]
    cos, sin = jnp.cos(ang), jnp.sin(ang)
    n = pos.shape[0]
    cc = jnp.concatenate([cos, cos], axis=1)
    ss = jnp.concatenate([-sin, sin], axis=1)
    z_lo = jnp.zeros((n, QK_NOPE), F32)
    z_hi = jnp.zeros((n, HEAD_PAD - QK_DIM), F32)
    c_q = jnp.concatenate([jnp.ones((n, QK_NOPE), F32), cc, z_hi], axis=1) * Q_PRESCALE
    s_q = jnp.concatenate([z_lo, ss, z_hi], axis=1)
    c_k = jnp.concatenate([z_lo, cc, z_hi], axis=1)
    return jnp.stack([c_q, s_q * Q_PRESCALE, c_k, s_q])


def _pick_tile(n, cap):
    t = min(n, cap)
    while n % t:
        t //= 2
    return t


def kernel(x_prompt, x_sample, cache_kv_latent, cache_k_rope, state_conv, p_prompt, p_sample, g_mix_pre, w_in, w_conv, g_q, w_uq, g_kv, w_ukv, g_conv_out, g_attn_out, w_o, g_mix_post, g_ffn_pre, w_ffn_gate, w_ffn_up, w_ffn_down, g_ffn_post, w_ple_proj, w_ple_gate):
    depth = w_in.shape[0]
    n_p, s_p, _ = x_prompt.shape
    n_d, s_d, _ = x_sample.shape
    past_len = cache_kv_latent.shape[2]
    tab_p = _rope_tables(jnp.arange(s_p))
    tab_d = _rope_tables(past_len + jnp.arange(s_d))
    conv_zero = jnp.zeros((1, n_p, CONV_W - 1, D_CONV), F32)
    ts_p = _pick_tile(s_p, SEQ_TILE)
    tq = _pick_tile(s_p, Q_TILE)
    tm_p = _pick_tile(n_p * s_p, TOKEN_TILE)
    tm_d = _pick_tile(n_d * s_d, TOKEN_TILE)
    pp_flat = p_prompt.reshape(depth, n_p * s_p, D_PLE)
    pd_flat = p_sample.reshape(depth, n_d * s_d, D_PLE)

    cache_kr_t = jnp.swapaxes(cache_k_rope, 2, 3)
    xp, xd = x_prompt, x_sample
    stack_p = stack_d = None
    for i in range(depth):
        lw = _layer_weights(i, g_mix_pre, w_in, w_conv, g_q, w_uq, g_kv, w_ukv, g_conv_out,
                            g_attn_out, w_o, g_mix_post, g_ffn_pre, w_ffn_gate, w_ffn_up,
                            w_ffn_down, g_ffn_post, w_ple_proj, w_ple_gate)
        convn, q, *stack_p, k, vt = _in_proj(xp, tab_p, conv_zero, 0, lw, i, depth, ts_p,
                                             tk=Q_TILES * tq, stacked=stack_p)
        attn = _attn_prompt(q, k, vt, lw["g_attn"], tq)
        xp = _out_ffn(xp.reshape(n_p * s_p, D_MODEL), convn.reshape(n_p * s_p, D_CONV),
                      attn.reshape(n_p * s_p, D_ATTN), pp_flat, i, lw, tm_p).reshape(n_p, s_p, D_MODEL)
        convn, q, *stack_d = _in_proj(xd, tab_d, state_conv, i, lw, i, depth, s_d, stacked=stack_d)
        attn = _attn_sample(q, cache_kv_latent, cache_kr_t, stack_d[0], stack_d[1], i, lw)
        xd = _out_ffn(xd.reshape(n_d * s_d, D_MODEL), convn.reshape(n_d * s_d, D_CONV),
                      attn.reshape(n_d * s_d, D_ATTN), pd_flat, i, lw, tm_d).reshape(n_d, s_d, D_MODEL)

    return (xp, xd, *stack_p, *stack_d)
```

```python
import functools

import jax
import jax.numpy as jnp
from jax import lax
from jax.experimental import pallas as pl
from jax.experimental.pallas import tpu as pltpu

D_MODEL = 1024
CHUNK = 64
D_PLE = 256
D_CONV = 512
CONV_W = 3
V_DIM = 64
QK_NOPE = 64
QK_ROPE = 32
QK_DIM = QK_NOPE + QK_ROPE
N_HEADS = 8
D_ATTN = N_HEADS * V_DIM
Q_LORA = 768
KV_LORA = 256
D_FF = 2816
ROPE_THETA = 10000.0
EPS = 1e-6
SCALE = QK_DIM ** -0.5

OFF_C = D_CONV
OFF_X = 2 * D_CONV
OFF_Q = 3 * D_CONV
OFF_KV = OFF_Q + Q_LORA
OFF_KR = OFF_KV + KV_LORA
IN_COLS = OFF_KR + QK_ROPE

LANES = 128
SUBLANES = 8
HEAD_PAD = LANES
D_HEADS_PAD = N_HEADS * HEAD_PAD
IN_EXT = OFF_KR + LANES
FF_CHUNK = 256
NEG_BIG = -1e30
Q_PRESCALE = SCALE * 1.4426950408889634
Q_TILES = 2
SCORE_LEAD = 3
ROW_GROUP = 256
SUM_ROWS = 16
SEQ_TILE = 1024
Q_TILE = 256
TOKEN_TILE = 512
VMEM_LIMIT = 48 * 1024 * 1024
VMEM_LIMIT_FFN = 56 * 1024 * 1024

BF16 = jnp.bfloat16
F32 = jnp.float32


def _rms(x, g):
    ms = jnp.mean(x * x, axis=-1, keepdims=True)
    return x * lax.rsqrt(ms + EPS) * g


def _dot(a, b):
    return jnp.dot(a, b, preferred_element_type=F32)


def _dot_nt(a, b):
    return lax.dot_general(a, b, (((1,), (1,)), ((), ())), preferred_element_type=F32)


def _const_spec(shape):
    nd = len(shape)
    return pl.BlockSpec(shape, lambda *_: (0,) * nd, pipeline_mode=pl.Buffered(1))


def _in_proj_kernel(x_ref, tab_ref, cinit_ref, gpre_ref, win_ref, wconv_ref, gq_ref, wq_ref,
                    gkv_ref, gconv_ref, *rest, with_kv, n_alias, layer):
    if with_kv:
        wk_ref, wvt_ref = rest[:2]
        rest = rest[2:]
    rest = rest[n_alias:]
    if with_kv:
        convn_ref, q_ref, lat_ref, kr_ref, newconv_ref, k_ref, vt_ref, carry_ref = rest
    else:
        convn_ref, q_ref, lat_ref, kr_ref, newconv_ref, carry_ref = rest
    ts = x_ref.shape[1]
    s_idx = pl.program_id(1)

    def store_layered(ref, rows, val):
        if n_alias:
            ref[0, rows, :] = val
        else:
            for d in range(ref.shape[0]):
                ref[d, 0, rows, :] = val if d == layer else jnp.zeros_like(val)

    @pl.when(s_idx == 0)
    def _():
        carry_ref[0:2, :] = cinit_ref[0]

    ng = max(1, ts // ROW_GROUP)
    gs = ts // ng
    groups = [slice(i * gs, (i + 1) * gs) for i in range(ng)]
    z = []
    for r in groups:
        h = _rms(x_ref[0, r, :], gpre_ref[...]).astype(BF16)
        z.append([_dot(h, win_ref[:, a:b]) for a, b in
                  ((0, OFF_C), (OFF_C, OFF_X), (OFF_X, OFF_Q), (OFF_Q, OFF_KV), (OFF_KV, OFF_KR),
                   (OFF_KR, IN_EXT))])
    cat = lambda j: z[0][j] if ng == 1 else jnp.concatenate([zg[j] for zg in z], axis=0)

    assert CONV_W == 3
    gb = cat(0)
    u = cat(1) * cat(2)
    row = lax.broadcasted_iota(jnp.int32, (ts, 1), 0)
    c0 = carry_ref[0:1, :]
    c1 = carry_ref[1:2, :]
    u_m1 = jnp.where(row == 0, c1, pltpu.roll(u, 1, axis=0))
    u_m2 = pltpu.roll(u, 2, axis=0)
    u_m2 = jnp.where(row == 0, c0, jnp.where(row == 1, c1, u_m2))
    conv = u_m2 * wconv_ref[0:1, :] + u_m1 * wconv_ref[1:2, :] + u * wconv_ref[2:3, :]
    convn_ref[0] = _rms(gb * conv, gconv_ref[...]).astype(BF16)
    tail = u[ts - 2:ts, :]
    carry_ref[0:2, :] = tail
    store_layered(newconv_ref, slice(None), tail)

    for gi, r in enumerate(groups):
        c_q = tab_ref[0, r, :]
        s_q = tab_ref[1, r, :]
        c_k = tab_ref[2, r, :]
        s_k = tab_ref[3, r, :]

        cqn = _rms(z[gi][3], gq_ref[...]).astype(BF16)
        qa = _dot(cqn, wq_ref[...])
        for hd in range(N_HEADS):
            blk = qa[:, hd * HEAD_PAD:(hd + 1) * HEAD_PAD]
            rot = blk * c_q + pltpu.roll(blk, HEAD_PAD - QK_ROPE, axis=1) * s_q
            q_ref[0, r, hd * HEAD_PAD:(hd + 1) * HEAD_PAD] = rot.astype(BF16)

        c_kv = _rms(z[gi][4], gkv_ref[...])
        store_layered(lat_ref, r, c_kv)
        ckvb = c_kv.astype(BF16)
        zkr = z[gi][5]
        kr_blk = zkr * c_k + pltpu.roll(zkr, HEAD_PAD - QK_ROPE, axis=1) * s_k
        store_layered(kr_ref, r, kr_blk[:, QK_NOPE:QK_DIM])
        if with_kv:
            kn = _dot(ckvb, wk_ref[...])
            for hd in range(N_HEADS):
                k_ref[0, r, hd * HEAD_PAD:(hd + 1) * HEAD_PAD] = (
                    kn[:, hd * HEAD_PAD:(hd + 1) * HEAD_PAD] + kr_blk).astype(BF16)
            vt = _dot_nt(wvt_ref[...], ckvb).astype(BF16)
            tk = vt_ref.shape[3]
            start = gi * gs
            vt_ref[0, start // tk, :, start % tk:start % tk + gs] = vt


def _in_proj(x, tables, conv_init, conv_layer, lw, layer, depth, ts, tk=None, stacked=None):
    n, s, _ = x.shape
    with_kv = tk is not None
    grid = (n, s // ts)
    tok = lambda w: pl.BlockSpec((1, ts, w), lambda b, i: (b, i, 0))

    def lay(rows, w, tiled):
        slab, first = (None, layer) if stacked is not None else (depth, 0)
        return pl.BlockSpec((slab, 1, rows, w),
                            (lambda b, i: (first, b, i, 0)) if tiled else (lambda b, i: (first, b, 0, 0)))
    out_shape = [
        jax.ShapeDtypeStruct((n, s, D_CONV), BF16),
        jax.ShapeDtypeStruct((n, s, D_HEADS_PAD), BF16),
        jax.ShapeDtypeStruct((depth, n, s, KV_LORA), F32),
        jax.ShapeDtypeStruct((depth, n, s, QK_ROPE), F32),
        jax.ShapeDtypeStruct((depth, n, CONV_W - 1, D_CONV), F32),
    ]
    in_specs = [
        tok(D_MODEL),
        pl.BlockSpec((4, ts, LANES), lambda b, i: (0, i, 0)),
        pl.BlockSpec((None, 1, CONV_W - 1, D_CONV), lambda b, i: (conv_layer, b, 0, 0)),
        _const_spec((1, D_MODEL)),
        _const_spec((D_MODEL, IN_EXT)),
        _const_spec((CONV_W, D_CONV)),
        _const_spec((1, Q_LORA)),
        _const_spec((Q_LORA, D_HEADS_PAD)),
        _const_spec((1, KV_LORA)),
        _const_spec((1, D_CONV)),
    ]
    out_specs = [
        tok(D_CONV), tok(D_HEADS_PAD), lay(ts, KV_LORA, True), lay(ts, QK_ROPE, True),
        lay(CONV_W - 1, D_CONV, False),
    ]
    args = [x, tables, conv_init, lw["g_mix_pre"], lw["w_in"], lw["w_conv"], lw["g_q"], lw["w_q"],
            lw["g_kv"], lw["g_conv_out"]]
    if with_kv:
        in_specs += [_const_spec((KV_LORA, D_HEADS_PAD)), _const_spec((D_ATTN, KV_LORA))]
        args += [lw["w_k"], lw["w_vt"]]
        out_shape += [jax.ShapeDtypeStruct((n, s, D_HEADS_PAD), BF16),
                      jax.ShapeDtypeStruct((n, s // tk, D_ATTN, tk), BF16)]
        out_specs += [tok(D_HEADS_PAD),
                      pl.BlockSpec((1, ts // tk, D_ATTN, tk), lambda b, i: (b, i, 0, 0))]
    aliases = {}
    if stacked is not None:
        for j, arr in enumerate(stacked):
            aliases[len(args)] = 2 + j
            in_specs.append(pl.BlockSpec(memory_space=pl.ANY))
            args.append(arr)
    return pl.pallas_call(
        functools.partial(_in_proj_kernel, with_kv=with_kv, n_alias=len(aliases), layer=layer),
        grid=grid,
        in_specs=in_specs,
        out_specs=tuple(out_specs),
        out_shape=tuple(out_shape),
        input_output_aliases=aliases,
        scratch_shapes=[pltpu.VMEM((SUBLANES, D_CONV), F32)],
        compiler_params=pltpu.CompilerParams(
            dimension_semantics=("arbitrary", "arbitrary"),
            vmem_limit_bytes=VMEM_LIMIT),
        name="in_proj_kv" if with_kv else "in_proj",
    )(*args)


def _attn_prompt_kernel(q_ref, k_ref, vt_ref, g_ref, o_ref, acc_ref, m_ref, ot_ref, *, tq):
    tk = vt_ref.shape[3]
    n_pairs = vt_ref.shape[1]
    items = [(t, hd) for hd in range(N_HEADS) for t in range(Q_TILES)]

    def scores(slot, t, kb, hd, width):
        cols = slice(hd * HEAD_PAD, (hd + 1) * HEAD_PAD)
        k0 = kb * tk
        q0 = slot * tk + t * tq
        return _dot_nt(k_ref[0, k0:k0 + width, cols], q_ref[0, q0:q0 + tq, cols])

    def visible(t, width):
        ck = lax.broadcasted_iota(jnp.int32, (width, tq), 0) // CHUNK
        rq = (lax.broadcasted_iota(jnp.int32, (width, tq), 1) + t * tq) // CHUNK
        return ck <= rq

    def softmax(slot, t, hd, st, vis, first):
        if vis is not None:
            st = jnp.where(vis, st, NEG_BIG)
        m_new = jnp.max(st, axis=0, keepdims=True)
        alpha = None
        if not first:
            m_prev = m_ref[slot, t, hd:hd + 1, :]
            m_new = jnp.maximum(m_prev, m_new)
            alpha = jnp.exp2(m_prev - m_new)
        m_ref[slot, t, hd:hd + 1, :] = m_new
        return alpha, jnp.exp2(st - m_new).astype(BF16)

    def weighted_values(slot, t, hd, kb, alpha, pt):
        width = pt.shape[0]
        lhs = jnp.concatenate([vt_ref[0, kb, hd * V_DIM:(hd + 1) * V_DIM, 0:width],
                               jnp.ones((SUM_ROWS, width), BF16)], axis=0)
        pv = _dot(lhs, pt)
        acc_ref[slot, t, hd] = pv if alpha is None else alpha * acc_ref[slot, t, hd] + pv

    def write_out(slot):
        for t in range(Q_TILES):
            ssq = jnp.zeros((1, tq), F32)
            for hd in range(N_HEADS):
                a = acc_ref[slot, t, hd]
                o = a[0:V_DIM, :] / a[V_DIM:V_DIM + 1, :]
                ssq = ssq + jnp.sum(o * o, axis=0, keepdims=True)
                ot_ref[slot % 2, t, hd * V_DIM:(hd + 1) * V_DIM, :] = o
            rs = lax.rsqrt(ssq * (1.0 / D_ATTN) + EPS)
            q0 = slot * tk + t * tq
            o_ref[0, q0:q0 + tq, :] = ((ot_ref[slot % 2, t] * rs).T * g_ref[...]).astype(BF16)

    work = []
    for slot in range(n_pairs):
        for kb in range(slot + 1):
            widths = (tk, tk) if kb < slot else (tq, tk)
            work += [(slot, kb, t, hd, widths[t], kb == slot) for t, hd in items]

    sts = {}
    for n in range(len(work) + SCORE_LEAD):
        if n < len(work):
            slot, kb, t, hd, width, _ = work[n]
            sts[n] = scores(slot, t, kb, hd, width)
        if n >= SCORE_LEAD:
            slot, kb, t, hd, width, masked = work[n - SCORE_LEAD]
            if masked and (t, hd) == items[0]:
                vis = [visible(0, tq), visible(1, tk)]
                if slot > 0:
                    write_out(slot - 1)
            weighted_values(slot, t, hd, kb,
                            *softmax(slot, t, hd, sts.pop(n - SCORE_LEAD), vis[t] if masked else None,
                                     kb == 0))
    write_out(n_pairs - 1)


def _attn_prompt(q, k, vt, g_attn, tq):
    n, s, _ = q.shape
    tk = vt.shape[3]
    n_pairs = s // tk
    assert tk == Q_TILES * tq and s % tk == 0
    row = lambda w: pl.BlockSpec((1, s, w), lambda b: (b, 0, 0))
    return pl.pallas_call(
        functools.partial(_attn_prompt_kernel, tq=tq),
        grid=(n,),
        in_specs=[
            row(D_HEADS_PAD),
            row(D_HEADS_PAD),
            pl.BlockSpec((1, n_pairs, D_ATTN, tk), lambda b: (b, 0, 0, 0)),
            _const_spec((1, D_ATTN)),
        ],
        out_specs=row(D_ATTN),
        out_shape=jax.ShapeDtypeStruct((n, s, D_ATTN), BF16),
        scratch_shapes=[pltpu.VMEM((n_pairs, Q_TILES, N_HEADS, V_DIM + SUM_ROWS, tq), F32),
                        pltpu.VMEM((n_pairs, Q_TILES, N_HEADS, tq), F32),
                        pltpu.VMEM((2, Q_TILES, D_ATTN, tq), F32)],
        compiler_params=pltpu.CompilerParams(
            dimension_semantics=("arbitrary",),
            vmem_limit_bytes=VMEM_LIMIT),
        name="attn_prompt",
    )(q, k, vt, g_attn)


def _attn_sample_kernel(q_ref, cpast_ref, krpast_ref, cnew_ref, krnew_ref, wabs_ref, psel_ref,
                        wuv_ref, g_ref, o_ref):
    l_new = q_ref.shape[1]
    q = q_ref[0]
    ql, qr = [], []
    for hd in range(N_HEADS):
        qh = q[:, hd * HEAD_PAD:(hd + 1) * HEAD_PAD]
        ql.append(_dot(qh, wabs_ref[hd]))
        qr.append(_dot(qh, psel_ref[...]))
    ql = jnp.concatenate(ql, axis=0).astype(BF16)
    qr = jnp.concatenate(qr, axis=0).astype(BF16)
    cp = cpast_ref[0].astype(BF16)
    cn = cnew_ref[0].astype(BF16)
    s_p = _dot_nt(ql, cp) + _dot(qr, krpast_ref[0].astype(BF16))
    s_n = _dot_nt(ql, cn) + _dot_nt(qr, krnew_ref[0].astype(BF16))
    m = jnp.maximum(jnp.max(s_p, axis=-1, keepdims=True), jnp.max(s_n, axis=-1, keepdims=True))
    p_p = jnp.exp2(s_p - m)
    p_n = jnp.exp2(s_n - m)
    l = jnp.sum(p_p, axis=-1, keepdims=True) + jnp.sum(p_n, axis=-1, keepdims=True)
    o_lat = ((_dot(p_p.astype(BF16), cp) + _dot(p_n.astype(BF16), cn)) / l).astype(BF16)
    o = _dot(o_lat[0:l_new, :], wuv_ref[0])
    for hd in range(1, N_HEADS):
        o = o + _dot(o_lat[hd * l_new:(hd + 1) * l_new, :], wuv_ref[hd])
    o_ref[0] = _rms(o, g_ref[...]).astype(BF16)


def _attn_sample(q, c_past, kr_past, c_new, kr_new, layer, lw):
    n, l_new, _ = q.shape
    p_len = c_past.shape[2]
    per_b = lambda rows, w: pl.BlockSpec((1, rows, w), lambda b: (b, 0, 0))
    lay = lambda rows, w: pl.BlockSpec((None, 1, rows, w), lambda b: (layer, b, 0, 0))
    return pl.pallas_call(
        _attn_sample_kernel,
        grid=(n,),
        in_specs=[
            per_b(l_new, D_HEADS_PAD),
            lay(p_len, KV_LORA),
            lay(QK_ROPE, p_len),
            lay(l_new, KV_LORA),
            lay(l_new, QK_ROPE),
            _const_spec((N_HEADS, HEAD_PAD, KV_LORA)),
            _const_spec((HEAD_PAD, QK_ROPE)),
            _const_spec((N_HEADS, KV_LORA, D_ATTN)),
            _const_spec((1, D_ATTN)),
        ],
        out_specs=per_b(l_new, D_ATTN),
        out_shape=jax.ShapeDtypeStruct((n, l_new, D_ATTN), BF16),
        compiler_params=pltpu.CompilerParams(
            dimension_semantics=("arbitrary",),
            vmem_limit_bytes=VMEM_LIMIT),
        name="attn_sample",
    )(q, c_past, kr_past, c_new, kr_new, lw["w_abs"], lw["p_sel"], lw["w_uv"], lw["g_attn"])


def _out_ffn_kernel(x_ref, convn_ref, attn_ref, p_ref, woc_ref, woa_ref, gpost_ref, gffn_ref,
                    wg_ref, wu_ref, wd_ref, gfpost_ref, wpg_ref, wpp_ref, y_ref, act_ref):
    tm = x_ref.shape[0]
    ng = max(1, tm // ROW_GROUP)
    groups = [slice(i * (tm // ng), (i + 1) * (tm // ng)) for i in range(ng)]
    mix = [_dot(convn_ref[r, :], woc_ref[...]) + _dot(attn_ref[r, :], woa_ref[...]) for r in groups]
    x1, h = [], []
    for r, mx in zip(groups, mix):
        x1.append(x_ref[r, :] + _rms(mx, gpost_ref[...]))
        h.append(_rms(x1[-1], gffn_ref[...]).astype(BF16))
    for r, hr in zip(groups, h):
        for c in range(D_FF // FF_CHUNK):
            cols = slice(c * FF_CHUNK, (c + 1) * FF_CHUNK)
            g = _dot(hr, wg_ref[:, cols])
            u = _dot(hr, wu_ref[:, cols])
            act_ref[r, cols] = (g * jax.nn.sigmoid(g) * u).astype(BF16)
    f = [_dot(act_ref[r, :], wd_ref[...]) for r in groups]
    for r, x1r, fr in zip(groups, x1, f):
        pp = _dot(p_ref[r, :].astype(BF16), wpp_ref[...])
        x2 = x1r + _rms(fr, gfpost_ref[...])
        gate = jax.nn.sigmoid(_dot(x2.astype(BF16), wpg_ref[...]))
        y_ref[r, :] = x2 + gate * pp


def _out_ffn(x, convn, attn, p, layer, lw, tm):
    m = x.shape[0]
    tok = lambda w: pl.BlockSpec((tm, w), lambda i: (i, 0))
    return pl.pallas_call(
        _out_ffn_kernel,
        grid=(m // tm,),
        in_specs=[
            tok(D_MODEL), tok(D_CONV), tok(D_ATTN),
            pl.BlockSpec((None, tm, D_PLE), lambda i: (layer, i, 0)),
            _const_spec((D_CONV, D_MODEL)),
            _const_spec((D_ATTN, D_MODEL)),
            _const_spec((1, D_MODEL)),
            _const_spec((1, D_MODEL)),
            _const_spec((D_MODEL, D_FF)),
            _const_spec((D_MODEL, D_FF)),
            _const_spec((D_FF, D_MODEL)),
            _const_spec((1, D_MODEL)),
            _const_spec((D_MODEL, D_MODEL)),
            _const_spec((D_PLE, D_MODEL)),
        ],
        out_specs=tok(D_MODEL),
        out_shape=jax.ShapeDtypeStruct((m, D_MODEL), F32),
        scratch_shapes=[pltpu.VMEM((tm, D_FF), BF16)],
        compiler_params=pltpu.CompilerParams(
            dimension_semantics=("arbitrary",),
            vmem_limit_bytes=VMEM_LIMIT_FFN),
        name="out_ffn",
    )(x, convn, attn, p, lw["w_o_conv"], lw["w_o_attn"], lw["g_mix_post"], lw["g_ffn_pre"],
      lw["w_gate"], lw["w_up"], lw["w_down"], lw["g_ffn_post"], lw["w_ple_gate"], lw["w_ple_proj"])


def _swap_halves(a):
    half = a.shape[-1] // 2
    return jnp.concatenate([a[..., half:], a[..., :half]], axis=-1)


def _layer_weights(i, g_mix_pre, w_in, w_conv, g_q, w_uq, g_kv, w_ukv, g_conv_out, g_attn_out, w_o,
                   g_mix_post, g_ffn_pre, w_ffn_gate, w_ffn_up, w_ffn_down, g_ffn_post,
                   w_ple_proj, w_ple_gate):
    row = lambda g: g[i].reshape(1, -1).astype(F32)
    w_kr = w_in[i][:, OFF_KR:IN_COLS]
    w_in_ext = jnp.concatenate(
        [w_in[i][:, :OFF_KR], jnp.zeros((D_MODEL, QK_NOPE), F32), w_kr, _swap_halves(w_kr)], axis=1)
    wq3 = w_uq[i].reshape(Q_LORA, N_HEADS, QK_DIM)
    w_q = jnp.concatenate(
        [wq3, _swap_halves(wq3[..., QK_NOPE:])], axis=-1).reshape(Q_LORA, D_HEADS_PAD)
    wkv3 = w_ukv[i].reshape(KV_LORA, N_HEADS, QK_NOPE + V_DIM)
    w_uk = wkv3[..., :QK_NOPE]
    w_uv = wkv3[..., QK_NOPE:]
    zpad = jnp.zeros((KV_LORA, N_HEADS, HEAD_PAD - QK_NOPE), F32)
    w_k = jnp.concatenate([w_uk, zpad], axis=-1).reshape(KV_LORA, D_HEADS_PAD)
    w_vt = w_uv.reshape(KV_LORA, D_ATTN).T
    w_abs = jnp.concatenate([w_uk, zpad], axis=-1).transpose(1, 2, 0)
    head_of_col = jnp.arange(D_ATTN) // V_DIM
    w_uv_heads = jnp.where(head_of_col[None, None, :] == jnp.arange(N_HEADS)[:, None, None],
                           w_uv.reshape(1, KV_LORA, D_ATTN), 0.0)
    p_sel = jnp.zeros((HEAD_PAD, QK_ROPE), F32).at[QK_NOPE:QK_DIM, :].set(jnp.eye(QK_ROPE, dtype=F32))
    return dict(
        g_mix_pre=row(g_mix_pre), w_in=w_in_ext.astype(BF16), w_conv=w_conv[i].astype(F32),
        g_q=row(g_q), w_q=w_q.astype(BF16), g_kv=row(g_kv), w_k=w_k.astype(BF16), w_vt=w_vt.astype(BF16),
        g_conv_out=row(g_conv_out), g_attn=row(g_attn_out),
        w_abs=w_abs.astype(BF16), w_uv=w_uv_heads.astype(BF16), p_sel=p_sel.astype(BF16),
        w_o_conv=w_o[i][:D_CONV].astype(BF16), w_o_attn=w_o[i][D_CONV:].astype(BF16),
        g_mix_post=row(g_mix_post), g_ffn_pre=row(g_ffn_pre),
        w_gate=w_ffn_gate[i].astype(BF16), w_up=w_ffn_up[i].astype(BF16), w_down=w_ffn_down[i].astype(BF16),
        g_ffn_post=row(g_ffn_post), w_ple_gate=w_ple_gate[i].astype(BF16), w_ple_proj=w_ple_proj[i].astype(BF16),
    )


def _rope_tables(pos):
    half = QK_ROPE // 2
    inv = ROPE_THETA ** (-jnp.arange(half, dtype=F32) / half)
    ang = pos.astype(F32)[:, None] * inv[None, :]
    cos, sin = jnp.cos(ang), jnp.sin(ang)
    n = pos.shape[0]
    cc = jnp.concatenate([cos, cos], axis=1)
    ss = jnp.concatenate([-sin, sin], axis=1)
    z_lo = jnp.zeros((n, QK_NOPE), F32)
    z_hi = jnp.zeros((n, HEAD_PAD - QK_DIM), F32)
    c_q = jnp.concatenate([jnp.ones((n, QK_NOPE), F32), cc, z_hi], axis=1) * Q_PRESCALE
    s_q = jnp.concatenate([z_lo, ss, z_hi], axis=1)
    c_k = jnp.concatenate([z_lo, cc, z_hi], axis=1)
    return jnp.stack([c_q, s_q * Q_PRESCALE, c_k, s_q])


def _pick_tile(n, cap):
    t = min(n, cap)
    while n % t:
        t //= 2
    return t


def kernel(x_prompt, x_sample, cache_kv_latent, cache_k_rope, state_conv, p_prompt, p_sample, g_mix_pre, w_in, w_conv, g_q, w_uq, g_kv, w_ukv, g_conv_out, g_attn_out, w_o, g_mix_post, g_ffn_pre, w_ffn_gate, w_ffn_up, w_ffn_down, g_ffn_post, w_ple_proj, w_ple_gate):
    depth = w_in.shape[0]
    n_p, s_p, _ = x_prompt.shape
    n_d, s_d, _ = x_sample.shape
    past_len = cache_kv_latent.shape[2]
    tab_p = _rope_tables(jnp.arange(s_p))
    tab_d = _rope_tables(past_len + jnp.arange(s_d))
    conv_zero = jnp.zeros((1, n_p, CONV_W - 1, D_CONV), F32)
    ts_p = _pick_tile(s_p, SEQ_TILE)
    tq = _pick_tile(s_p, Q_TILE)
    tm_p = _pick_tile(n_p * s_p, TOKEN_TILE)
    tm_d = _pick_tile(n_d * s_d, TOKEN_TILE)
    pp_flat = p_prompt.reshape(depth, n_p * s_p, D_PLE)
    pd_flat = p_sample.reshape(depth, n_d * s_d, D_PLE)

    cache_kr_t = jnp.swapaxes(cache_k_rope, 2, 3)
    xp, xd = x_prompt, x_sample
    stack_p = stack_d = None
    for i in range(depth):
        lw = _layer_weights(i, g_mix_pre, w_in, w_conv, g_q, w_uq, g_kv, w_ukv, g_conv_out,
                            g_attn_out, w_o, g_mix_post, g_ffn_pre, w_ffn_gate, w_ffn_up,
                            w_ffn_down, g_ffn_post, w_ple_proj, w_ple_gate)
        convn, q, *stack_p, k, vt = _in_proj(xp, tab_p, conv_zero, 0, lw, i, depth, ts_p,
                                             tk=Q_TILES * tq, stacked=stack_p)
        attn = _attn_prompt(q, k, vt, lw["g_attn"], tq)
        xp = _out_ffn(xp.reshape(n_p * s_p, D_MODEL), convn.reshape(n_p * s_p, D_CONV),
                      attn.reshape(n_p * s_p, D_ATTN), pp_flat, i, lw, tm_p).reshape(n_p, s_p, D_MODEL)
        convn, q, *stack_d = _in_proj(xd, tab_d, state_conv, i, lw, i, depth, s_d, stacked=stack_d)
        attn = _attn_sample(q, cache_kv_latent, cache_kr_t, stack_d[0], stack_d[1], i, lw)
        xd = _out_ffn(xd.reshape(n_d * s_d, D_MODEL), convn.reshape(n_d * s_d, D_CONV),
                      attn.reshape(n_d * s_d, D_ATTN), pd_flat, i, lw, tm_d).reshape(n_d, s_d, D_MODEL)

    return (xp, xd, *stack_p, *stack_d)
```

```python
import functools

import jax
import jax.numpy as jnp
from jax import lax
from jax.experimental import pallas as pl
from jax.experimental.pallas import tpu as pltpu

D_MODEL = 1024
CHUNK = 64
D_PLE = 256
D_CONV = 512
CONV_W = 3
V_DIM = 64
QK_NOPE = 64
QK_ROPE = 32
QK_DIM = QK_NOPE + QK_ROPE
N_HEADS = 8
D_ATTN = N_HEADS * V_DIM
Q_LORA = 768
KV_LORA = 256
D_FF = 2816
ROPE_THETA = 10000.0
EPS = 1e-6
SCALE = QK_DIM ** -0.5

OFF_C = D_CONV
OFF_X = 2 * D_CONV
OFF_Q = 3 * D_CONV
OFF_KV = OFF_Q + Q_LORA
OFF_KR = OFF_KV + KV_LORA
IN_COLS = OFF_KR + QK_ROPE

LANES = 128
SUBLANES = 8
HEAD_PAD = LANES
D_HEADS_PAD = N_HEADS * HEAD_PAD
IN_EXT = OFF_KR + LANES
FF_CHUNK = 256
NEG_BIG = -1e30
Q_PRESCALE = SCALE * 1.4426950408889634
Q_TILES = 2
SCORE_LEAD = 3
ROW_GROUP = 256
SUM_ROWS = 16
SEQ_TILE = 1024
Q_TILE = 256
TOKEN_TILE = 1024
VMEM_LIMIT = 48 * 1024 * 1024
VMEM_LIMIT_FFN = 56 * 1024 * 1024

BF16 = jnp.bfloat16
F32 = jnp.float32


def _rms(x, g):
    ms = jnp.mean(x * x, axis=-1, keepdims=True)
    return x * lax.rsqrt(ms + EPS) * g


def _dot(a, b):
    return jnp.dot(a, b, preferred_element_type=F32)


def _dot_nt(a, b):
    return lax.dot_general(a, b, (((1,), (1,)), ((), ())), preferred_element_type=F32)


def _const_spec(shape):
    nd = len(shape)
    return pl.BlockSpec(shape, lambda *_: (0,) * nd, pipeline_mode=pl.Buffered(1))


def _in_proj_kernel(x_ref, tab_ref, cinit_ref, gpre_ref, win_ref, wconv_ref, gq_ref, wq_ref,
                    gkv_ref, gconv_ref, *rest, with_kv, n_alias, layer):
    if with_kv:
        wk_ref, wvt_ref = rest[:2]
        rest = rest[2:]
    rest = rest[n_alias:]
    if with_kv:
        convn_ref, q_ref, lat_ref, kr_ref, newconv_ref, k_ref, vt_ref, carry_ref = rest
    else:
        convn_ref, q_ref, lat_ref, kr_ref, newconv_ref, carry_ref = rest
    ts = x_ref.shape[1]
    s_idx = pl.program_id(1)

    def store_layered(ref, rows, val):
        if n_alias:
            ref[0, rows, :] = val
        else:
            for d in range(ref.shape[0]):
                ref[d, 0, rows, :] = val if d == layer else jnp.zeros_like(val)

    @pl.when(s_idx == 0)
    def _():
        carry_ref[0:2, :] = cinit_ref[0]

    ng = max(1, ts // ROW_GROUP)
    gs = ts // ng
    groups = [slice(i * gs, (i + 1) * gs) for i in range(ng)]
    z = []
    for r in groups:
        h = _rms(x_ref[0, r, :], gpre_ref[...]).astype(BF16)
        z.append([_dot(h, win_ref[:, a:b]) for a, b in
                  ((0, OFF_C), (OFF_C, OFF_X), (OFF_X, OFF_Q), (OFF_Q, OFF_KV), (OFF_KV, OFF_KR),
                   (OFF_KR, IN_EXT))])
    cat = lambda j: z[0][j] if ng == 1 else jnp.concatenate([zg[j] for zg in z], axis=0)

    assert CONV_W == 3
    gb = cat(0)
    u = cat(1) * cat(2)
    row = lax.broadcasted_iota(jnp.int32, (ts, 1), 0)
    c0 = carry_ref[0:1, :]
    c1 = carry_ref[1:2, :]
    u_m1 = jnp.where(row == 0, c1, pltpu.roll(u, 1, axis=0))
    u_m2 = pltpu.roll(u, 2, axis=0)
    u_m2 = jnp.where(row == 0, c0, jnp.where(row == 1, c1, u_m2))
    conv = u_m2 * wconv_ref[0:1, :] + u_m1 * wconv_ref[1:2, :] + u * wconv_ref[2:3, :]
    convn_ref[0] = _rms(gb * conv, gconv_ref[...]).astype(BF16)
    tail = u[ts - 2:ts, :]
    carry_ref[0:2, :] = tail
    store_layered(newconv_ref, slice(None), tail)

    for gi, r in enumerate(groups):
        c_q = tab_ref[0, r, :]
        s_q = tab_ref[1, r, :]
        c_k = tab_ref[2, r, :]
        s_k = tab_ref[3, r, :]

        cqn = _rms(z[gi][3], gq_ref[...]).astype(BF16)
        qa = _dot(cqn, wq_ref[...])
        for hd in range(N_HEADS):
            blk = qa[:, hd * HEAD_PAD:(hd + 1) * HEAD_PAD]
            rot = blk * c_q + pltpu.roll(blk, HEAD_PAD - QK_ROPE, axis=1) * s_q
            q_ref[0, r, hd * HEAD_PAD:(hd + 1) * HEAD_PAD] = rot.astype(BF16)

        c_kv = _rms(z[gi][4], gkv_ref[...])
        store_layered(lat_ref, r, c_kv)
        ckvb = c_kv.astype(BF16)
        zkr = z[gi][5]
        kr_blk = zkr * c_k + pltpu.roll(zkr, HEAD_PAD - QK_ROPE, axis=1) * s_k
        store_layered(kr_ref, r, kr_blk[:, QK_NOPE:QK_DIM])
        if with_kv:
            kn = _dot(ckvb, wk_ref[...])
            for hd in range(N_HEADS):
                k_ref[0, r, hd * HEAD_PAD:(hd + 1) * HEAD_PAD] = (
                    kn[:, hd * HEAD_PAD:(hd + 1) * HEAD_PAD] + kr_blk).astype(BF16)
            vt = _dot_nt(wvt_ref[...], ckvb).astype(BF16)
            tk = vt_ref.shape[3]
            start = gi * gs
            vt_ref[0, start // tk, :, start % tk:start % tk + gs] = vt


def _in_proj(x, tables, conv_init, conv_layer, lw, layer, depth, ts, tk=None, stacked=None):
    n, s, _ = x.shape
    with_kv = tk is not None
    grid = (n, s // ts)
    tok = lambda w: pl.BlockSpec((1, ts, w), lambda b, i: (b, i, 0))

    def lay(rows, w, tiled):
        slab, first = (None, layer) if stacked is not None else (depth, 0)
        return pl.BlockSpec((slab, 1, rows, w),
                            (lambda b, i: (first, b, i, 0)) if tiled else (lambda b, i: (first, b, 0, 0)))
    out_shape = [
        jax.ShapeDtypeStruct((n, s, D_CONV), BF16),
        jax.ShapeDtypeStruct((n, s, D_HEADS_PAD), BF16),
        jax.ShapeDtypeStruct((depth, n, s, KV_LORA), F32),
        jax.ShapeDtypeStruct((depth, n, s, QK_ROPE), F32),
        jax.ShapeDtypeStruct((depth, n, CONV_W - 1, D_CONV), F32),
    ]
    in_specs = [
        tok(D_MODEL),
        pl.BlockSpec((4, ts, LANES), lambda b, i: (0, i, 0)),
        pl.BlockSpec((None, 1, CONV_W - 1, D_CONV), lambda b, i: (conv_layer, b, 0, 0)),
        _const_spec((1, D_MODEL)),
        _const_spec((D_MODEL, IN_EXT)),
        _const_spec((CONV_W, D_CONV)),
        _const_spec((1, Q_LORA)),
        _const_spec((Q_LORA, D_HEADS_PAD)),
        _const_spec((1, KV_LORA)),
        _const_spec((1, D_CONV)),
    ]
    out_specs = [
        tok(D_CONV), tok(D_HEADS_PAD), lay(ts, KV_LORA, True), lay(ts, QK_ROPE, True),
        lay(CONV_W - 1, D_CONV, False),
    ]
    args = [x, tables, conv_init, lw["g_mix_pre"], lw["w_in"], lw["w_conv"], lw["g_q"], lw["w_q"],
            lw["g_kv"], lw["g_conv_out"]]
    if with_kv:
        in_specs += [_const_spec((KV_LORA, D_HEADS_PAD)), _const_spec((D_ATTN, KV_LORA))]
        args += [lw["w_k"], lw["w_vt"]]
        out_shape += [jax.ShapeDtypeStruct((n, s, D_HEADS_PAD), BF16),
                      jax.ShapeDtypeStruct((n, s // tk, D_ATTN, tk), BF16)]
        out_specs += [tok(D_HEADS_PAD),
                      pl.BlockSpec((1, ts // tk, D_ATTN, tk), lambda b, i: (b, i, 0, 0))]
    aliases = {}
    if stacked is not None:
        for j, arr in enumerate(stacked):
            aliases[len(args)] = 2 + j
            in_specs.append(pl.BlockSpec(memory_space=pl.ANY))
            args.append(arr)
    return pl.pallas_call(
        functools.partial(_in_proj_kernel, with_kv=with_kv, n_alias=len(aliases), layer=layer),
        grid=grid,
        in_specs=in_specs,
        out_specs=tuple(out_specs),
        out_shape=tuple(out_shape),
        input_output_aliases=aliases,
        scratch_shapes=[pltpu.VMEM((SUBLANES, D_CONV), F32)],
        compiler_params=pltpu.CompilerParams(
            dimension_semantics=("arbitrary", "arbitrary"),
            vmem_limit_bytes=VMEM_LIMIT),
        name="in_proj_kv" if with_kv else "in_proj",
    )(*args)


def _attn_prompt_kernel(q_ref, k_ref, vt_ref, g_ref, o_ref, acc_ref, m_ref, ot_ref, *, tq):
    tk = vt_ref.shape[3]
    n_pairs = vt_ref.shape[1]
    items = [(t, hd) for hd in range(N_HEADS) for t in range(Q_TILES)]

    def scores(slot, t, kb, hd, width):
        cols = slice(hd * HEAD_PAD, (hd + 1) * HEAD_PAD)
        k0 = kb * tk
        q0 = slot * tk + t * tq
        return _dot_nt(k_ref[0, k0:k0 + width, cols], q_ref[0, q0:q0 + tq, cols])

    def visible(t, width):
        ck = lax.broadcasted_iota(jnp.int32, (width, tq), 0) // CHUNK
        rq = (lax.broadcasted_iota(jnp.int32, (width, tq), 1) + t * tq) // CHUNK
        return ck <= rq

    def softmax(slot, t, hd, st, vis, first):
        if vis is not None:
            st = jnp.where(vis, st, NEG_BIG)
        m_new = jnp.max(st, axis=0, keepdims=True)
        alpha = None
        if not first:
            m_prev = m_ref[slot, t, hd:hd + 1, :]
            m_new = jnp.maximum(m_prev, m_new)
            alpha = jnp.exp2(m_prev - m_new)
        m_ref[slot, t, hd:hd + 1, :] = m_new
        return alpha, jnp.exp2(st - m_new).astype(BF16)

    def weighted_values(slot, t, hd, kb, alpha, pt):
        width = pt.shape[0]
        lhs = jnp.concatenate([vt_ref[0, kb, hd * V_DIM:(hd + 1) * V_DIM, 0:width],
                               jnp.ones((SUM_ROWS, width), BF16)], axis=0)
        pv = _dot(lhs, pt)
        acc_ref[slot, t, hd] = pv if alpha is None else alpha * acc_ref[slot, t, hd] + pv

    def write_out(slot):
        for t in range(Q_TILES):
            ssq = jnp.zeros((1, tq), F32)
            for hd in range(N_HEADS):
                a = acc_ref[slot, t, hd]
                o = a[0:V_DIM, :] / a[V_DIM:V_DIM + 1, :]
                ssq = ssq + jnp.sum(o * o, axis=0, keepdims=True)
                ot_ref[slot % 2, t, hd * V_DIM:(hd + 1) * V_DIM, :] = o
            rs = lax.rsqrt(ssq * (1.0 / D_ATTN) + EPS)
            q0 = slot * tk + t * tq
            o_ref[0, q0:q0 + tq, :] = ((ot_ref[slot % 2, t] * rs).T * g_ref[...]).astype(BF16)

    work = []
    for slot in range(n_pairs):
        for kb in range(slot + 1):
            widths = (tk, tk) if kb < slot else (tq, tk)
            work += [(slot, kb, t, hd, widths[t], kb == slot) for t, hd in items]

    sts = {}
    for n in range(len(work) + SCORE_LEAD):
        if n < len(work):
            slot, kb, t, hd, width, _ = work[n]
            sts[n] = scores(slot, t, kb, hd, width)
        if n >= SCORE_LEAD:
            slot, kb, t, hd, width, masked = work[n - SCORE_LEAD]
            if masked and (t, hd) == items[0]:
                vis = [visible(0, tq), visible(1, tk)]
                if slot > 0:
                    write_out(slot - 1)
            weighted_values(slot, t, hd, kb,
                            *softmax(slot, t, hd, sts.pop(n - SCORE_LEAD), vis[t] if masked else None,
                                     kb == 0))
    write_out(n_pairs - 1)


def _attn_prompt(q, k, vt, g_attn, tq):
    n, s, _ = q.shape
    tk = vt.shape[3]
    n_pairs = s // tk
    assert tk == Q_TILES * tq and s % tk == 0
    row = lambda w: pl.BlockSpec((1, s, w), lambda b: (b, 0, 0))
    return pl.pallas_call(
        functools.partial(_attn_prompt_kernel, tq=tq),
        grid=(n,),
        in_specs=[
            row(D_HEADS_PAD),
            row(D_HEADS_PAD),
            pl.BlockSpec((1, n_pairs, D_ATTN, tk), lambda b: (b, 0, 0, 0)),
            _const_spec((1, D_ATTN)),
        ],
        out_specs=row(D_ATTN),
        out_shape=jax.ShapeDtypeStruct((n, s, D_ATTN), BF16),
        scratch_shapes=[pltpu.VMEM((n_pairs, Q_TILES, N_HEADS, V_DIM + SUM_ROWS, tq), F32),
                        pltpu.VMEM((n_pairs, Q_TILES, N_HEADS, tq), F32),
                        pltpu.VMEM((2, Q_TILES, D_ATTN, tq), F32)],
        compiler_params=pltpu.CompilerParams(
            dimension_semantics=("arbitrary",),
            vmem_limit_bytes=VMEM_LIMIT),
        name="attn_prompt",
    )(q, k, vt, g_attn)


def _attn_sample_kernel(q_ref, cpast_ref, krpast_ref, cnew_ref, krnew_ref, wabs_ref, psel_ref,
                        wuv_ref, g_ref, o_ref):
    l_new = q_ref.shape[1]
    q = q_ref[0]
    ql, qr = [], []
    for hd in range(N_HEADS):
        qh = q[:, hd * HEAD_PAD:(hd + 1) * HEAD_PAD]
        ql.append(_dot(qh, wabs_ref[hd]))
        qr.append(_dot(qh, psel_ref[...]))
    ql = jnp.concatenate(ql, axis=0).astype(BF16)
    qr = jnp.concatenate(qr, axis=0).astype(BF16)
    cp = cpast_ref[0].astype(BF16)
    cn = cnew_ref[0].astype(BF16)
    s_p = _dot_nt(ql, cp) + _dot(qr, krpast_ref[0].astype(BF16))
    s_n = _dot_nt(ql, cn) + _dot_nt(qr, krnew_ref[0].astype(BF16))
    m = jnp.maximum(jnp.max(s_p, axis=-1, keepdims=True), jnp.max(s_n, axis=-1, keepdims=True))
    p_p = jnp.exp2(s_p - m)
    p_n = jnp.exp2(s_n - m)
    l = jnp.sum(p_p, axis=-1, keepdims=True) + jnp.sum(p_n, axis=-1, keepdims=True)
    o_lat = ((_dot(p_p.astype(BF16), cp) + _dot(p_n.astype(BF16), cn)) / l).astype(BF16)
    o = _dot(o_lat[0:l_new, :], wuv_ref[0])
    for hd in range(1, N_HEADS):
        o = o + _dot(o_lat[hd * l_new:(hd + 1) * l_new, :], wuv_ref[hd])
    o_ref[0] = _rms(o, g_ref[...]).astype(BF16)


def _attn_sample(q, c_past, kr_past, c_new, kr_new, layer, lw):
    n, l_new, _ = q.shape
    p_len = c_past.shape[2]
    per_b = lambda rows, w: pl.BlockSpec((1, rows, w), lambda b: (b, 0, 0))
    lay = lambda rows, w: pl.BlockSpec((None, 1, rows, w), lambda b: (layer, b, 0, 0))
    return pl.pallas_call(
        _attn_sample_kernel,
        grid=(n,),
        in_specs=[
            per_b(l_new, D_HEADS_PAD),
            lay(p_len, KV_LORA),
            lay(QK_ROPE, p_len),
            lay(l_new, KV_LORA),
            lay(l_new, QK_ROPE),
            _const_spec((N_HEADS, HEAD_PAD, KV_LORA)),
            _const_spec((HEAD_PAD, QK_ROPE)),
            _const_spec((N_HEADS, KV_LORA, D_ATTN)),
            _const_spec((1, D_ATTN)),
        ],
        out_specs=per_b(l_new, D_ATTN),
        out_shape=jax.ShapeDtypeStruct((n, l_new, D_ATTN), BF16),
        compiler_params=pltpu.CompilerParams(
            dimension_semantics=("arbitrary",),
            vmem_limit_bytes=VMEM_LIMIT),
        name="attn_sample",
    )(q, c_past, kr_past, c_new, kr_new, lw["w_abs"], lw["p_sel"], lw["w_uv"], lw["g_attn"])


def _out_ffn_kernel(x_ref, convn_ref, attn_ref, p_ref, woc_ref, woa_ref, gpost_ref, gffn_ref,
                    wg_ref, wu_ref, wd_ref, gfpost_ref, wpg_ref, wpp_ref, y_ref, act_ref):
    tm = x_ref.shape[0]
    ng = max(1, tm // ROW_GROUP)
    groups = [slice(i * (tm // ng), (i + 1) * (tm // ng)) for i in range(ng)]
    mix = [_dot(convn_ref[r, :], woc_ref[...]) + _dot(attn_ref[r, :], woa_ref[...]) for r in groups]
    x1, h = [], []
    for r, mx in zip(groups, mix):
        x1.append(x_ref[r, :] + _rms(mx, gpost_ref[...]))
        h.append(_rms(x1[-1], gffn_ref[...]).astype(BF16))
    f = []
    gs = tm // ng
    for gi, hr in enumerate(h):
        a_rows = slice((gi % 2) * gs, (gi % 2 + 1) * gs)
        for c in range(D_FF // FF_CHUNK):
            cols = slice(c * FF_CHUNK, (c + 1) * FF_CHUNK)
            g = _dot(hr, wg_ref[:, cols])
            u = _dot(hr, wu_ref[:, cols])
            act_ref[a_rows, cols] = (g * jax.nn.sigmoid(g) * u).astype(BF16)
        f.append(_dot(act_ref[a_rows, :], wd_ref[...]))
    for r, x1r, fr in zip(groups, x1, f):
        pp = _dot(p_ref[r, :].astype(BF16), wpp_ref[...])
        x2 = x1r + _rms(fr, gfpost_ref[...])
        gate = jax.nn.sigmoid(_dot(x2.astype(BF16), wpg_ref[...]))
        y_ref[r, :] = x2 + gate * pp


def _out_ffn(x, convn, attn, p, layer, lw, tm):
    m = x.shape[0]
    tok = lambda w: pl.BlockSpec((tm, w), lambda i: (i, 0))
    return pl.pallas_call(
        _out_ffn_kernel,
        grid=(m // tm,),
        in_specs=[
            tok(D_MODEL), tok(D_CONV), tok(D_ATTN),
            pl.BlockSpec((None, tm, D_PLE), lambda i: (layer, i, 0)),
            _const_spec((D_CONV, D_MODEL)),
            _const_spec((D_ATTN, D_MODEL)),
            _const_spec((1, D_MODEL)),
            _const_spec((1, D_MODEL)),
            _const_spec((D_MODEL, D_FF)),
            _const_spec((D_MODEL, D_FF)),
            _const_spec((D_FF, D_MODEL)),
            _const_spec((1, D_MODEL)),
            _const_spec((D_MODEL, D_MODEL)),
            _const_spec((D_PLE, D_MODEL)),
        ],
        out_specs=tok(D_MODEL),
        out_shape=jax.ShapeDtypeStruct((m, D_MODEL), F32),
        scratch_shapes=[pltpu.VMEM((min(tm, 2 * ROW_GROUP), D_FF), BF16)],
        compiler_params=pltpu.CompilerParams(
            dimension_semantics=("arbitrary",),
            vmem_limit_bytes=VMEM_LIMIT_FFN),
        name="out_ffn",
    )(x, convn, attn, p, lw["w_o_conv"], lw["w_o_attn"], lw["g_mix_post"], lw["g_ffn_pre"],
      lw["w_gate"], lw["w_up"], lw["w_down"], lw["g_ffn_post"], lw["w_ple_gate"], lw["w_ple_proj"])


def _swap_halves(a):
    half = a.shape[-1] // 2
    return jnp.concatenate([a[..., half:], a[..., :half]], axis=-1)


def _layer_weights(i, g_mix_pre, w_in, w_conv, g_q, w_uq, g_kv, w_ukv, g_conv_out, g_attn_out, w_o,
                   g_mix_post, g_ffn_pre, w_ffn_gate, w_ffn_up, w_ffn_down, g_ffn_post,
                   w_ple_proj, w_ple_gate):
    row = lambda g: g[i].reshape(1, -1).astype(F32)
    w_kr = w_in[i][:, OFF_KR:IN_COLS]
    w_in_ext = jnp.concatenate(
        [w_in[i][:, :OFF_KR], jnp.zeros((D_MODEL, QK_NOPE), F32), w_kr, _swap_halves(w_kr)], axis=1)
    wq3 = w_uq[i].reshape(Q_LORA, N_HEADS, QK_DIM)
    w_q = jnp.concatenate(
        [wq3, _swap_halves(wq3[..., QK_NOPE:])], axis=-1).reshape(Q_LORA, D_HEADS_PAD)
    wkv3 = w_ukv[i].reshape(KV_LORA, N_HEADS, QK_NOPE + V_DIM)
    w_uk = wkv3[..., :QK_NOPE]
    w_uv = wkv3[..., QK_NOPE:]
    zpad = jnp.zeros((KV_LORA, N_HEADS, HEAD_PAD - QK_NOPE), F32)
    w_k = jnp.concatenate([w_uk, zpad], axis=-1).reshape(KV_LORA, D_HEADS_PAD)
    w_vt = w_uv.reshape(KV_LORA, D_ATTN).T
    w_abs = jnp.concatenate([w_uk, zpad], axis=-1).transpose(1, 2, 0)
    head_of_col = jnp.arange(D_ATTN) // V_DIM
    w_uv_heads = jnp.where(head_of_col[None, None, :] == jnp.arange(N_HEADS)[:, None, None],
                           w_uv.reshape(1, KV_LORA, D_ATTN), 0.0)
    p_sel = jnp.zeros((HEAD_PAD, QK_ROPE), F32).at[QK_NOPE:QK_DIM, :].set(jnp.eye(QK_ROPE, dtype=F32))
    return dict(
        g_mix_pre=row(g_mix_pre), w_in=w_in_ext.astype(BF16), w_conv=w_conv[i].astype(F32),
        g_q=row(g_q), w_q=w_q.astype(BF16), g_kv=row(g_kv), w_k=w_k.astype(BF16), w_vt=w_vt.astype(BF16),
        g_conv_out=row(g_conv_out), g_attn=row(g_attn_out),
        w_abs=w_abs.astype(BF16), w_uv=w_uv_heads.astype(BF16), p_sel=p_sel.astype(BF16),
        w_o_conv=w_o[i][:D_CONV].astype(BF16), w_o_attn=w_o[i][D_CONV:].astype(BF16),
        g_mix_post=row(g_mix_post), g_ffn_pre=row(g_ffn_pre),
        w_gate=w_ffn_gate[i].astype(BF16), w_up=w_ffn_up[i].astype(BF16), w_down=w_ffn_down[i].astype(BF16),
        g_ffn_post=row(g_ffn_post), w_ple_gate=w_ple_gate[i].astype(BF16), w_ple_proj=w_ple_proj[i].astype(BF16),
    )


def _rope_tables(pos):
    half = QK_ROPE // 2
    inv = ROPE_THETA ** (-jnp.arange(half, dtype=F32) / half)
    ang = pos.astype(F32)[:, None] * inv[None, :]
    cos, sin = jnp.cos(ang), jnp.sin(ang)
    n = pos.shape[0]
    cc = jnp.concatenate([cos, cos], axis=1)
    ss = jnp.concatenate([-sin, sin], axis=1)
    z_lo = jnp.zeros((n, QK_NOPE), F32)
    z_hi = jnp.zeros((n, HEAD_PAD - QK_DIM), F32)
    c_q = jnp.concatenate([jnp.ones((n, QK_NOPE), F32), cc, z_hi], axis=1) * Q_PRESCALE
    s_q = jnp.concatenate([z_lo, ss, z_hi], axis=1)
    c_k = jnp.concatenate([z_lo, cc, z_hi], axis=1)
    return jnp.stack([c_q, s_q * Q_PRESCALE, c_k, s_q])


def _pick_tile(n, cap):
    t = min(n, cap)
    while n % t:
        t //= 2
    return t


def kernel(x_prompt, x_sample, cache_kv_latent, cache_k_rope, state_conv, p_prompt, p_sample, g_mix_pre, w_in, w_conv, g_q, w_uq, g_kv, w_ukv, g_conv_out, g_attn_out, w_o, g_mix_post, g_ffn_pre, w_ffn_gate, w_ffn_up, w_ffn_down, g_ffn_post, w_ple_proj, w_ple_gate):
    depth = w_in.shape[0]
    n_p, s_p, _ = x_prompt.shape
    n_d, s_d, _ = x_sample.shape
    past_len = cache_kv_latent.shape[2]
    tab_p = _rope_tables(jnp.arange(s_p))
    tab_d = _rope_tables(past_len + jnp.arange(s_d))
    conv_zero = jnp.zeros((1, n_p, CONV_W - 1, D_CONV), F32)
    ts_p = _pick_tile(s_p, SEQ_TILE)
    tq = _pick_tile(s_p, Q_TILE)
    tm_p = _pick_tile(n_p * s_p, TOKEN_TILE)
    tm_d = _pick_tile(n_d * s_d, TOKEN_TILE)
    pp_flat = p_prompt.reshape(depth, n_p * s_p, D_PLE)
    pd_flat = p_sample.reshape(depth, n_d * s_d, D_PLE)

    cache_kr_t = jnp.swapaxes(cache_k_rope, 2, 3)
    xp, xd = x_prompt, x_sample
    stack_p = stack_d = None
    for i in range(depth):
        lw = _layer_weights(i, g_mix_pre, w_in, w_conv, g_q, w_uq, g_kv, w_ukv, g_conv_out,
                            g_attn_out, w_o, g_mix_post, g_ffn_pre, w_ffn_gate, w_ffn_up,
                            w_ffn_down, g_ffn_post, w_ple_proj, w_ple_gate)
        convn, q, *stack_p, k, vt = _in_proj(xp, tab_p, conv_zero, 0, lw, i, depth, ts_p,
                                             tk=Q_TILES * tq, stacked=stack_p)
        attn = _attn_prompt(q, k, vt, lw["g_attn"], tq)
        xp = _out_ffn(xp.reshape(n_p * s_p, D_MODEL), convn.reshape(n_p * s_p, D_CONV),
                      attn.reshape(n_p * s_p, D_ATTN), pp_flat, i, lw, tm_p).reshape(n_p, s_p, D_MODEL)
        convn, q, *stack_d = _in_proj(xd, tab_d, state_conv, i, lw, i, depth, s_d, stacked=stack_d)
        attn = _attn_sample(q, cache_kv_latent, cache_kr_t, stack_d[0], stack_d[1], i, lw)
        xd = _out_ffn(xd.reshape(n_d * s_d, D_MODEL), convn.reshape(n_d * s_d, D_CONV),
                      attn.reshape(n_d * s_d, D_ATTN), pd_flat, i, lw, tm_d).reshape(n_d, s_d, D_MODEL)

    return (xp, xd, *stack_p, *stack_d)
```

```python
import functools

import jax
import jax.numpy as jnp
from jax import lax
from jax.experimental import pallas as pl
from jax.experimental.pallas import tpu as pltpu

D_MODEL = 1024
CHUNK = 64
D_PLE = 256
D_CONV = 512
CONV_W = 3
V_DIM = 64
QK_NOPE = 64
QK_ROPE = 32
QK_DIM = QK_NOPE + QK_ROPE
N_HEADS = 8
D_ATTN = N_HEADS * V_DIM
Q_LORA = 768
KV_LORA = 256
D_FF = 2816
ROPE_THETA = 10000.0
EPS = 1e-6
SCALE = QK_DIM ** -0.5

OFF_C = D_CONV
OFF_X = 2 * D_CONV
OFF_Q = 3 * D_CONV
OFF_KV = OFF_Q + Q_LORA
OFF_KR = OFF_KV + KV_LORA
IN_COLS = OFF_KR + QK_ROPE

LANES = 128
SUBLANES = 8
HEAD_PAD = LANES
D_HEADS_PAD = N_HEADS * HEAD_PAD
IN_EXT = OFF_KR + LANES
FF_CHUNK = 256
NEG_BIG = -1e30
Q_PRESCALE = SCALE * 1.4426950408889634
Q_TILES = 2
SCORE_LEAD = 3
ROW_GROUP = 256
SUM_ROWS = 16
SEQ_TILE = 1024
Q_TILE = 256
TOKEN_TILE = 1024
VMEM_LIMIT = 48 * 1024 * 1024
VMEM_LIMIT_FFN = 56 * 1024 * 1024

BF16 = jnp.bfloat16
F32 = jnp.float32


def _rms(x, g):
    ms = jnp.mean(x * x, axis=-1, keepdims=True)
    return x * lax.rsqrt(ms + EPS) * g


def _dot(a, b):
    return jnp.dot(a, b, preferred_element_type=F32)


def _dot_nt(a, b):
    return lax.dot_general(a, b, (((1,), (1,)), ((), ())), preferred_element_type=F32)


def _const_spec(shape):
    nd = len(shape)
    return pl.BlockSpec(shape, lambda *_: (0,) * nd, pipeline_mode=pl.Buffered(1))


def _in_proj_kernel(x_ref, tab_ref, cinit_ref, gpre_ref, win_ref, wconv_ref, gq_ref, wq_ref,
                    gkv_ref, gconv_ref, *rest, with_kv, n_alias, layer):
    if with_kv:
        wk_ref, wvt_ref = rest[:2]
        rest = rest[2:]
    rest = rest[n_alias:]
    if with_kv:
        convn_ref, q_ref, lat_ref, kr_ref, newconv_ref, k_ref, vt_ref, carry_ref = rest
    else:
        convn_ref, q_ref, lat_ref, kr_ref, newconv_ref, carry_ref = rest
    ts = x_ref.shape[1]
    s_idx = pl.program_id(1)

    def store_layered(ref, rows, val):
        if n_alias:
            ref[0, rows, :] = val
        else:
            for d in range(ref.shape[0]):
                ref[d, 0, rows, :] = val if d == layer else jnp.zeros_like(val)

    @pl.when(s_idx == 0)
    def _():
        carry_ref[0:2, :] = cinit_ref[0]

    ng = max(1, ts // ROW_GROUP)
    gs = ts // ng
    groups = [slice(i * gs, (i + 1) * gs) for i in range(ng)]
    z = []
    for r in groups:
        h = _rms(x_ref[0, r, :], gpre_ref[...]).astype(BF16)
        z.append([_dot(h, win_ref[:, a:b]) for a, b in
                  ((0, OFF_C), (OFF_C, OFF_X), (OFF_X, OFF_Q), (OFF_Q, OFF_KV), (OFF_KV, OFF_KR),
                   (OFF_KR, IN_EXT))])
    cat = lambda j: z[0][j] if ng == 1 else jnp.concatenate([zg[j] for zg in z], axis=0)

    assert CONV_W == 3
    gb = cat(0)
    u = cat(1) * cat(2)
    row = lax.broadcasted_iota(jnp.int32, (ts, 1), 0)
    c0 = carry_ref[0:1, :]
    c1 = carry_ref[1:2, :]
    u_m1 = jnp.where(row == 0, c1, pltpu.roll(u, 1, axis=0))
    u_m2 = pltpu.roll(u, 2, axis=0)
    u_m2 = jnp.where(row == 0, c0, jnp.where(row == 1, c1, u_m2))
    conv = u_m2 * wconv_ref[0:1, :] + u_m1 * wconv_ref[1:2, :] + u * wconv_ref[2:3, :]
    convn_ref[0] = _rms(gb * conv, gconv_ref[...]).astype(BF16)
    tail = u[ts - 2:ts, :]
    carry_ref[0:2, :] = tail
    store_layered(newconv_ref, slice(None), tail)

    for gi, r in enumerate(groups):
        c_q = tab_ref[0, r, :]
        s_q = tab_ref[1, r, :]
        c_k = tab_ref[2, r, :]
        s_k = tab_ref[3, r, :]

        cqn = _rms(z[gi][3], gq_ref[...]).astype(BF16)
        qa = _dot(cqn, wq_ref[...])
        for hd in range(N_HEADS):
            blk = qa[:, hd * HEAD_PAD:(hd + 1) * HEAD_PAD]
            rot = blk * c_q + pltpu.roll(blk, HEAD_PAD - QK_ROPE, axis=1) * s_q
            q_ref[0, r, hd * HEAD_PAD:(hd + 1) * HEAD_PAD] = rot.astype(BF16)

        c_kv = _rms(z[gi][4], gkv_ref[...])
        store_layered(lat_ref, r, c_kv)
        ckvb = c_kv.astype(BF16)
        zkr = z[gi][5]
        kr_blk = zkr * c_k + pltpu.roll(zkr, HEAD_PAD - QK_ROPE, axis=1) * s_k
        store_layered(kr_ref, r, kr_blk[:, QK_NOPE:QK_DIM])
        if with_kv:
            kn = _dot(ckvb, wk_ref[...])
            for hd in range(N_HEADS):
                k_ref[0, r, hd * HEAD_PAD:(hd + 1) * HEAD_PAD] = (
                    kn[:, hd * HEAD_PAD:(hd + 1) * HEAD_PAD] + kr_blk).astype(BF16)
            vt = _dot_nt(wvt_ref[...], ckvb).astype(BF16)
            tk = vt_ref.shape[3]
            start = gi * gs
            vt_ref[0, start // tk, :, start % tk:start % tk + gs] = vt


def _in_proj(x, tables, conv_init, conv_layer, lw, layer, depth, ts, tk=None, stacked=None):
    n, s, _ = x.shape
    with_kv = tk is not None
    grid = (n, s // ts)
    tok = lambda w: pl.BlockSpec((1, ts, w), lambda b, i: (b, i, 0))

    def lay(rows, w, tiled):
        slab, first = (None, layer) if stacked is not None else (depth, 0)
        return pl.BlockSpec((slab, 1, rows, w),
                            (lambda b, i: (first, b, i, 0)) if tiled else (lambda b, i: (first, b, 0, 0)))
    out_shape = [
        jax.ShapeDtypeStruct((n, s, D_CONV), BF16),
        jax.ShapeDtypeStruct((n, s, D_HEADS_PAD), BF16),
        jax.ShapeDtypeStruct((depth, n, s, KV_LORA), F32),
        jax.ShapeDtypeStruct((depth, n, s, QK_ROPE), F32),
        jax.ShapeDtypeStruct((depth, n, CONV_W - 1, D_CONV), F32),
    ]
    in_specs = [
        tok(D_MODEL),
        pl.BlockSpec((4, ts, LANES), lambda b, i: (0, i, 0)),
        pl.BlockSpec((None, 1, CONV_W - 1, D_CONV), lambda b, i: (conv_layer, b, 0, 0)),
        _const_spec((1, D_MODEL)),
        _const_spec((D_MODEL, IN_EXT)),
        _const_spec((CONV_W, D_CONV)),
        _const_spec((1, Q_LORA)),
        _const_spec((Q_LORA, D_HEADS_PAD)),
        _const_spec((1, KV_LORA)),
        _const_spec((1, D_CONV)),
    ]
    out_specs = [
        tok(D_CONV), tok(D_HEADS_PAD), lay(ts, KV_LORA, True), lay(ts, QK_ROPE, True),
        lay(CONV_W - 1, D_CONV, False),
    ]
    args = [x, tables, conv_init, lw["g_mix_pre"], lw["w_in"], lw["w_conv"], lw["g_q"], lw["w_q"],
            lw["g_kv"], lw["g_conv_out"]]
    if with_kv:
        in_specs += [_const_spec((KV_LORA, D_HEADS_PAD)), _const_spec((D_ATTN, KV_LORA))]
        args += [lw["w_k"], lw["w_vt"]]
        out_shape += [jax.ShapeDtypeStruct((n, s, D_HEADS_PAD), BF16),
                      jax.ShapeDtypeStruct((n, s // tk, D_ATTN, tk), BF16)]
        out_specs += [tok(D_HEADS_PAD),
                      pl.BlockSpec((1, ts // tk, D_ATTN, tk), lambda b, i: (b, i, 0, 0))]
    aliases = {}
    if stacked is not None:
        for j, arr in enumerate(stacked):
            aliases[len(args)] = 2 + j
            in_specs.append(pl.BlockSpec(memory_space=pl.ANY))
            args.append(arr)
    return pl.pallas_call(
        functools.partial(_in_proj_kernel, with_kv=with_kv, n_alias=len(aliases), layer=layer),
        grid=grid,
        in_specs=in_specs,
        out_specs=tuple(out_specs),
        out_shape=tuple(out_shape),
        input_output_aliases=aliases,
        scratch_shapes=[pltpu.VMEM((SUBLANES, D_CONV), F32)],
        compiler_params=pltpu.CompilerParams(
            dimension_semantics=("arbitrary", "arbitrary"),
            vmem_limit_bytes=VMEM_LIMIT),
        name="in_proj_kv" if with_kv else "in_proj",
    )(*args)


def _attn_prompt_kernel(q_ref, k_ref, vt_ref, g_ref, o_ref, acc_ref, m_ref, ot_ref, *, tq):
    tk = vt_ref.shape[3]
    n_pairs = vt_ref.shape[1]
    items = [(t, hd) for hd in range(N_HEADS) for t in range(Q_TILES)]
    ones_rows = jnp.ones((SUM_ROWS, tk), BF16)

    def scores(slot, t, kb, hd, width):
        cols = slice(hd * HEAD_PAD, (hd + 1) * HEAD_PAD)
        k0 = kb * tk
        q0 = slot * tk + t * tq
        return _dot_nt(k_ref[0, k0:k0 + width, cols], q_ref[0, q0:q0 + tq, cols])

    def visible(t, width):
        ck = lax.broadcasted_iota(jnp.int32, (width, tq), 0) // CHUNK
        rq = (lax.broadcasted_iota(jnp.int32, (width, tq), 1) + t * tq) // CHUNK
        return ck <= rq

    def softmax(slot, t, hd, st, vis, first):
        if vis is not None:
            st = jnp.where(vis, st, NEG_BIG)
        m_new = jnp.max(st, axis=0, keepdims=True)
        alpha = None
        if not first:
            m_prev = m_ref[slot, t, hd:hd + 1, :]
            m_new = jnp.maximum(m_prev, m_new)
            alpha = jnp.exp2(m_prev - m_new)
        m_ref[slot, t, hd:hd + 1, :] = m_new
        return alpha, jnp.exp2(st - m_new).astype(BF16)

    def weighted_values(slot, t, hd, kb, alpha, pt):
        width = pt.shape[0]
        lhs = jnp.concatenate([vt_ref[0, kb, hd * V_DIM:(hd + 1) * V_DIM, 0:width],
                               ones_rows[:, 0:width]], axis=0)
        pv = _dot(lhs, pt)
        acc_ref[slot, t, hd] = pv if alpha is None else alpha * acc_ref[slot, t, hd] + pv

    def write_out(slot):
        for t in range(Q_TILES):
            ssq = jnp.zeros((1, tq), F32)
            for hd in range(N_HEADS):
                a = acc_ref[slot, t, hd]
                o = a[0:V_DIM, :] / a[V_DIM:V_DIM + 1, :]
                ssq = ssq + jnp.sum(o * o, axis=0, keepdims=True)
                ot_ref[slot % 2, t, hd * V_DIM:(hd + 1) * V_DIM, :] = o
            rs = lax.rsqrt(ssq * (1.0 / D_ATTN) + EPS)
            q0 = slot * tk + t * tq
            o_ref[0, q0:q0 + tq, :] = ((ot_ref[slot % 2, t] * rs).T * g_ref[...]).astype(BF16)

    work = []
    for slot in range(n_pairs):
        for kb in range(slot + 1):
            widths = (tk, tk) if kb < slot else (tq, tk)
            work += [(slot, kb, t, hd, widths[t], kb == slot) for t, hd in items]

    sts = {}
    for n in range(len(work) + SCORE_LEAD):
        if n < len(work):
            slot, kb, t, hd, width, _ = work[n]
            sts[n] = scores(slot, t, kb, hd, width)
        if n >= SCORE_LEAD:
            slot, kb, t, hd, width, masked = work[n - SCORE_LEAD]
            if masked and (t, hd) == items[0]:
                vis = [visible(0, tq), visible(1, tk)]
                if slot > 0:
                    write_out(slot - 1)
            weighted_values(slot, t, hd, kb,
                            *softmax(slot, t, hd, sts.pop(n - SCORE_LEAD), vis[t] if masked else None,
                                     kb == 0))
    write_out(n_pairs - 1)


def _attn_prompt(q, k, vt, g_attn, tq):
    n, s, _ = q.shape
    tk = vt.shape[3]
    n_pairs = s // tk
    assert tk == Q_TILES * tq and s % tk == 0
    row = lambda w: pl.BlockSpec((1, s, w), lambda b: (b, 0, 0))
    return pl.pallas_call(
        functools.partial(_attn_prompt_kernel, tq=tq),
        grid=(n,),
        in_specs=[
            row(D_HEADS_PAD),
            row(D_HEADS_PAD),
            pl.BlockSpec((1, n_pairs, D_ATTN, tk), lambda b: (b, 0, 0, 0)),
            _const_spec((1, D_ATTN)),
        ],
        out_specs=row(D_ATTN),
        out_shape=jax.ShapeDtypeStruct((n, s, D_ATTN), BF16),
        scratch_shapes=[pltpu.VMEM((n_pairs, Q_TILES, N_HEADS, V_DIM + SUM_ROWS, tq), F32),
                        pltpu.VMEM((n_pairs, Q_TILES, N_HEADS, tq), F32),
                        pltpu.VMEM((2, Q_TILES, D_ATTN, tq), F32)],
        compiler_params=pltpu.CompilerParams(
            dimension_semantics=("arbitrary",),
            vmem_limit_bytes=VMEM_LIMIT),
        name="attn_prompt",
    )(q, k, vt, g_attn)


def _attn_sample_kernel(q_ref, cpast_ref, krpast_ref, cnew_ref, krnew_ref, wabs_ref, psel_ref,
                        wuv_ref, g_ref, o_ref):
    l_new = q_ref.shape[1]
    q = q_ref[0]
    ql, qr = [], []
    for hd in range(N_HEADS):
        qh = q[:, hd * HEAD_PAD:(hd + 1) * HEAD_PAD]
        ql.append(_dot(qh, wabs_ref[hd]))
        qr.append(_dot(qh, psel_ref[...]))
    ql = jnp.concatenate(ql, axis=0).astype(BF16)
    qr = jnp.concatenate(qr, axis=0).astype(BF16)
    cp = cpast_ref[0].astype(BF16)
    cn = cnew_ref[0].astype(BF16)
    s_p = _dot_nt(ql, cp) + _dot(qr, krpast_ref[0].astype(BF16))
    s_n = _dot_nt(ql, cn) + _dot_nt(qr, krnew_ref[0].astype(BF16))
    m = jnp.maximum(jnp.max(s_p, axis=-1, keepdims=True), jnp.max(s_n, axis=-1, keepdims=True))
    p_p = jnp.exp2(s_p - m)
    p_n = jnp.exp2(s_n - m)
    l = jnp.sum(p_p, axis=-1, keepdims=True) + jnp.sum(p_n, axis=-1, keepdims=True)
    o_lat = ((_dot(p_p.astype(BF16), cp) + _dot(p_n.astype(BF16), cn)) / l).astype(BF16)
    o = _dot(o_lat[0:l_new, :], wuv_ref[0])
    for hd in range(1, N_HEADS):
        o = o + _dot(o_lat[hd * l_new:(hd + 1) * l_new, :], wuv_ref[hd])
    o_ref[0] = _rms(o, g_ref[...]).astype(BF16)


def _attn_sample(q, c_past, kr_past, c_new, kr_new, layer, lw):
    n, l_new, _ = q.shape
    p_len = c_past.shape[2]
    per_b = lambda rows, w: pl.BlockSpec((1, rows, w), lambda b: (b, 0, 0))
    lay = lambda rows, w: pl.BlockSpec((None, 1, rows, w), lambda b: (layer, b, 0, 0))
    return pl.pallas_call(
        _attn_sample_kernel,
        grid=(n,),
        in_specs=[
            per_b(l_new, D_HEADS_PAD),
            lay(p_len, KV_LORA),
            lay(QK_ROPE, p_len),
            lay(l_new, KV_LORA),
            lay(l_new, QK_ROPE),
            _const_spec((N_HEADS, HEAD_PAD, KV_LORA)),
            _const_spec((HEAD_PAD, QK_ROPE)),
            _const_spec((N_HEADS, KV_LORA, D_ATTN)),
            _const_spec((1, D_ATTN)),
        ],
        out_specs=per_b(l_new, D_ATTN),
        out_shape=jax.ShapeDtypeStruct((n, l_new, D_ATTN), BF16),
        compiler_params=pltpu.CompilerParams(
            dimension_semantics=("arbitrary",),
            vmem_limit_bytes=VMEM_LIMIT),
        name="attn_sample",
    )(q, c_past, kr_past, c_new, kr_new, lw["w_abs"], lw["p_sel"], lw["w_uv"], lw["g_attn"])


def _out_ffn_kernel(x_ref, convn_ref, attn_ref, p_ref, woc_ref, woa_ref, gpost_ref, gffn_ref,
                    wg_ref, wu_ref, wd_ref, gfpost_ref, wpg_ref, wpp_ref, y_ref, act_ref):
    tm = x_ref.shape[0]
    ng = max(1, tm // ROW_GROUP)
    groups = [slice(i * (tm // ng), (i + 1) * (tm // ng)) for i in range(ng)]
    mix = [_dot(convn_ref[r, :], woc_ref[...]) + _dot(attn_ref[r, :], woa_ref[...]) for r in groups]
    x1, h = [], []
    for r, mx in zip(groups, mix):
        x1.append(x_ref[r, :] + _rms(mx, gpost_ref[...]))
        h.append(_rms(x1[-1], gffn_ref[...]).astype(BF16))
    f = []
    gs = tm // ng
    for gi, hr in enumerate(h):
        a_rows = slice((gi % 2) * gs, (gi % 2 + 1) * gs)
        for c in range(D_FF // FF_CHUNK):
            cols = slice(c * FF_CHUNK, (c + 1) * FF_CHUNK)
            g = _dot(hr, wg_ref[:, cols])
            u = _dot(hr, wu_ref[:, cols])
            act_ref[a_rows, cols] = (g * jax.nn.sigmoid(g) * u).astype(BF16)
        f.append(_dot(act_ref[a_rows, :], wd_ref[...]))
    for r, x1r, fr in zip(groups, x1, f):
        pp = _dot(p_ref[r, :].astype(BF16), wpp_ref[...])
        x2 = x1r + _rms(fr, gfpost_ref[...])
        gate = jax.nn.sigmoid(_dot(x2.astype(BF16), wpg_ref[...]))
        y_ref[r, :] = x2 + gate * pp


def _out_ffn(x, convn, attn, p, layer, lw, tm):
    m = x.shape[0]
    tok = lambda w: pl.BlockSpec((tm, w), lambda i: (i, 0))
    return pl.pallas_call(
        _out_ffn_kernel,
        grid=(m // tm,),
        in_specs=[
            tok(D_MODEL), tok(D_CONV), tok(D_ATTN),
            pl.BlockSpec((None, tm, D_PLE), lambda i: (layer, i, 0)),
            _const_spec((D_CONV, D_MODEL)),
            _const_spec((D_ATTN, D_MODEL)),
            _const_spec((1, D_MODEL)),
            _const_spec((1, D_MODEL)),
            _const_spec((D_MODEL, D_FF)),
            _const_spec((D_MODEL, D_FF)),
            _const_spec((D_FF, D_MODEL)),
            _const_spec((1, D_MODEL)),
            _const_spec((D_MODEL, D_MODEL)),
            _const_spec((D_PLE, D_MODEL)),
        ],
        out_specs=tok(D_MODEL),
        out_shape=jax.ShapeDtypeStruct((m, D_MODEL), F32),
        scratch_shapes=[pltpu.VMEM((min(tm, 2 * ROW_GROUP), D_FF), BF16)],
        compiler_params=pltpu.CompilerParams(
            dimension_semantics=("arbitrary",),
            vmem_limit_bytes=VMEM_LIMIT_FFN),
        name="out_ffn",
    )(x, convn, attn, p, lw["w_o_conv"], lw["w_o_attn"], lw["g_mix_post"], lw["g_ffn_pre"],
      lw["w_gate"], lw["w_up"], lw["w_down"], lw["g_ffn_post"], lw["w_ple_gate"], lw["w_ple_proj"])


def _swap_halves(a):
    half = a.shape[-1] // 2
    return jnp.concatenate([a[..., half:], a[..., :half]], axis=-1)


def _layer_weights(i, g_mix_pre, w_in, w_conv, g_q, w_uq, g_kv, w_ukv, g_conv_out, g_attn_out, w_o,
                   g_mix_post, g_ffn_pre, w_ffn_gate, w_ffn_up, w_ffn_down, g_ffn_post,
                   w_ple_proj, w_ple_gate):
    row = lambda g: g[i].reshape(1, -1).astype(F32)
    w_kr = w_in[i][:, OFF_KR:IN_COLS]
    w_in_ext = jnp.concatenate(
        [w_in[i][:, :OFF_KR], jnp.zeros((D_MODEL, QK_NOPE), F32), w_kr, _swap_halves(w_kr)], axis=1)
    wq3 = w_uq[i].reshape(Q_LORA, N_HEADS, QK_DIM)
    w_q = jnp.concatenate(
        [wq3, _swap_halves(wq3[..., QK_NOPE:])], axis=-1).reshape(Q_LORA, D_HEADS_PAD)
    wkv3 = w_ukv[i].reshape(KV_LORA, N_HEADS, QK_NOPE + V_DIM)
    w_uk = wkv3[..., :QK_NOPE]
    w_uv = wkv3[..., QK_NOPE:]
    zpad = jnp.zeros((KV_LORA, N_HEADS, HEAD_PAD - QK_NOPE), F32)
    w_k = jnp.concatenate([w_uk, zpad], axis=-1).reshape(KV_LORA, D_HEADS_PAD)
    w_vt = w_uv.reshape(KV_LORA, D_ATTN).T
    w_abs = jnp.concatenate([w_uk, zpad], axis=-1).transpose(1, 2, 0)
    head_of_col = jnp.arange(D_ATTN) // V_DIM
    w_uv_heads = jnp.where(head_of_col[None, None, :] == jnp.arange(N_HEADS)[:, None, None],
                           w_uv.reshape(1, KV_LORA, D_ATTN), 0.0)
    p_sel = jnp.zeros((HEAD_PAD, QK_ROPE), F32).at[QK_NOPE:QK_DIM, :].set(jnp.eye(QK_ROPE, dtype=F32))
    return dict(
        g_mix_pre=row(g_mix_pre), w_in=w_in_ext.astype(BF16), w_conv=w_conv[i].astype(F32),
        g_q=row(g_q), w_q=w_q.astype(BF16), g_kv=row(g_kv), w_k=w_k.astype(BF16), w_vt=w_vt.astype(BF16),
        g_conv_out=row(g_conv_out), g_attn=row(g_attn_out),
        w_abs=w_abs.astype(BF16), w_uv=w_uv_heads.astype(BF16), p_sel=p_sel.astype(BF16),
        w_o_conv=w_o[i][:D_CONV].astype(BF16), w_o_attn=w_o[i][D_CONV:].astype(BF16),
        g_mix_post=row(g_mix_post), g_ffn_pre=row(g_ffn_pre),
        w_gate=w_ffn_gate[i].astype(BF16), w_up=w_ffn_up[i].astype(BF16), w_down=w_ffn_down[i].astype(BF16),
        g_ffn_post=row(g_ffn_post), w_ple_gate=w_ple_gate[i].astype(BF16), w_ple_proj=w_ple_proj[i].astype(BF16),
    )


def _rope_tables(pos):
    half = QK_ROPE // 2
    inv = ROPE_THETA ** (-jnp.arange(half, dtype=F32) / half)
    ang = pos.astype(F32)[:, None] * inv[None, :]
    cos, sin = jnp.cos(ang), jnp.sin(ang)
    n = pos.shape[0]
    cc = jnp.concatenate([cos, cos], axis=1)
    ss = jnp.concatenate([-sin, sin], axis=1)
    z_lo = jnp.zeros((n, QK_NOPE), F32)
    z_hi = jnp.zeros((n, HEAD_PAD - QK_DIM), F32)
    c_q = jnp.concatenate([jnp.ones((n, QK_NOPE), F32), cc, z_hi], axis=1) * Q_PRESCALE
    s_q = jnp.concatenate([z_lo, ss, z_hi], axis=1)
    c_k = jnp.concatenate([z_lo, cc, z_hi], axis=1)
    return jnp.stack([c_q, s_q * Q_PRESCALE, c_k, s_q])


def _pick_tile(n, cap):
    t = min(n, cap)
    while n % t:
        t //= 2
    return t


def kernel(x_prompt, x_sample, cache_kv_latent, cache_k_rope, state_conv, p_prompt, p_sample, g_mix_pre, w_in, w_conv, g_q, w_uq, g_kv, w_ukv, g_conv_out, g_attn_out, w_o, g_mix_post, g_ffn_pre, w_ffn_gate, w_ffn_up, w_ffn_down, g_ffn_post, w_ple_proj, w_ple_gate):
    depth = w_in.shape[0]
    n_p, s_p, _ = x_prompt.shape
    n_d, s_d, _ = x_sample.shape
    past_len = cache_kv_latent.shape[2]
    tab_p = _rope_tables(jnp.arange(s_p))
    tab_d = _rope_tables(past_len + jnp.arange(s_d))
    conv_zero = jnp.zeros((1, n_p, CONV_W - 1, D_CONV), F32)
    ts_p = _pick_tile(s_p, SEQ_TILE)
    tq = _pick_tile(s_p, Q_TILE)
    tm_p = _pick_tile(n_p * s_p, TOKEN_TILE)
    tm_d = _pick_tile(n_d * s_d, TOKEN_TILE)
    pp_flat = p_prompt.reshape(depth, n_p * s_p, D_PLE)
    pd_flat = p_sample.reshape(depth, n_d * s_d, D_PLE)

    cache_kr_t = jnp.swapaxes(cache_k_rope, 2, 3)
    xp, xd = x_prompt, x_sample
    stack_p = stack_d = None
    for i in range(depth):
        lw = _layer_weights(i, g_mix_pre, w_in, w_conv, g_q, w_uq, g_kv, w_ukv, g_conv_out,
                            g_attn_out, w_o, g_mix_post, g_ffn_pre, w_ffn_gate, w_ffn_up,
                            w_ffn_down, g_ffn_post, w_ple_proj, w_ple_gate)
        convn, q, *stack_p, k, vt = _in_proj(xp, tab_p, conv_zero, 0, lw, i, depth, ts_p,
                                             tk=Q_TILES * tq, stacked=stack_p)
        attn = _attn_prompt(q, k, vt, lw["g_attn"], tq)
        xp = _out_ffn(xp.reshape(n_p * s_p, D_MODEL), convn.reshape(n_p * s_p, D_CONV),
                      attn.reshape(n_p * s_p, D_ATTN), pp_flat, i, lw, tm_p).reshape(n_p, s_p, D_MODEL)
        convn, q, *stack_d = _in_proj(xd, tab_d, state_conv, i, lw, i, depth, s_d, stacked=stack_d)
        attn = _attn_sample(q, cache_kv_latent, cache_kr_t, stack_d[0], stack_d[1], i, lw)
        xd = _out_ffn(xd.reshape(n_d * s_d, D_MODEL), convn.reshape(n_d * s_d, D_CONV),
                      attn.reshape(n_d * s_d, D_ATTN), pd_flat, i, lw, tm_d).reshape(n_d, s_d, D_MODEL)

    return (xp, xd, *stack_p, *stack_d)
```
